```python
import math
import jax, jax.numpy as jnp
from jax import lax
import numpy as np

D_MODEL = 2048
BATCH = 8
SEQ = 4096
DEPTH = 4

N_META = 16
EPS = 1e-6
D_MIX = D_MODEL
GLA_WIDTH = D_MIX // 4
GLA_HEADS = 4
GLA_DV = GLA_WIDTH // GLA_HEADS
GLA_DK = GLA_DV // 2
GLA_RANK = 16
GLA_TAU = 16.0
GLA_CHUNK = 64
DIFF_WIDTH = D_MIX // 4
DIFF_HEADS = 4
DIFF_DV = DIFF_WIDTH // DIFF_HEADS
DIFF_DQK = DIFF_DV // 2
Q_BLOCK = 128
N_BUCKETS = 32
MAX_DISTANCE = 128
SSD_WIDTH = D_MIX // 2
SSD_HEADDIM = 64
SSD_HEADS = SSD_WIDTH // SSD_HEADDIM
SSD_GROUPS = 2
SSD_STATE = 128
SSD_CONV = 5
SSD_CHUNK = 128
SSD_CONV_DIM = SSD_WIDTH + 2 * SSD_GROUPS * SSD_STATE

IN_SIZES = (
    GLA_HEADS * GLA_DK, GLA_HEADS * GLA_DK, GLA_WIDTH, GLA_WIDTH, 2 * GLA_RANK,
    2 * DIFF_HEADS * DIFF_DQK, 2 * DIFF_HEADS * DIFF_DQK, DIFF_WIDTH, DIFF_WIDTH,
    SSD_WIDTH, SSD_CONV_DIM, 2 * SSD_HEADS,
)
IN_TOTAL = sum(IN_SIZES)

kernel_name = 'hybrid_gla_diffattn_ssd_encoder'


def rmsnorm(x, w):
    xf = x.astype(jnp.float32)
    y = xf * lax.rsqrt(jnp.mean(xf * xf, axis=-1, keepdims=True) + EPS)
    return (y * w.astype(jnp.float32)).astype(x.dtype)


def t5_bucket(rel):
    nb = N_BUCKETS // 2
    max_exact = nb // 2
    ret = jnp.where(rel > 0, nb, 0)
    n = jnp.abs(rel)
    nf = jnp.maximum(n, 1).astype(jnp.float32)
    large = max_exact + (jnp.log(nf / max_exact) / math.log(MAX_DISTANCE / max_exact)
                         * (nb - max_exact)).astype(jnp.int32)
    large = jnp.minimum(large, nb - 1)
    return ret + jnp.where(n < max_exact, n, large)


def bidirectional(scan_fn, fwd_args, bwd_args, chunk):
    pad = chunk - N_META
    L = fwd_args[0].shape[1]

    def padseq(t, front):
        widths = [(0, 0)] * t.ndim
        widths[1] = (pad, 0) if front else (0, pad)
        return jnp.pad(t, widths)

    y_f = scan_fn(*[padseq(t, True) for t in fwd_args])[:, pad:]
    y_b = scan_fn(*[padseq(jnp.flip(t, 1), False) for t in bwd_args])[:, :L]
    return y_f + jnp.flip(y_b, 1)


def gla_chunk_scan(q, k, v, g):
    f32 = jnp.float32
    Bsz, T, H, DK = q.shape
    DV = v.shape[-1]
    C = GLA_CHUNK
    N = T // C
    q, k, g = [t.astype(f32).reshape(Bsz, N, C, H, DK) for t in (q, k, g)]
    v = v.astype(f32).reshape(Bsz, N, C, H, DV)
    b = jnp.cumsum(g, axis=2)
    b_last = b[:, :, -1]
    q_in = q * jnp.exp(b)
    k_in = k * jnp.exp(-b)
    k_out = k * jnp.exp(b_last[:, :, None] - b)
    mask = jnp.tril(jnp.ones((C, C), dtype=bool))
    att = jnp.where(mask, jnp.einsum('bnthk,bnshk->bnhts', q_in, k_in), 0.0)
    o_intra = jnp.einsum('bnhts,bnshv->bnthv', att, v)
    chunk_state = jnp.einsum('bnshk,bnshv->bnhkv', k_out, v)

    def step(S, inp):
        d, cs = inp
        return S * d[..., None] + cs, S

    S0 = jnp.zeros((Bsz, H, DK, DV), f32)
    _, S_prev = lax.scan(step, S0, (jnp.moveaxis(jnp.exp(b_last), 1, 0),
                                    jnp.moveaxis(chunk_state, 1, 0)))
    S_prev = jnp.moveaxis(S_prev, 0, 1)
    o_inter = jnp.einsum('bnthk,bnhkv->bnthv', q_in, S_prev)
    return (o_intra + o_inter).reshape(Bsz, T, H, DV)


def ssd_chunk_scan(x, dt, a, Bm, Cm):
    f32 = jnp.float32
    Bsz, T, H, P = x.shape
    G, N = Bm.shape[2], Bm.shape[3]
    R = H // G
    C = SSD_CHUNK
    Nc = T // C
    xd = (x.astype(f32) * dt.astype(f32)[..., None]).reshape(Bsz, Nc, C, G, R, P)
    a = a.astype(f32).reshape(Bsz, Nc, C, G, R)
    Bm = Bm.astype(f32).reshape(Bsz, Nc, C, G, N)
    Cm = Cm.astype(f32).reshape(Bsz, Nc, C, G, N)
    acs = jnp.cumsum(a, axis=2)
    acs_last = acs[:, :, -1]
    causal = jnp.tril(jnp.ones((C, C), dtype=bool))[:, :, None, None]
    seg = acs[:, :, :, None] - acs[:, :, None, :]
    decay_ts = jnp.exp(jnp.where(causal, seg, -jnp.inf))
    scores = jnp.einsum('bctgn,bcsgn->bctsg', Cm, Bm)[..., None] * decay_ts
    y_diag = jnp.einsum('bctsgr,bcsgrp->bctgrp', scores, xd)
    to_end = jnp.exp(acs_last[:, :, None] - acs)
    states = jnp.einsum('bcsgn,bcsgrp->bcgrpn', Bm, xd * to_end[..., None])

    def step(S, inp):
        d, st = inp
        return S * d[..., None, None] + st, S

    S0 = jnp.zeros((Bsz, G, R, P, N), f32)
    _, S_prev = lax.scan(step, S0, (jnp.moveaxis(jnp.exp(acs_last), 1, 0),
                                    jnp.moveaxis(states, 1, 0)))
    S_prev = jnp.moveaxis(S_prev, 0, 1)
    y_off = jnp.einsum('bctgn,bcgrpn->bctgrp', Cm, S_prev) * jnp.exp(acs)[..., None]
    return (y_diag + y_off).reshape(Bsz, T, H, P)


def depthwise_conv(x, w, b):
    K, Ch = w.shape
    y = lax.conv_general_dilated(x, w[:, None, :].astype(x.dtype), window_strides=(1,),
                                 padding=[((K - 1) // 2, K // 2)],
                                 dimension_numbers=('NWC', 'WIO', 'NWC'),
                                 feature_group_count=Ch)
    return y + b.astype(x.dtype)


def diff_attention(q, k, v, lam, lambda_init, rel_bias, sub_w):
    Bsz, L, H = q.shape[0], q.shape[1], q.shape[2]
    DV = v.shape[-1]
    n_blocks = -(-L // Q_BLOCK)
    Lq = n_blocks * Q_BLOCK
    qp = jnp.pad(q, ((0, 0), (0, Lq - L), (0, 0), (0, 0), (0, 0)))
    qb = jnp.moveaxis(qp.reshape(Bsz, n_blocks, Q_BLOCK, H, 2, DIFF_DQK), 1, 0)
    k_pos = jnp.arange(L, dtype=jnp.int32)
    scale = DIFF_DQK ** -0.5

    def block(args):
        qblk, start = args
        q_pos = start + jnp.arange(Q_BLOCK, dtype=jnp.int32)
        bias = rel_bias[t5_bucket(k_pos[None, :] - q_pos[:, None])]
        bias = jnp.transpose(bias, (2, 0, 1)).astype(jnp.float32)
        s = jnp.einsum('bqhcd,bkhcd->bhcqk', qblk, k).astype(jnp.float32) * scale
        p = jax.nn.softmax(s + bias[None, :, None], axis=-1)
        w = p[:, :, 0] - lam * p[:, :, 1]
        return jnp.einsum('bhqk,bkhv->bqhv', w.astype(v.dtype), v)

    starts = jnp.arange(n_blocks, dtype=jnp.int32) * Q_BLOCK
    out = lax.map(block, (qb, starts))
    out = jnp.moveaxis(out, 0, 1).reshape(Bsz, Lq, H, DV)[:, :L]
    return rmsnorm(out, sub_w) * (1.0 - lambda_init)


def hybrid_layer(h, norm_w, w_in, w_out, gla_wa2, gla_ba, gla_norm_w, diff_lambda,
                 diff_norm_w, conv_w, conv_b, ssd_A_log, ssd_dt_bias, ssd_D, ssd_norm_w,
                 rel_bias, lambda_init):
    Bsz, L, _ = h.shape
    dt_ = h.dtype
    u = rmsnorm(h, norm_w)
    proj = u @ w_in.astype(dt_)
    split_idx = np.cumsum(IN_SIZES)[:-1].tolist()
    (gq, gk, gv, ggate, gcode, dq, dk, dv, dgate, z, xbc, dt_raw) = jnp.split(proj, split_idx, axis=-1)

    gq = gq.reshape(Bsz, L, GLA_HEADS, GLA_DK) * (GLA_DK ** -0.5)
    gk = gk.reshape(Bsz, L, GLA_HEADS, GLA_DK)
    gv = gv.reshape(Bsz, L, GLA_HEADS, GLA_DV)
    gcode = gcode.reshape(Bsz, L, 2, GLA_RANK).astype(jnp.float32)
    glog = jax.nn.log_sigmoid(jnp.einsum('blzr,zrk->blzk', gcode, gla_wa2.astype(jnp.float32))
                              + gla_ba.astype(jnp.float32)) / GLA_TAU
    glog = glog.reshape(Bsz, L, 2, GLA_HEADS, GLA_DK)
    o = bidirectional(gla_chunk_scan, (gq, gk, gv, glog[:, :, 0]), (gq, gk, gv, glog[:, :, 1]), GLA_CHUNK)
    o_gla = rmsnorm(o, gla_norm_w).astype(dt_).reshape(Bsz, L, GLA_WIDTH) * jax.nn.silu(ggate)

    dq = dq.reshape(Bsz, L, DIFF_HEADS, 2, DIFF_DQK)
    dk = dk.reshape(Bsz, L, DIFF_HEADS, 2, DIFF_DQK)
    dv = dv.reshape(Bsz, L, DIFF_HEADS, DIFF_DV)
    lp = diff_lambda.astype(jnp.float32)
    lam = jnp.exp(jnp.sum(lp[0] * lp[1])) - jnp.exp(jnp.sum(lp[2] * lp[3])) + lambda_init
    o = diff_attention(dq, dk, dv, lam, lambda_init, rel_bias, diff_norm_w)
    o_diff = o.astype(dt_).reshape(Bsz, L, DIFF_WIDTH) * jax.nn.silu(dgate)

    xbc = jax.nn.silu(depthwise_conv(xbc, conv_w, conv_b))
    xs, Bm, Cm = jnp.split(xbc, [SSD_WIDTH, SSD_WIDTH + SSD_GROUPS * SSD_STATE], axis=-1)
    xs = xs.reshape(Bsz, L, SSD_HEADS, SSD_HEADDIM)
    Bm = Bm.reshape(Bsz, L, SSD_GROUPS, SSD_STATE)
    Cm = Cm.reshape(Bsz, L, SSD_GROUPS, SSD_STATE)
    dt = jax.nn.softplus(dt_raw.reshape(Bsz, L, 2, SSD_HEADS).astype(jnp.float32)
                         + ssd_dt_bias.astype(jnp.float32))
    a = dt * (-jnp.exp(ssd_A_log.astype(jnp.float32)))
    y = bidirectional(ssd_chunk_scan, (xs, dt[:, :, 0], a[:, :, 0], Bm, Cm),
                      (xs, dt[:, :, 1], a[:, :, 1], Bm, Cm), SSD_CHUNK)
    y = y + xs.astype(jnp.float32) * ssd_D.astype(jnp.float32)[:, None]
    y = y.reshape(Bsz, L, SSD_WIDTH) * jax.nn.silu(z.astype(jnp.float32))
    y = rmsnorm(y.reshape(Bsz, L, SSD_GROUPS, SSD_WIDTH // SSD_GROUPS),
                ssd_norm_w.reshape(SSD_GROUPS, SSD_WIDTH // SSD_GROUPS))
    o_ssd = y.reshape(Bsz, L, SSD_WIDTH).astype(dt_)

    mix = jnp.concatenate([o_gla, o_diff, o_ssd], axis=-1)
    return h + mix @ w_out.astype(dt_)


def setup_inputs(seed: int = 0) -> dict:
    key = jax.random.key(seed)
    ks = jax.random.split(key, 20)
    f32 = jnp.float32
    nrm = lambda k, s: jax.random.normal(k, s, f32)
    x = nrm(ks[0], (BATCH, SEQ, D_MODEL))
    meta_tokens = nrm(ks[1], (N_META, D_MODEL))
    rel_bias = 0.5 * nrm(ks[2], (N_BUCKETS, DIFF_HEADS))
    final_norm_w = 1.0 + 0.02 * nrm(ks[3], (D_MODEL,))
    norm_w = 1.0 + 0.02 * nrm(ks[4], (DEPTH, D_MODEL))
    w_in = nrm(ks[5], (DEPTH, D_MODEL, IN_TOTAL)) * D_MODEL ** -0.5
    w_out = nrm(ks[6], (DEPTH, D_MIX, D_MODEL)) * (0.5 * D_MIX ** -0.5)
    gla_wa2 = nrm(ks[7], (DEPTH, 2, GLA_RANK, GLA_HEADS * GLA_DK)) * GLA_RANK ** -0.5
    gla_ba = 0.1 * nrm(ks[8], (DEPTH, 2, GLA_HEADS * GLA_DK))
    gla_norm_w = 1.0 + 0.02 * nrm(ks[9], (DEPTH, GLA_DV))
    diff_lambda = 0.1 * nrm(ks[10], (DEPTH, 4, DIFF_DQK))
    diff_norm_w = 1.0 + 0.02 * nrm(ks[11], (DEPTH, DIFF_DV))
    conv_w = nrm(ks[12], (DEPTH, SSD_CONV, SSD_CONV_DIM)) * SSD_CONV ** -0.5
    conv_b = 0.01 * nrm(ks[13], (DEPTH, SSD_CONV_DIM))
    ssd_A_log = jnp.log(jax.random.uniform(ks[14], (DEPTH, 2, SSD_HEADS), f32, 1.0, 16.0))
    dt0 = jnp.exp(jax.random.uniform(ks[15], (DEPTH, 2, SSD_HEADS), f32,
                                     math.log(1e-3), math.log(1e-1)))
    ssd_dt_bias = dt0 + jnp.log(-jnp.expm1(-dt0))
    ssd_D = 1.0 + 0.1 * nrm(ks[16], (DEPTH, SSD_HEADS))
    ssd_norm_w = 1.0 + 0.02 * nrm(ks[17], (DEPTH, SSD_WIDTH))
    return {'x': x, 'meta_tokens': meta_tokens, 'rel_bias': rel_bias, 'final_norm_w': final_norm_w,
            'norm_w': norm_w, 'w_in': w_in, 'w_out': w_out, 'gla_wa2': gla_wa2, 'gla_ba': gla_ba,
            'gla_norm_w': gla_norm_w, 'diff_lambda': diff_lambda, 'diff_norm_w': diff_norm_w,
            'conv_w': conv_w, 'conv_b': conv_b, 'ssd_A_log': ssd_A_log, 'ssd_dt_bias': ssd_dt_bias,
            'ssd_D': ssd_D, 'ssd_norm_w': ssd_norm_w}


def reference(x, meta_tokens, rel_bias, final_norm_w, norm_w, w_in, w_out, gla_wa2, gla_ba,
              gla_norm_w, diff_lambda, diff_norm_w, conv_w, conv_b, ssd_A_log, ssd_dt_bias,
              ssd_D, ssd_norm_w):
    Bsz = x.shape[0]
    meta = jnp.broadcast_to(meta_tokens[None].astype(x.dtype), (Bsz, N_META, D_MODEL))
    h = jnp.concatenate([meta, x], axis=1)
    for l in range(DEPTH):
        lambda_init = 0.8 - 0.6 * math.exp(-0.3 * l)
        h = hybrid_layer(h, norm_w[l], w_in[l], w_out[l], gla_wa2[l], gla_ba[l], gla_norm_w[l],
                         diff_lambda[l], diff_norm_w[l], conv_w[l], conv_b[l], ssd_A_log[l],
                         ssd_dt_bias[l], ssd_D[l], ssd_norm_w[l], rel_bias, lambda_init)
    h = rmsnorm(h, final_norm_w)
    return h[:, N_META:]
```

```python
import functools
import math

import numpy as np
import jax
import jax.numpy as jnp
from jax import lax
from jax.experimental import pallas as pl
from jax.experimental.pallas import tpu as pltpu

F32 = jnp.float32
BF16 = jnp.bfloat16

D_MODEL = 2048
DEPTH = 4
N_META = 16
EPS = 1e-6
GLA_HEADS = 4
GLA_DK = 64
GLA_DV = 128
GLA_WIDTH = 512
GLA_RANK = 16
GLA_TAU = 16.0
GLA_CHUNK = 64
DIFF_HEADS = 4
DIFF_DQK = 64
DIFF_DV = 128
DIFF_WIDTH = 512
N_BUCKETS = 32
MAX_DISTANCE = 128
SSD_WIDTH = 1024
SSD_HEADDIM = 64
SSD_HEADS = 16
SSD_GROUPS = 2
SSD_STATE = 128
SSD_CONV = 5
SSD_CHUNK = 128
SSD_CONV_DIM = SSD_WIDTH + 2 * SSD_GROUPS * SSD_STATE

BLK = 128
TOK0 = BLK
PAD = TOK0 - N_META

C_GQ, C_GK, C_GV, C_GG = 0, 256, 512, 1024
C_DQ, C_DK, C_DV = 1536, 2048, 2560
C_Z, C_DG, C_XBC, C_MISC = 3072, 4096, 4608, 6144
N_PROJ = 6272
DT_LANE0 = 2 * GLA_RANK

VMEM_LIMIT = 56 * 1024 * 1024


def _params(sem, limit=VMEM_LIMIT):
    return pltpu.CompilerParams(dimension_semantics=sem, vmem_limit_bytes=limit)


def _silu(x):
    return x / (1.0 + jnp.exp(-x))


def _softplus(x):
    return jnp.maximum(x, 0.0) + jnp.log1p(jnp.exp(-jnp.abs(x)))


def _log_sigmoid(x):
    return jnp.minimum(x, 0.0) - jnp.log1p(jnp.exp(-jnp.abs(x)))


def _bdot(a, b):
    return jnp.dot(a.astype(BF16), b.astype(BF16), preferred_element_type=F32)


def _split3(x):
    hi = x.astype(BF16)
    r = x - hi.astype(F32)
    mid = r.astype(BF16)
    lo = (r - mid.astype(F32)).astype(BF16)
    return hi, mid, lo


def _dot_l3(a, b_exact):
    hi, mid, lo = _split3(a)
    d = functools.partial(jnp.dot, preferred_element_type=F32)
    return d(hi, b_exact) + d(mid, b_exact) + d(lo, b_exact)


def _dot_r3(a_exact, b):
    hi, mid, lo = _split3(b)
    d = functools.partial(jnp.dot, preferred_element_type=F32)
    return d(a_exact, hi) + d(a_exact, mid) + d(a_exact, lo)


def _dot_22(a, b):
    ah = a.astype(BF16)
    al = (a - ah.astype(F32)).astype(BF16)
    bh = b.astype(BF16)
    bl = (b - bh.astype(F32)).astype(BF16)
    d = functools.partial(jnp.dot, preferred_element_type=F32)
    return d(ah, bh) + d(ah, bl) + d(al, bh)


def _iota(shape, dim):
    return lax.broadcasted_iota(jnp.int32, shape, dim)


def _rms(x, w):
    return x * lax.rsqrt(jnp.mean(x * x, axis=-1, keepdims=True) + EPS) * w


def _inproj_kernel(h_ref, nw_ref, w_ref, o_ref, u_scr, *, tm):
    @pl.when(pl.program_id(2) == 0)
    def _():
        x = h_ref[...]
        y = _rms(x, nw_ref[...])
        row = pl.program_id(1) * tm + _iota((tm, 1), 0)
        u_scr[...] = jnp.where(row >= PAD, y, 0.0).astype(BF16)

    o_ref[...] = jnp.dot(u_scr[...], w_ref[...], preferred_element_type=F32)


def _inproj(h, norm_w, w_p):
    B, Lp, D = h.shape
    tm = Lp // 4
    tn = N_PROJ // 7
    return pl.pallas_call(
        functools.partial(_inproj_kernel, tm=tm),
        grid=(B, Lp // tm, N_PROJ // tn),
        in_specs=[
            pl.BlockSpec((None, tm, D), lambda b, i, j: (b, i, 0)),
            pl.BlockSpec((1, D), lambda b, i, j: (0, 0)),
            pl.BlockSpec((D, tn), lambda b, i, j: (0, j)),
        ],
        out_specs=pl.BlockSpec((None, tm, tn), lambda b, i, j: (b, i, j)),
        out_shape=jax.ShapeDtypeStruct((B, Lp, N_PROJ), F32),
        scratch_shapes=[pltpu.VMEM((tm, D), BF16)],
        compiler_params=_params(("arbitrary", "arbitrary", "arbitrary")),
        name="inproj",
    )(h, norm_w.reshape(1, D), w_p)


def _outproj_kernel(h_ref, a_ref, b_ref, c_ref, w_ref, o_ref):
    d = functools.partial(jnp.dot, preferred_element_type=F32)
    acc = d(a_ref[...], w_ref[0:GLA_WIDTH, :])
    acc += d(b_ref[...], w_ref[GLA_WIDTH:GLA_WIDTH + DIFF_WIDTH, :])
    acc += d(c_ref[...], w_ref[GLA_WIDTH + DIFF_WIDTH:, :])
    o_ref[...] = h_ref[...] + acc


def _outproj(h, o_gla, o_diff, o_ssd, w_out):
    B, Lp, D = h.shape
    tm = Lp // 4
    tn = D // 2
    return pl.pallas_call(
        _outproj_kernel,
        grid=(B, Lp // tm, D // tn),
        in_specs=[
            pl.BlockSpec((None, tm, tn), lambda b, i, j: (b, i, j)),
            pl.BlockSpec((None, tm, GLA_WIDTH), lambda b, i, j: (b, i, 0)),
            pl.BlockSpec((None, tm, DIFF_WIDTH), lambda b, i, j: (b, i, 0)),
            pl.BlockSpec((None, tm, SSD_WIDTH), lambda b, i, j: (b, i, 0)),
            pl.BlockSpec((D, tn), lambda b, i, j: (0, j)),
        ],
        out_specs=pl.BlockSpec((None, tm, tn), lambda b, i, j: (b, i, j)),
        out_shape=jax.ShapeDtypeStruct((B, Lp, D), F32),
        compiler_params=_params(("arbitrary", "arbitrary", "arbitrary")),
        name="outproj",
    )(h, o_gla, o_diff, o_ssd, w_out)


def _final_kernel(h_ref, w_ref, o_ref):
    o_ref[...] = _rms(h_ref[...], w_ref[...])


def _final_norm(h, w):
    B, Lp, D = h.shape
    S = Lp - TOK0
    return pl.pallas_call(
        _final_kernel,
        grid=(B, S // BLK),
        in_specs=[
            pl.BlockSpec((None, BLK, D), lambda b, i: (b, i + 1, 0)),
            pl.BlockSpec((1, D), lambda b, i: (0, 0)),
        ],
        out_specs=pl.BlockSpec((None, BLK, D), lambda b, i: (b, i, 0)),
        out_shape=jax.ShapeDtypeStruct((B, S, D), F32),
        compiler_params=_params(("arbitrary", "arbitrary")),
        name="final_norm",
    )(h, w.reshape(1, D))


def _gla_block(q_ref, k_ref, v_ref, gate_ref, misc_ref, wa_ref, ba_ref, nw_ref, o_ref,
               of_scr, s_scr, *, d, blk):
    C = GLA_CHUNK
    row = _iota((BLK, 1), 0)
    valid = (blk * BLK + row) >= PAD
    x = _dot_22(misc_ref[...], wa_ref[d]) + ba_ref[d]
    g = jnp.where(valid, _log_sigmoid(x) * (1.0 / GLA_TAU), 0.0)

    r = _iota((BLK, BLK), 0)
    c = _iota((BLK, BLK), 1)
    same = (r >= C) == (c >= C)
    tri = (same & (c <= r)) if d == 0 else (same & (c >= r))
    b = _dot_r3(jnp.where(tri, 1.0, 0.0).astype(BF16), g)
    first, second = (C - 1, 2 * C - 1) if d == 0 else (0, C)
    blast = jnp.where(row < C, b[first:first + 1, :], b[second:second + 1, :])

    k = k_ref[...]
    q_in = q_ref[...] * (GLA_DK ** -0.5) * jnp.exp(b)
    kT = k.T
    bT = b.T
    lane = _iota((1, BLK), 1)
    blastT = jnp.where(lane < C, bT[:, first:first + 1], bT[:, second:second + 1])
    kinT = (kT * jnp.exp(-bT)).astype(BF16)
    koutT = kT * jnp.exp(blastT - bT)
    vb = v_ref[...].astype(BF16)

    lane_k = _iota((1, GLA_HEADS * GLA_DK), 1)
    qm = [jnp.where((lane_k >= h * GLA_DK) & (lane_k < (h + 1) * GLA_DK), q_in, 0.0).astype(BF16)
          for h in range(GLA_HEADS)]
    o_intra = []
    for h in range(GLA_HEADS):
        att = jnp.dot(qm[h], kinT, preferred_element_type=F32)
        att = jnp.where(tri, att, 0.0)
        o_intra.append(_bdot(att, vb[:, h * GLA_DV:(h + 1) * GLA_DV]))

    S = s_scr[...]
    o_inter = [[None, None] for _ in range(GLA_HEADS)]
    for cc in ((0, 1) if d == 0 else (1, 0)):
        Sb = S.astype(BF16)
        for h in range(GLA_HEADS):
            o_inter[h][cc] = jnp.dot(qm[h][cc * C:(cc + 1) * C, :], Sb, preferred_element_type=F32)
        tot = first if cc == 0 else second
        dec = jnp.exp(bT[:, tot:tot + 1])
        kc = jnp.where((lane >= cc * C) & (lane < (cc + 1) * C), koutT, 0.0).astype(BF16)
        upd = [jnp.dot(kc[h * GLA_DK:(h + 1) * GLA_DK, :], vb[:, h * GLA_DV:(h + 1) * GLA_DV],
                       preferred_element_type=F32) for h in range(GLA_HEADS)]
        S = S * dec + jnp.concatenate(upd, axis=0)
    s_scr[...] = S

    o = jnp.concatenate(
        [o_intra[h] + jnp.concatenate(o_inter[h], axis=0) for h in range(GLA_HEADS)], axis=1)
    if d == 0:
        of_scr[blk] = o
    else:
        o = o + of_scr[blk]
        gate = gate_ref[...]
        nw = nw_ref[...]
        outs = []
        for h in range(GLA_HEADS):
            sl = slice(h * GLA_DV, (h + 1) * GLA_DV)
            outs.append(_rms(o[:, sl], nw) * _silu(gate[:, sl]))
        o_ref[...] = jnp.concatenate(outs, axis=1).astype(o_ref.dtype)


def _gla_kernel(*refs, nblk):
    s_scr = refs[-1]
    s = pl.program_id(1)

    @pl.when((s == 0) | (s == nblk))
    def _():
        s_scr[...] = jnp.zeros_like(s_scr)

    @pl.when(s < nblk)
    def _():
        _gla_block(*refs, d=0, blk=s)

    @pl.when(s >= nblk)
    def _():
        _gla_block(*refs, d=1, blk=2 * nblk - 1 - s)


def _scan_blk(s, nblk):
    return jnp.where(s < nblk, s, 2 * nblk - 1 - s)


def _scan_out_blk(s, nblk):
    return jnp.where(s < nblk, nblk - 1, 2 * nblk - 1 - s)


def _gla(proj, wa_p, ba, nw):
    B, Lp, _ = proj.shape
    nblk = Lp // BLK
    kd = GLA_HEADS * GLA_DK
    im = lambda col: (lambda b, s: (b, _scan_blk(s, nblk), col))
    return pl.pallas_call(
        functools.partial(_gla_kernel, nblk=nblk),
        grid=(B, 2 * nblk),
        in_specs=[
            pl.BlockSpec((None, BLK, kd), im(C_GQ // kd)),
            pl.BlockSpec((None, BLK, kd), im(C_GK // kd)),
            pl.BlockSpec((None, BLK, GLA_WIDTH), im(C_GV // GLA_WIDTH)),
            pl.BlockSpec((None, BLK, GLA_WIDTH), im(C_GG // GLA_WIDTH)),
            pl.BlockSpec((None, BLK, 128), im(C_MISC // 128)),
            pl.BlockSpec((2, 128, kd), lambda b, s: (0, 0, 0)),
            pl.BlockSpec((2, 1, kd), lambda b, s: (0, 0, 0)),
            pl.BlockSpec((1, GLA_DV), lambda b, s: (0, 0)),
        ],
        out_specs=pl.BlockSpec((None, BLK, GLA_WIDTH), lambda b, s: (b, _scan_out_blk(s, nblk), 0)),
        out_shape=jax.ShapeDtypeStruct((B, Lp, GLA_WIDTH), BF16),
        scratch_shapes=[pltpu.VMEM((nblk, BLK, GLA_WIDTH), F32), pltpu.VMEM((kd, GLA_DV), F32)],
        compiler_params=_params(("arbitrary", "arbitrary")),
        name="gla",
    )(proj, proj, proj, proj, proj, wa_p, ba, nw)


def _attn_kernel(q_ref, k_ref, v_ref, g_ref, band_ref, lam_ref, lc_ref, nw_ref, o_ref,
                 kT_scr, vb_scr, s_scr, m_scr, l_scr, acc_scr, *, T, nk):
    i = pl.program_id(2)

    @pl.when(i == 0)
    def _():
        for j in range(nk):
            kT_scr[j] = k_ref[j * T:(j + 1) * T, :].T.astype(BF16)
            vb_scr[j] = v_ref[j * T:(j + 1) * T, :].astype(BF16)

    lane = _iota((1, 2 * DIFF_DQK), 1)
    q = q_ref[...] * (DIFF_DQK ** -0.5)
    col0 = _iota((1, T), 1)
    outs = []
    for c in range(2):
        qc = jnp.where((lane >= c * DIFF_DQK) & (lane < (c + 1) * DIFF_DQK), q, 0.0).astype(BF16)

        def score_tile(j, masked):
            s = jnp.dot(qc, kT_scr[j], preferred_element_type=F32)
            s = s + band_ref[jnp.clip(j - i, -2, 2) + 2]
            if masked:
                s = jnp.where(col0 >= PAD, s, -1e30)
            s_scr[j] = s
            m = s[:, 0:128]
            for t in range(1, T // 128):
                m = jnp.maximum(m, s[:, t * 128:(t + 1) * 128])
            return m

        m_scr[...] = score_tile(0, True)

        def body_a(j, carry):
            m_scr[...] = jnp.maximum(m_scr[...], score_tile(j, False))
            return carry

        lax.fori_loop(1, nk, body_a, 0)
        m = jnp.max(m_scr[...], axis=-1, keepdims=True)

        l_scr[...] = jnp.zeros_like(l_scr)
        acc_scr[...] = jnp.zeros_like(acc_scr)

        def body_b(j, carry):
            e = jnp.exp(s_scr[j] - m)
            ls = e[:, 0:128]
            for t in range(1, T // 128):
                ls = ls + e[:, t * 128:(t + 1) * 128]
            l_scr[...] += ls
            acc_scr[...] += jnp.dot(e.astype(BF16), vb_scr[j], preferred_element_type=F32)
            return carry

        lax.fori_loop(0, nk, body_b, 0)
        l = jnp.sum(l_scr[...], axis=-1, keepdims=True)
        outs.append(acc_scr[...] / l)

    lp = lam_ref[...]
    lc = lc_ref[...]
    lam = (jnp.exp(jnp.sum(lp[0:1] * lp[1:2], axis=-1, keepdims=True))
           - jnp.exp(jnp.sum(lp[2:3] * lp[3:4], axis=-1, keepdims=True)) + lc[:, 0:1])
    o = outs[0] - lam * outs[1]
    y = _rms(o, nw_ref[...]) * lc[:, 1:2]
    o_ref[...] = (y * _silu(g_ref[...])).astype(o_ref.dtype)


def _attn(proj, band, diff_lambda, lcoef, nw, T):
    B, Lp, _ = proj.shape
    nk = Lp // T
    H = DIFF_HEADS
    return pl.pallas_call(
        functools.partial(_attn_kernel, T=T, nk=nk),
        grid=(B, H, nk),
        in_specs=[
            pl.BlockSpec((None, T, 128), lambda b, h, i: (b, i, C_DQ // 128 + h)),
            pl.BlockSpec((None, Lp, 128), lambda b, h, i: (b, 0, C_DK // 128 + h)),
            pl.BlockSpec((None, Lp, 128), lambda b, h, i: (b, 0, C_DV // 128 + h)),
            pl.BlockSpec((None, T, 128), lambda b, h, i: (b, i, C_DG // 128 + h)),
            pl.BlockSpec((None, 5, T, T), lambda b, h, i: (h, 0, 0, 0)),
            pl.BlockSpec((4, DIFF_DQK), lambda b, h, i: (0, 0)),
            pl.BlockSpec((1, 128), lambda b, h, i: (0, 0)),
            pl.BlockSpec((1, DIFF_DV), lambda b, h, i: (0, 0)),
        ],
        out_specs=pl.BlockSpec((None, T, 128), lambda b, h, i: (b, i, h)),
        out_shape=jax.ShapeDtypeStruct((B, Lp, DIFF_WIDTH), BF16),
        scratch_shapes=[
            pltpu.VMEM((nk, 128, T), BF16),
            pltpu.VMEM((nk, T, 128), BF16),
            pltpu.VMEM((nk, T, T), F32),
            pltpu.VMEM((T, 128), F32),
            pltpu.VMEM((T, 128), F32),
            pltpu.VMEM((T, 128), F32),
        ],
        compiler_params=_params(("arbitrary", "arbitrary", "arbitrary")),
        name="diff_attn",
    )(proj, proj, proj, proj, band, diff_lambda, lcoef, nw)


def _bucket_table(T):
    nb = N_BUCKETS // 2
    max_exact = nb // 2
    rel = (np.arange(5 * T)[None, :] - 2 * T) - np.arange(T)[:, None]
    ret = np.where(rel > 0, nb, 0)
    n = np.abs(rel)
    nf = np.maximum(n, 1).astype(np.float64)
    large = max_exact + (np.log(nf / max_exact) / math.log(MAX_DISTANCE / max_exact)
                         * (nb - max_exact)).astype(np.int32)
    large = np.minimum(large, nb - 1)
    return (ret + np.where(n < max_exact, n, large)).astype(np.int32)


def _bias_band(rel_bias, T):
    band = rel_bias.astype(F32)[_bucket_table(T)]
    band = jnp.transpose(band, (2, 0, 1)).reshape(DIFF_HEADS, T, 5, T)
    return jnp.transpose(band, (0, 2, 1, 3))


def _conv_kernel(x_ref, w_ref, b_ref, o_ref):
    x = x_ref[...]
    w = w_ref[...]
    n = x.shape[0]
    half = (SSD_CONV - 1) // 2
    acc = x * w[half:half + 1, :] + b_ref[...]
    for kk in range(SSD_CONV):
        if kk != half:
            acc = acc + pltpu.roll(x, (half - kk) % n, 0) * w[kk:kk + 1, :]
    o_ref[...] = _silu(acc)


def _conv(proj, conv_w, conv_b):
    B, Lp, _ = proj.shape
    tn = 256
    return pl.pallas_call(
        _conv_kernel,
        grid=(B, SSD_CONV_DIM // tn),
        in_specs=[
            pl.BlockSpec((None, Lp, tn), lambda b, j: (b, 0, C_XBC // tn + j)),
            pl.BlockSpec((SSD_CONV, tn), lambda b, j: (0, j)),
            pl.BlockSpec((1, tn), lambda b, j: (0, j)),
        ],
        out_specs=pl.BlockSpec((None, Lp, tn), lambda b, j: (b, 0, j)),
        out_shape=jax.ShapeDtypeStruct((B, Lp, SSD_CONV_DIM), F32),
        compiler_params=_params(("arbitrary", "arbitrary")),
        name="ssd_conv",
    )(proj, conv_w, conv_b.reshape(1, SSD_CONV_DIM))


def _ssd_block(xc_ref, z_ref, misc_ref, dtb_r_ref, dtb_c_ref, al_r_ref, al_c_ref, e_ref, dskip_ref,
               nw_ref, o_ref, yf_scr, s_scr, *, d, blk):
    G, N, P = SSD_GROUPS, SSD_STATE, SSD_HEADDIM
    R = SSD_HEADS // G
    xs = xc_ref[:, 0:SSD_WIDTH]
    Bm = xc_ref[:, SSD_WIDTH:SSD_WIDTH + G * N]
    Cb = xc_ref[:, SSD_WIDTH + G * N:].astype(BF16)
    misc = misc_ref[...]
    miscT = misc.T
    row = _iota((BLK, 1), 0)
    lane = _iota((1, BLK), 1)
    dt = jnp.where((blk * BLK + row) >= PAD, _softplus(misc + dtb_r_ref[...]), 0.0)
    a = dt * (-jnp.exp(al_r_ref[...]))
    dtT = jnp.where((blk * BLK + lane) >= PAD, _softplus(miscT + dtb_c_ref[...]), 0.0)
    aT = dtT * (-jnp.exp(al_c_ref[...]))

    r = _iota((BLK, BLK), 0)
    c = _iota((BLK, BLK), 1)
    tri = (c <= r) if d == 0 else (c >= r)
    triT = (r <= c) if d == 0 else (r >= c)
    cum = _dot_r3(jnp.where(tri, 1.0, 0.0).astype(BF16), a)
    cumT = _dot_l3(aT, jnp.where(triT, 1.0, 0.0).astype(BF16))

    lo = DT_LANE0 + SSD_HEADS * d
    hm = (lane >= lo) & (lane < lo + SSD_HEADS)
    last = BLK - 1 if d == 0 else 0
    tot = cum[last:last + 1, :]
    E = e_ref[d]
    dt_x = _dot_l3(jnp.where(hm, dt, 0.0), E)
    ecum_x = _dot_l3(jnp.where(hm, jnp.exp(cum), 0.0), E)
    toend_x = _dot_l3(jnp.where(hm, jnp.exp(tot - cum), 0.0), E)
    etot_x = ecum_x[last:last + 1, :]
    xd = xs * dt_x
    xdb = xd.astype(BF16)
    xdw = (xd * toend_x).astype(BF16)

    y_parts = []
    for g in range(G):
        BgT = Bm[:, g * N:(g + 1) * N].T.astype(BF16)
        Cg = Cb[:, g * N:(g + 1) * N]
        cols = slice(g * R * P, (g + 1) * R * P)
        CB = jnp.dot(Cg, BgT, preferred_element_type=F32)
        S = s_scr[g]
        y_off = jnp.dot(Cg, S.astype(BF16), preferred_element_type=F32) * ecum_x[:, cols]
        s_scr[g] = S * etot_x[:, cols] + jnp.dot(BgT, xdw[:, cols], preferred_element_type=F32)
        for pr in range(R // 2):
            h0 = g * R + 2 * pr
            xpair = xdb[:, h0 * P:(h0 + 2) * P]
            acc = y_off[:, 2 * pr * P:(2 * pr + 2) * P]
            for hh in range(2):
                li = lo + h0 + hh
                seg = cum[:, li:li + 1] - cumT[li:li + 1, :]
                dec = jnp.exp(jnp.where(tri, seg, -jnp.inf))
                sc = (CB * dec).astype(BF16)
                xm = jnp.where((lane >= hh * P) & (lane < (hh + 1) * P), xpair, 0.0)
                acc = acc + jnp.dot(sc, xm, preferred_element_type=F32)
            y_parts.append(acc)
    y = jnp.concatenate(y_parts, axis=1)

    if d == 0:
        yf_scr[blk] = y
    else:
        y = (y + yf_scr[blk] + xs * dskip_ref[...]) * _silu(z_ref[...])
        nw = nw_ref[...]
        W = SSD_WIDTH // G
        outs = [_rms(y[:, g * W:(g + 1) * W], nw[:, g * W:(g + 1) * W]) for g in range(G)]
        o_ref[...] = jnp.concatenate(outs, axis=1).astype(o_ref.dtype)


def _ssd_kernel(*refs, nblk):
    s_scr = refs[-1]
    s = pl.program_id(1)

    @pl.when((s == 0) | (s == nblk))
    def _():
        s_scr[...] = jnp.zeros_like(s_scr)

    @pl.when(s < nblk)
    def _():
        _ssd_block(*refs, d=0, blk=s)

    @pl.when(s >= nblk)
    def _():
        _ssd_block(*refs, d=1, blk=2 * nblk - 1 - s)


def _ssd(proj, xc, dtb_r, dtb_c, al_r, al_c, E, dskip, nw):
    B, Lp, _ = proj.shape
    nblk = Lp // BLK
    const2 = lambda b, s: (0, 0)
    return pl.pallas_call(
        functools.partial(_ssd_kernel, nblk=nblk),
        grid=(B, 2 * nblk),
        in_specs=[
            pl.BlockSpec((None, BLK, SSD_CONV_DIM), lambda b, s: (b, _scan_blk(s, nblk), 0)),
            pl.BlockSpec((None, BLK, SSD_WIDTH), lambda b, s: (b, _scan_blk(s, nblk), C_Z // SSD_WIDTH)),
            pl.BlockSpec((None, BLK, 128), lambda b, s: (b, _scan_blk(s, nblk), C_MISC // 128)),
            pl.BlockSpec((1, 128), const2),
            pl.BlockSpec((128, 128), const2),
            pl.BlockSpec((1, 128), const2),
            pl.BlockSpec((128, 128), const2),
            pl.BlockSpec((2, 128, SSD_WIDTH), lambda b, s: (0, 0, 0)),
            pl.BlockSpec((1, SSD_WIDTH), const2),
            pl.BlockSpec((1, SSD_WIDTH), const2),
        ],
        out_specs=pl.BlockSpec((None, BLK, SSD_WIDTH), lambda b, s: (b, _scan_out_blk(s, nblk), 0)),
        out_shape=jax.ShapeDtypeStruct((B, Lp, SSD_WIDTH), BF16),
        scratch_shapes=[
            pltpu.VMEM((nblk, BLK, SSD_WIDTH), F32),
            pltpu.VMEM((SSD_GROUPS, SSD_STATE, SSD_WIDTH // SSD_GROUPS), F32),
        ],
        compiler_params=_params(("arbitrary", "arbitrary")),
        name="ssd_scan",
    )(xc, proj, proj, dtb_r, dtb_c, al_r, al_c, E, dskip, nw)


def _permute_w_in(w):
    o = np.cumsum([0, 256, 256, 512, 512, 32, 512, 512, 512, 512, 1024, 1536, 32])
    gq, gk, gv, gg, gcode, dq, dk, dv, dg, z, xbc, dt, end = [int(v) for v in o]
    pieces = [w[..., gq:gcode], w[..., dq:dg], w[..., z:xbc], w[..., dg:z], w[..., xbc:dt],
              w[..., gcode:dq], w[..., dt:end],
              jnp.zeros(w.shape[:-1] + (N_PROJ - C_MISC - 64,), w.dtype)]
    return jnp.concatenate(pieces, axis=-1)


def _expansion():
    E = np.zeros((2, 128, SSD_WIDTH), np.float32)
    for d in range(2):
        for h in range(SSD_HEADS):
            E[d, DT_LANE0 + SSD_HEADS * d + h, h * SSD_HEADDIM:(h + 1) * SSD_HEADDIM] = 1.0
    return jnp.asarray(E, BF16)


def _misc_row(p):
    flat = p.reshape(p.shape[0], 1, 2 * SSD_HEADS).astype(F32)
    return jnp.pad(flat, ((0, 0), (0, 0), (DT_LANE0, 128 - DT_LANE0 - 2 * SSD_HEADS)))


def kernel(x, meta_tokens, rel_bias, final_norm_w, norm_w, w_in, w_out, gla_wa2, gla_ba, gla_norm_w,
           diff_lambda, diff_norm_w, conv_w, conv_b, ssd_A_log, ssd_dt_bias, ssd_D, ssd_norm_w):
    B, S, D = x.shape
    assert D == D_MODEL and S % BLK == 0
    Lp = TOK0 + S
    T = 384 if Lp % 384 == 0 else 128

    h = jnp.concatenate([
        jnp.zeros((B, PAD, D), x.dtype),
        jnp.broadcast_to(meta_tokens[None].astype(x.dtype), (B, N_META, D)),
        x], axis=1)

    band = _bias_band(rel_bias, T)
    E = _expansion()
    lam_init = np.array([0.8 - 0.6 * math.exp(-0.3 * l) for l in range(DEPTH)], np.float32)
    lcoef = np.zeros((DEPTH, 1, 128), np.float32)
    lcoef[:, 0, 0] = lam_init
    lcoef[:, 0, 1] = 1.0 - lam_init

    kd = GLA_HEADS * GLA_DK
    wa_p = jnp.zeros((DEPTH, 2, 128, kd), F32)
    wa_p = wa_p.at[:, 0, 0:GLA_RANK].set(gla_wa2[:, 0].astype(F32))
    wa_p = wa_p.at[:, 1, GLA_RANK:2 * GLA_RANK].set(gla_wa2[:, 1].astype(F32))
    dtb_r = _misc_row(ssd_dt_bias)
    al_r = _misc_row(ssd_A_log)
    layers = dict(
        norm_w=norm_w,
        w_in=_permute_w_in(w_in).astype(BF16),
        w_out=w_out.astype(BF16),
        wa_p=wa_p,
        ba=gla_ba.reshape(DEPTH, 2, 1, kd).astype(F32),
        gla_nw=gla_norm_w.reshape(DEPTH, 1, GLA_DV).astype(F32),
        lam=diff_lambda.astype(F32),
        lcoef=jnp.asarray(lcoef),
        diff_nw=diff_norm_w.reshape(DEPTH, 1, DIFF_DV).astype(F32),
        conv_w=conv_w.astype(F32),
        conv_b=conv_b.astype(F32),
        dtb_r=dtb_r,
        dtb_c=jnp.broadcast_to(jnp.swapaxes(dtb_r, 1, 2), (DEPTH, 128, 128)),
        al_r=al_r,
        al_c=jnp.broadcast_to(jnp.swapaxes(al_r, 1, 2), (DEPTH, 128, 128)),
        dskip=jnp.repeat(ssd_D.astype(F32), SSD_HEADDIM, axis=-1).reshape(DEPTH, 1, SSD_WIDTH),
        ssd_nw=ssd_norm_w.reshape(DEPTH, 1, SSD_WIDTH).astype(F32),
    )

    def layer(h, p):
        proj = _inproj(h, p["norm_w"], p["w_in"])
        o_gla = _gla(proj, p["wa_p"], p["ba"], p["gla_nw"])
        o_diff = _attn(proj, band, p["lam"], p["lcoef"], p["diff_nw"], T)
        xc = _conv(proj, p["conv_w"], p["conv_b"])
        o_ssd = _ssd(proj, xc, p["dtb_r"], p["dtb_c"], p["al_r"], p["al_c"], E, p["dskip"], p["ssd_nw"])
        return _outproj(h, o_gla, o_diff, o_ssd, p["w_out"]), None

    h, _ = lax.scan(layer, h, layers)
    return _final_norm(h, final_norm_w)
```

```python
import functools
import math

import numpy as np
import jax
import jax.numpy as jnp
from jax import lax
from jax.experimental import pallas as pl
from jax.experimental.pallas import tpu as pltpu

F32 = jnp.float32
BF16 = jnp.bfloat16

D_MODEL = 2048
DEPTH = 4
N_META = 16
EPS = 1e-6
GLA_HEADS = 4
GLA_DK = 64
GLA_DV = 128
GLA_WIDTH = 512
GLA_RANK = 16
GLA_TAU = 16.0
GLA_CHUNK = 64
DIFF_HEADS = 4
DIFF_DQK = 64
DIFF_DV = 128
DIFF_WIDTH = 512
N_BUCKETS = 32
MAX_DISTANCE = 128
SSD_WIDTH = 1024
SSD_HEADDIM = 64
SSD_HEADS = 16
SSD_GROUPS = 2
SSD_STATE = 128
SSD_CONV = 5
SSD_CHUNK = 128
SSD_CONV_DIM = SSD_WIDTH + 2 * SSD_GROUPS * SSD_STATE

BLK = 128
TOK0 = BLK
PAD = TOK0 - N_META

C_GQ, C_GK, C_GV, C_GG = 0, 256, 512, 1024
C_DQ, C_DK, C_DV = 1536, 2048, 2560
C_Z, C_DG, C_XBC, C_MISC = 3072, 4096, 4608, 6144
N_PROJ = 6272
DT_LANE0 = 2 * GLA_RANK

VMEM_LIMIT = 56 * 1024 * 1024
LOG2E = 1.4426950408889634


def _params(sem, limit=VMEM_LIMIT):
    return pltpu.CompilerParams(dimension_semantics=sem, vmem_limit_bytes=limit)


def _silu(x):
    return x / (1.0 + jnp.exp(-x))


def _softplus(x):
    return jnp.maximum(x, 0.0) + jnp.log1p(jnp.exp(-jnp.abs(x)))


def _log_sigmoid(x):
    return jnp.minimum(x, 0.0) - jnp.log1p(jnp.exp(-jnp.abs(x)))


def _bdot(a, b):
    return jnp.dot(a.astype(BF16), b.astype(BF16), preferred_element_type=F32)


def _split3(x):
    hi = x.astype(BF16)
    r = x - hi.astype(F32)
    mid = r.astype(BF16)
    lo = (r - mid.astype(F32)).astype(BF16)
    return hi, mid, lo


def _dot_l3(a, b_exact):
    hi, mid, lo = _split3(a)
    d = functools.partial(jnp.dot, preferred_element_type=F32)
    return d(hi, b_exact) + d(mid, b_exact) + d(lo, b_exact)


def _dot_r3(a_exact, b):
    hi, mid, lo = _split3(b)
    d = functools.partial(jnp.dot, preferred_element_type=F32)
    return d(a_exact, hi) + d(a_exact, mid) + d(a_exact, lo)


def _dot_22(a, b):
    ah = a.astype(BF16)
    al = (a - ah.astype(F32)).astype(BF16)
    bh = b.astype(BF16)
    bl = (b - bh.astype(F32)).astype(BF16)
    d = functools.partial(jnp.dot, preferred_element_type=F32)
    return d(ah, bh) + d(ah, bl) + d(al, bh)


def _iota(shape, dim):
    return lax.broadcasted_iota(jnp.int32, shape, dim)


def _rms(x, w):
    return x * lax.rsqrt(jnp.mean(x * x, axis=-1, keepdims=True) + EPS) * w


def _inproj_kernel(h_ref, nw_ref, w_ref, o_ref, u_scr, *, tm):
    @pl.when(pl.program_id(2) == 0)
    def _():
        x = h_ref[...]
        y = _rms(x, nw_ref[...])
        row = pl.program_id(1) * tm + _iota((tm, 1), 0)
        u_scr[...] = jnp.where(row >= PAD, y, 0.0).astype(BF16)

    o_ref[...] = jnp.dot(u_scr[...], w_ref[...], preferred_element_type=F32)


def _inproj(h, norm_w, w_p):
    B, Lp, D = h.shape
    tm = Lp // 4
    tn = N_PROJ // 7
    return pl.pallas_call(
        functools.partial(_inproj_kernel, tm=tm),
        grid=(B, Lp // tm, N_PROJ // tn),
        in_specs=[
            pl.BlockSpec((None, tm, D), lambda b, i, j: (b, i, 0)),
            pl.BlockSpec((1, D), lambda b, i, j: (0, 0)),
            pl.BlockSpec((D, tn), lambda b, i, j: (0, j)),
        ],
        out_specs=pl.BlockSpec((None, tm, tn), lambda b, i, j: (b, i, j)),
        out_shape=jax.ShapeDtypeStruct((B, Lp, N_PROJ), F32),
        scratch_shapes=[pltpu.VMEM((tm, D), BF16)],
        compiler_params=_params(("arbitrary", "arbitrary", "arbitrary")),
        name="inproj",
    )(h, norm_w.reshape(1, D), w_p)


def _outproj_kernel(h_ref, a_ref, b_ref, c_ref, w_ref, o_ref):
    d = functools.partial(jnp.dot, preferred_element_type=F32)
    acc = d(a_ref[...], w_ref[0:GLA_WIDTH, :])
    acc += d(b_ref[...], w_ref[GLA_WIDTH:GLA_WIDTH + DIFF_WIDTH, :])
    acc += d(c_ref[...], w_ref[GLA_WIDTH + DIFF_WIDTH:, :])
    o_ref[...] = h_ref[...] + acc


def _outproj(h, o_gla, o_diff, o_ssd, w_out):
    B, Lp, D = h.shape
    tm = Lp // 4
    tn = D // 2
    return pl.pallas_call(
        _outproj_kernel,
        grid=(B, Lp // tm, D // tn),
        in_specs=[
            pl.BlockSpec((None, tm, tn), lambda b, i, j: (b, i, j)),
            pl.BlockSpec((None, tm, GLA_WIDTH), lambda b, i, j: (b, i, 0)),
            pl.BlockSpec((None, tm, DIFF_WIDTH), lambda b, i, j: (b, i, 0)),
            pl.BlockSpec((None, tm, SSD_WIDTH), lambda b, i, j: (b, i, 0)),
            pl.BlockSpec((D, tn), lambda b, i, j: (0, j)),
        ],
        out_specs=pl.BlockSpec((None, tm, tn), lambda b, i, j: (b, i, j)),
        out_shape=jax.ShapeDtypeStruct((B, Lp, D), F32),
        compiler_params=_params(("arbitrary", "arbitrary", "arbitrary")),
        name="outproj",
    )(h, o_gla, o_diff, o_ssd, w_out)


def _final_kernel(h_ref, w_ref, o_ref):
    o_ref[...] = _rms(h_ref[...], w_ref[...])


def _final_norm(h, w):
    B, Lp, D = h.shape
    S = Lp - TOK0
    return pl.pallas_call(
        _final_kernel,
        grid=(B, S // BLK),
        in_specs=[
            pl.BlockSpec((None, BLK, D), lambda b, i: (b, i + 1, 0)),
            pl.BlockSpec((1, D), lambda b, i: (0, 0)),
        ],
        out_specs=pl.BlockSpec((None, BLK, D), lambda b, i: (b, i, 0)),
        out_shape=jax.ShapeDtypeStruct((B, S, D), F32),
        compiler_params=_params(("arbitrary", "arbitrary")),
        name="final_norm",
    )(h, w.reshape(1, D))


def _gla_block(q_ref, k_ref, v_ref, gate_ref, misc_ref, wa_ref, ba_ref, nw_ref, o_ref,
               of_scr, s_scr, *, d, blk):
    C = GLA_CHUNK
    row = _iota((BLK, 1), 0)
    valid = (blk * BLK + row) >= PAD
    x = _dot_22(misc_ref[...], wa_ref[d]) + ba_ref[d]
    g = jnp.where(valid, _log_sigmoid(x) * (1.0 / GLA_TAU), 0.0)

    r = _iota((BLK, BLK), 0)
    c = _iota((BLK, BLK), 1)
    same = (r >= C) == (c >= C)
    tri = (same & (c <= r)) if d == 0 else (same & (c >= r))
    b = _dot_r3(jnp.where(tri, 1.0, 0.0).astype(BF16), g)
    first, second = (C - 1, 2 * C - 1) if d == 0 else (0, C)
    blast = jnp.where(row < C, b[first:first + 1, :], b[second:second + 1, :])

    k = k_ref[...]
    q_in = q_ref[...] * (GLA_DK ** -0.5) * jnp.exp(b)
    kT = k.T
    bT = b.T
    lane = _iota((1, BLK), 1)
    blastT = jnp.where(lane < C, bT[:, first:first + 1], bT[:, second:second + 1])
    kinT = (kT * jnp.exp(-bT)).astype(BF16)
    koutT = kT * jnp.exp(blastT - bT)
    vb = v_ref[...].astype(BF16)

    lane_k = _iota((1, GLA_HEADS * GLA_DK), 1)
    qm = [jnp.where((lane_k >= h * GLA_DK) & (lane_k < (h + 1) * GLA_DK), q_in, 0.0).astype(BF16)
          for h in range(GLA_HEADS)]
    o_intra = []
    for h in range(GLA_HEADS):
        att = jnp.dot(qm[h], kinT, preferred_element_type=F32)
        att = jnp.where(tri, att, 0.0)
        o_intra.append(_bdot(att, vb[:, h * GLA_DV:(h + 1) * GLA_DV]))

    S = s_scr[...]
    o_inter = [[None, None] for _ in range(GLA_HEADS)]
    for cc in ((0, 1) if d == 0 else (1, 0)):
        Sb = S.astype(BF16)
        for h in range(GLA_HEADS):
            o_inter[h][cc] = jnp.dot(qm[h][cc * C:(cc + 1) * C, :], Sb, preferred_element_type=F32)
        tot = first if cc == 0 else second
        dec = jnp.exp(bT[:, tot:tot + 1])
        kc = jnp.where((lane >= cc * C) & (lane < (cc + 1) * C), koutT, 0.0).astype(BF16)
        upd = [jnp.dot(kc[h * GLA_DK:(h + 1) * GLA_DK, :], vb[:, h * GLA_DV:(h + 1) * GLA_DV],
                       preferred_element_type=F32) for h in range(GLA_HEADS)]
        S = S * dec + jnp.concatenate(upd, axis=0)
    s_scr[...] = S

    o = jnp.concatenate(
        [o_intra[h] + jnp.concatenate(o_inter[h], axis=0) for h in range(GLA_HEADS)], axis=1)
    if d == 0:
        of_scr[blk] = o
    else:
        o = o + of_scr[blk]
        gate = gate_ref[...]
        nw = nw_ref[...]
        outs = []
        for h in range(GLA_HEADS):
            sl = slice(h * GLA_DV, (h + 1) * GLA_DV)
            outs.append(_rms(o[:, sl], nw) * _silu(gate[:, sl]))
        o_ref[...] = jnp.concatenate(outs, axis=1).astype(o_ref.dtype)


def _gla_kernel(*refs, nblk):
    s_scr = refs[-1]
    s = pl.program_id(1)

    @pl.when((s == 0) | (s == nblk))
    def _():
        s_scr[...] = jnp.zeros_like(s_scr)

    @pl.when(s < nblk)
    def _():
        _gla_block(*refs, d=0, blk=s)

    @pl.when(s >= nblk)
    def _():
        _gla_block(*refs, d=1, blk=2 * nblk - 1 - s)


def _scan_blk(s, nblk):
    return jnp.where(s < nblk, s, 2 * nblk - 1 - s)


def _scan_out_blk(s, nblk):
    return jnp.where(s < nblk, nblk - 1, 2 * nblk - 1 - s)


def _gla(proj, wa_p, ba, nw):
    B, Lp, _ = proj.shape
    nblk = Lp // BLK
    kd = GLA_HEADS * GLA_DK
    im = lambda col: (lambda b, s: (b, _scan_blk(s, nblk), col))
    return pl.pallas_call(
        functools.partial(_gla_kernel, nblk=nblk),
        grid=(B, 2 * nblk),
        in_specs=[
            pl.BlockSpec((None, BLK, kd), im(C_GQ // kd)),
            pl.BlockSpec((None, BLK, kd), im(C_GK // kd)),
            pl.BlockSpec((None, BLK, GLA_WIDTH), im(C_GV // GLA_WIDTH)),
            pl.BlockSpec((None, BLK, GLA_WIDTH), im(C_GG // GLA_WIDTH)),
            pl.BlockSpec((None, BLK, 128), im(C_MISC // 128)),
            pl.BlockSpec((2, 128, kd), lambda b, s: (0, 0, 0)),
            pl.BlockSpec((2, 1, kd), lambda b, s: (0, 0, 0)),
            pl.BlockSpec((1, GLA_DV), lambda b, s: (0, 0)),
        ],
        out_specs=pl.BlockSpec((None, BLK, GLA_WIDTH), lambda b, s: (b, _scan_out_blk(s, nblk), 0)),
        out_shape=jax.ShapeDtypeStruct((B, Lp, GLA_WIDTH), BF16),
        scratch_shapes=[pltpu.VMEM((nblk, BLK, GLA_WIDTH), F32), pltpu.VMEM((kd, GLA_DV), F32)],
        compiler_params=_params(("arbitrary", "arbitrary")),
        name="gla",
    )(proj, proj, proj, proj, proj, wa_p, ba, nw)


def _attn_kernel(q_ref, k_ref, v_ref, g_ref, band_ref, lam_ref, lc_ref, nw_ref, o_ref,
                 kT_scr, vb_scr, s_scr, e_scr, *, T, nk, RC, KC):
    i = pl.program_id(2)

    @pl.when(i == 0)
    def _():
        for j in range(nk):
            kT_scr[:, j * T:(j + 1) * T] = k_ref[j * T:(j + 1) * T, :].T.astype(BF16)
        vb_scr[...] = v_ref[...].astype(BF16)

    def fold(x, op):
        r = x[:, 0:128]
        for t in range(1, KC // 128):
            r = op(r, x[:, t * 128:(t + 1) * 128])
        return r

    lane = _iota((1, 2 * DIFF_DQK), 1)
    q = q_ref[...] * (DIFF_DQK ** -0.5 * LOG2E)
    col0 = _iota((1, KC), 1)
    qcs = [jnp.where((lane >= c * DIFF_DQK) & (lane < (c + 1) * DIFF_DQK), q, 0.0).astype(BF16)
           for c in range(2)]
    groups = [(c, rc) for c in range(2) for rc in range(T // RC)]
    n_sub = nk * T // KC
    bidx = [jnp.clip(j - i, -2, 2) + 2 for j in range(nk)]

    def score_step(g, u, m_run):
        c, rc = g
        rows = slice(rc * RC, (rc + 1) * RC)
        cols = slice(u * KC, (u + 1) * KC)
        j, t = divmod(u, T // KC)
        s = jnp.dot(qcs[c][rows], kT_scr[:, cols], preferred_element_type=F32)
        s = s + band_ref[bidx[j], rows, t * KC:(t + 1) * KC]
        if u == 0:
            s = jnp.where(col0 >= PAD, s, -1e30)
        s_scr[c, rows, cols] = s
        mt = fold(s, jnp.maximum)
        return mt if m_run is None else jnp.maximum(m_run, mt)

    def prob_step(g, u, m, l_run):
        c, rc = g
        rows = slice(rc * RC, (rc + 1) * RC)
        cols = slice(u * KC, (u + 1) * KC)
        e = jnp.exp2(s_scr[c, rows, cols] - m)
        e_scr[c, rows, cols] = e.astype(BF16)
        lt = fold(e, jnp.add)
        return lt if l_run is None else l_run + lt

    def pv_step(g, j, pv):
        c, rc = g
        rows = slice(rc * RC, (rc + 1) * RC)
        cols = slice(j * T, (j + 1) * T)
        d = jnp.dot(e_scr[c, rows, cols], vb_scr[cols, :], preferred_element_type=F32)
        return d if pv is None else pv + d

    per_tile = T // KC
    m_run = None
    for u in range(n_sub):
        m_run = score_step(groups[0], u, m_run)
    parts = []
    for gi, g in enumerate(groups):
        m = jnp.max(m_run, axis=-1, keepdims=True)
        m_run = l_run = pv = None
        for u in range(n_sub):
            l_run = prob_step(g, u, m, l_run)
            if u % per_tile == per_tile - 1 and u >= 2 * per_tile - 1:
                pv = pv_step(g, u // per_tile - 1, pv)
            if gi + 1 < len(groups):
                m_run = score_step(groups[gi + 1], u, m_run)
        pv = pv_step(g, nk - 1, pv)
        parts.append(pv / jnp.sum(l_run, axis=-1, keepdims=True))
    per_half = len(groups) // 2
    outs = [jnp.concatenate(parts[:per_half], axis=0), jnp.concatenate(parts[per_half:], axis=0)]

    lp = lam_ref[...]
    lc = lc_ref[...]
    lam = (jnp.exp(jnp.sum(lp[0:1] * lp[1:2], axis=-1, keepdims=True))
           - jnp.exp(jnp.sum(lp[2:3] * lp[3:4], axis=-1, keepdims=True)) + lc[:, 0:1])
    o = outs[0] - lam * outs[1]
    y = _rms(o, nw_ref[...]) * lc[:, 1:2]
    o_ref[...] = (y * _silu(g_ref[...])).astype(o_ref.dtype)


def _attn(proj, band, diff_lambda, lcoef, nw, T):
    B, Lp, _ = proj.shape
    nk = Lp // T
    H = DIFF_HEADS
    RC = T // 2 if T > 128 else T
    KC = 128
    return pl.pallas_call(
        functools.partial(_attn_kernel, T=T, nk=nk, RC=RC, KC=KC),
        grid=(B, H, nk),
        in_specs=[
            pl.BlockSpec((None, T, 128), lambda b, h, i: (b, i, C_DQ // 128 + h)),
            pl.BlockSpec((None, Lp, 128), lambda b, h, i: (b, 0, C_DK // 128 + h)),
            pl.BlockSpec((None, Lp, 128), lambda b, h, i: (b, 0, C_DV // 128 + h)),
            pl.BlockSpec((None, T, 128), lambda b, h, i: (b, i, C_DG // 128 + h)),
            pl.BlockSpec((None, 5, T, T), lambda b, h, i: (h, 0, 0, 0)),
            pl.BlockSpec((4, DIFF_DQK), lambda b, h, i: (0, 0)),
            pl.BlockSpec((1, 128), lambda b, h, i: (0, 0)),
            pl.BlockSpec((1, DIFF_DV), lambda b, h, i: (0, 0)),
        ],
        out_specs=pl.BlockSpec((None, T, 128), lambda b, h, i: (b, i, h)),
        out_shape=jax.ShapeDtypeStruct((B, Lp, DIFF_WIDTH), BF16),
        scratch_shapes=[
            pltpu.VMEM((128, Lp), BF16),
            pltpu.VMEM((Lp, 128), BF16),
            pltpu.VMEM((2, T, Lp), F32),
            pltpu.VMEM((2, T, Lp), BF16),
        ],
        compiler_params=_params(("arbitrary", "arbitrary", "arbitrary")),
        name="diff_attn",
    )(proj, proj, proj, proj, band, diff_lambda, lcoef, nw)


def _bucket_table(T):
    nb = N_BUCKETS // 2
    max_exact = nb // 2
    rel = np.arange(6 * T) - (3 * T - 1)
    ret = np.where(rel > 0, nb, 0)
    n = np.abs(rel)
    nf = np.maximum(n, 1).astype(np.float64)
    large = max_exact + (np.log(nf / max_exact) / math.log(MAX_DISTANCE / max_exact)
                         * (nb - max_exact)).astype(np.int32)
    large = np.minimum(large, nb - 1)
    return (ret + np.where(n < max_exact, n, large)).astype(np.int32)


def _bias_band(rel_bias, T):
    n = 6 * T
    v = (rel_bias.astype(F32)[_bucket_table(T)] * LOG2E).T
    rows = jnp.tile(v, (1, T))[:, :T * (n - 1)].reshape(DIFF_HEADS, T, n - 1)
    band = rows[:, :, T - 1:n - 1].reshape(DIFF_HEADS, T, 5, T)
    return jnp.transpose(band, (0, 2, 1, 3))


def _conv_kernel(x_ref, w_ref, b_ref, o_ref):
    x = x_ref[...]
    w = w_ref[...]
    n = x.shape[0]
    half = (SSD_CONV - 1) // 2
    acc = x * w[half:half + 1, :] + b_ref[...]
    for kk in range(SSD_CONV):
        if kk != half:
            acc = acc + pltpu.roll(x, (half - kk) % n, 0) * w[kk:kk + 1, :]
    o_ref[...] = _silu(acc)


def _conv(proj, conv_w, conv_b):
    B, Lp, _ = proj.shape
    tn = 256
    return pl.pallas_call(
        _conv_kernel,
        grid=(B, SSD_CONV_DIM // tn),
        in_specs=[
            pl.BlockSpec((None, Lp, tn), lambda b, j: (b, 0, C_XBC // tn + j)),
            pl.BlockSpec((SSD_CONV, tn), lambda b, j: (0, j)),
            pl.BlockSpec((1, tn), lambda b, j: (0, j)),
        ],
        out_specs=pl.BlockSpec((None, Lp, tn), lambda b, j: (b, 0, j)),
        out_shape=jax.ShapeDtypeStruct((B, Lp, SSD_CONV_DIM), F32),
        compiler_params=_params(("arbitrary", "arbitrary")),
        name="ssd_conv",
    )(proj, conv_w, conv_b.reshape(1, SSD_CONV_DIM))


def _ssd_block(xc_ref, z_ref, misc_ref, dtb_r_ref, dtb_c_ref, al_r_ref, al_c_ref, e_ref, dskip_ref,
               nw_ref, o_ref, yf_scr, s_scr, *, d, blk):
    G, N, P = SSD_GROUPS, SSD_STATE, SSD_HEADDIM
    R = SSD_HEADS // G
    xs = xc_ref[:, 0:SSD_WIDTH]
    Bm = xc_ref[:, SSD_WIDTH:SSD_WIDTH + G * N]
    Cb = xc_ref[:, SSD_WIDTH + G * N:].astype(BF16)
    misc = misc_ref[...]
    miscT = misc.T
    row = _iota((BLK, 1), 0)
    lane = _iota((1, BLK), 1)
    dt = jnp.where((blk * BLK + row) >= PAD, _softplus(misc + dtb_r_ref[...]), 0.0)
    a = dt * (-jnp.exp(al_r_ref[...]))
    dtT = jnp.where((blk * BLK + lane) >= PAD, _softplus(miscT + dtb_c_ref[...]), 0.0)
    aT = dtT * (-jnp.exp(al_c_ref[...]))

    r = _iota((BLK, BLK), 0)
    c = _iota((BLK, BLK), 1)
    tri = (c <= r) if d == 0 else (c >= r)
    triT = (r <= c) if d == 0 else (r >= c)
    cum = _dot_r3(jnp.where(tri, 1.0, 0.0).astype(BF16), a)
    cumT = _dot_l3(aT, jnp.where(triT, 1.0, 0.0).astype(BF16))

    lo = DT_LANE0 + SSD_HEADS * d
    hm = (lane >= lo) & (lane < lo + SSD_HEADS)
    last = BLK - 1 if d == 0 else 0
    tot = cum[last:last + 1, :]
    E = e_ref[d]
    dt_x = _dot_l3(jnp.where(hm, dt, 0.0), E)
    ecum_x = _dot_l3(jnp.where(hm, jnp.exp(cum), 0.0), E)
    toend_x = _dot_l3(jnp.where(hm, jnp.exp(tot - cum), 0.0), E)
    etot_x = ecum_x[last:last + 1, :]
    xd = xs * dt_x
    xdb = xd.astype(BF16)
    xdw = (xd * toend_x).astype(BF16)

    y_parts = []
    for g in range(G):
        BgT = Bm[:, g * N:(g + 1) * N].T.astype(BF16)
        Cg = Cb[:, g * N:(g + 1) * N]
        cols = slice(g * R * P, (g + 1) * R * P)
        CB = jnp.dot(Cg, BgT, preferred_element_type=F32)
        S = s_scr[g]
        y_off = jnp.dot(Cg, S.astype(BF16), preferred_element_type=F32) * ecum_x[:, cols]
        s_scr[g] = S * etot_x[:, cols] + jnp.dot(BgT, xdw[:, cols], preferred_element_type=F32)
        for pr in range(R // 2):
            h0 = g * R + 2 * pr
            xpair = xdb[:, h0 * P:(h0 + 2) * P]
            acc = y_off[:, 2 * pr * P:(2 * pr + 2) * P]
            for hh in range(2):
                li = lo + h0 + hh
                seg = cum[:, li:li + 1] - cumT[li:li + 1, :]
                dec = jnp.exp(jnp.where(tri, seg, -jnp.inf))
                sc = (CB * dec).astype(BF16)
                xm = jnp.where((lane >= hh * P) & (lane < (hh + 1) * P), xpair, 0.0)
                acc = acc + jnp.dot(sc, xm, preferred_element_type=F32)
            y_parts.append(acc)
    y = jnp.concatenate(y_parts, axis=1)

    if d == 0:
        yf_scr[blk] = y
    else:
        y = (y + yf_scr[blk] + xs * dskip_ref[...]) * _silu(z_ref[...])
        nw = nw_ref[...]
        W = SSD_WIDTH // G
        outs = [_rms(y[:, g * W:(g + 1) * W], nw[:, g * W:(g + 1) * W]) for g in range(G)]
        o_ref[...] = jnp.concatenate(outs, axis=1).astype(o_ref.dtype)


def _ssd_kernel(*refs, nblk):
    s_scr = refs[-1]
    s = pl.program_id(1)

    @pl.when((s == 0) | (s == nblk))
    def _():
        s_scr[...] = jnp.zeros_like(s_scr)

    @pl.when(s < nblk)
    def _():
        _ssd_block(*refs, d=0, blk=s)

    @pl.when(s >= nblk)
    def _():
        _ssd_block(*refs, d=1, blk=2 * nblk - 1 - s)


def _ssd(proj, xc, dtb_r, dtb_c, al_r, al_c, E, dskip, nw):
    B, Lp, _ = proj.shape
    nblk = Lp // BLK
    const2 = lambda b, s: (0, 0)
    return pl.pallas_call(
        functools.partial(_ssd_kernel, nblk=nblk),
        grid=(B, 2 * nblk),
        in_specs=[
            pl.BlockSpec((None, BLK, SSD_CONV_DIM), lambda b, s: (b, _scan_blk(s, nblk), 0)),
            pl.BlockSpec((None, BLK, SSD_WIDTH), lambda b, s: (b, _scan_blk(s, nblk), C_Z // SSD_WIDTH)),
            pl.BlockSpec((None, BLK, 128), lambda b, s: (b, _scan_blk(s, nblk), C_MISC // 128)),
            pl.BlockSpec((1, 128), const2),
            pl.BlockSpec((128, 128), const2),
            pl.BlockSpec((1, 128), const2),
            pl.BlockSpec((128, 128), const2),
            pl.BlockSpec((2, 128, SSD_WIDTH), lambda b, s: (0, 0, 0)),
            pl.BlockSpec((1, SSD_WIDTH), const2),
            pl.BlockSpec((1, SSD_WIDTH), const2),
        ],
        out_specs=pl.BlockSpec((None, BLK, SSD_WIDTH), lambda b, s: (b, _scan_out_blk(s, nblk), 0)),
        out_shape=jax.ShapeDtypeStruct((B, Lp, SSD_WIDTH), BF16),
        scratch_shapes=[
            pltpu.VMEM((nblk, BLK, SSD_WIDTH), F32),
            pltpu.VMEM((SSD_GROUPS, SSD_STATE, SSD_WIDTH // SSD_GROUPS), F32),
        ],
        compiler_params=_params(("arbitrary", "arbitrary")),
        name="ssd_scan",
    )(xc, proj, proj, dtb_r, dtb_c, al_r, al_c, E, dskip, nw)


def _permute_w_in(w):
    o = np.cumsum([0, 256, 256, 512, 512, 32, 512, 512, 512, 512, 1024, 1536, 32])
    gq, gk, gv, gg, gcode, dq, dk, dv, dg, z, xbc, dt, end = [int(v) for v in o]
    pieces = [w[..., gq:gcode], w[..., dq:dg], w[..., z:xbc], w[..., dg:z], w[..., xbc:dt],
              w[..., gcode:dq], w[..., dt:end],
              jnp.zeros(w.shape[:-1] + (N_PROJ - C_MISC - 64,), w.dtype)]
    return jnp.concatenate(pieces, axis=-1)


def _expansion():
    E = np.zeros((2, 128, SSD_WIDTH), np.float32)
    for d in range(2):
        for h in range(SSD_HEADS):
            E[d, DT_LANE0 + SSD_HEADS * d + h, h * SSD_HEADDIM:(h + 1) * SSD_HEADDIM] = 1.0
    return jnp.asarray(E, BF16)


def _misc_row(p):
    flat = p.reshape(p.shape[0], 1, 2 * SSD_HEADS).astype(F32)
    return jnp.pad(flat, ((0, 0), (0, 0), (DT_LANE0, 128 - DT_LANE0 - 2 * SSD_HEADS)))


def kernel(x, meta_tokens, rel_bias, final_norm_w, norm_w, w_in, w_out, gla_wa2, gla_ba, gla_norm_w,
           diff_lambda, diff_norm_w, conv_w, conv_b, ssd_A_log, ssd_dt_bias, ssd_D, ssd_norm_w):
    B, S, D = x.shape
    assert D == D_MODEL and S % BLK == 0
    Lp = TOK0 + S
    T = 384 if Lp % 384 == 0 else 128

    h = jnp.concatenate([
        jnp.zeros((B, PAD, D), x.dtype),
        jnp.broadcast_to(meta_tokens[None].astype(x.dtype), (B, N_META, D)),
        x], axis=1)

    band = _bias_band(rel_bias, T)
    E = _expansion()
    lam_init = np.array([0.8 - 0.6 * math.exp(-0.3 * l) for l in range(DEPTH)], np.float32)
    lcoef = np.zeros((DEPTH, 1, 128), np.float32)
    lcoef[:, 0, 0] = lam_init
    lcoef[:, 0, 1] = 1.0 - lam_init

    kd = GLA_HEADS * GLA_DK
    wa_p = jnp.zeros((DEPTH, 2, 128, kd), F32)
    wa_p = wa_p.at[:, 0, 0:GLA_RANK].set(gla_wa2[:, 0].astype(F32))
    wa_p = wa_p.at[:, 1, GLA_RANK:2 * GLA_RANK].set(gla_wa2[:, 1].astype(F32))
    dtb_r = _misc_row(ssd_dt_bias)
    al_r = _misc_row(ssd_A_log)
    layers = dict(
        norm_w=norm_w,
        w_in=_permute_w_in(w_in).astype(BF16),
        w_out=w_out.astype(BF16),
        wa_p=wa_p,
        ba=gla_ba.reshape(DEPTH, 2, 1, kd).astype(F32),
        gla_nw=gla_norm_w.reshape(DEPTH, 1, GLA_DV).astype(F32),
        lam=diff_lambda.astype(F32),
        lcoef=jnp.asarray(lcoef),
        diff_nw=diff_norm_w.reshape(DEPTH, 1, DIFF_DV).astype(F32),
        conv_w=conv_w.astype(F32),
        conv_b=conv_b.astype(F32),
        dtb_r=dtb_r,
        dtb_c=jnp.broadcast_to(jnp.swapaxes(dtb_r, 1, 2), (DEPTH, 128, 128)),
        al_r=al_r,
        al_c=jnp.broadcast_to(jnp.swapaxes(al_r, 1, 2), (DEPTH, 128, 128)),
        dskip=jnp.repeat(ssd_D.astype(F32), SSD_HEADDIM, axis=-1).reshape(DEPTH, 1, SSD_WIDTH),
        ssd_nw=ssd_norm_w.reshape(DEPTH, 1, SSD_WIDTH).astype(F32),
    )

    def layer(h, p):
        proj = _inproj(h, p["norm_w"], p["w_in"])
        o_gla = _gla(proj, p["wa_p"], p["ba"], p["gla_nw"])
        o_diff = _attn(proj, band, p["lam"], p["lcoef"], p["diff_nw"], T)
        xc = _conv(proj, p["conv_w"], p["conv_b"])
        o_ssd = _ssd(proj, xc, p["dtb_r"], p["dtb_c"], p["al_r"], p["al_c"], E, p["dskip"], p["ssd_nw"])
        return _outproj(h, o_gla, o_diff, o_ssd, p["w_out"]), None

    h, _ = lax.scan(layer, h, layers)
    return _final_norm(h, final_norm_w)
```

```python
import functools
import math

import numpy as np
import jax
import jax.numpy as jnp
from jax import lax
from jax.experimental import pallas as pl
from jax.experimental.pallas import tpu as pltpu

F32 = jnp.float32
BF16 = jnp.bfloat16

D_MODEL = 2048
DEPTH = 4
N_META = 16
EPS = 1e-6
GLA_HEADS = 4
GLA_DK = 64
GLA_DV = 128
GLA_WIDTH = 512
GLA_RANK = 16
GLA_TAU = 16.0
GLA_CHUNK = 64
DIFF_HEADS = 4
DIFF_DQK = 64
DIFF_DV = 128
DIFF_WIDTH = 512
N_BUCKETS = 32
MAX_DISTANCE = 128
SSD_WIDTH = 1024
SSD_HEADDIM = 64
SSD_HEADS = 16
SSD_GROUPS = 2
SSD_STATE = 128
SSD_CONV = 5
SSD_CHUNK = 128
SSD_CONV_DIM = SSD_WIDTH + 2 * SSD_GROUPS * SSD_STATE

BLK = 128
TOK0 = BLK
PAD = TOK0 - N_META

C_GQ, C_GK, C_GV, C_GG = 0, 256, 512, 1024
C_DQ, C_DK, C_DV = 1536, 2048, 2560
C_Z, C_DG, C_XBC, C_MISC = 3072, 4096, 4608, 6144
N_PROJ = 6272
DT_LANE0 = 2 * GLA_RANK

VMEM_LIMIT = 56 * 1024 * 1024
LOG2E = 1.4426950408889634


def _params(sem, limit=VMEM_LIMIT):
    return pltpu.CompilerParams(dimension_semantics=sem, vmem_limit_bytes=limit)


def _silu(x):
    return x / (1.0 + jnp.exp(-x))


def _softplus(x):
    return jnp.maximum(x, 0.0) + jnp.log1p(jnp.exp(-jnp.abs(x)))


def _log_sigmoid(x):
    return jnp.minimum(x, 0.0) - jnp.log1p(jnp.exp(-jnp.abs(x)))


def _bdot(a, b):
    return jnp.dot(a.astype(BF16), b.astype(BF16), preferred_element_type=F32)


def _split3(x):
    hi = x.astype(BF16)
    r = x - hi.astype(F32)
    mid = r.astype(BF16)
    lo = (r - mid.astype(F32)).astype(BF16)
    return hi, mid, lo


def _dot_l3(a, b_exact):
    hi, mid, lo = _split3(a)
    d = functools.partial(jnp.dot, preferred_element_type=F32)
    return d(hi, b_exact) + d(mid, b_exact) + d(lo, b_exact)


def _dot_r3(a_exact, b):
    hi, mid, lo = _split3(b)
    d = functools.partial(jnp.dot, preferred_element_type=F32)
    return d(a_exact, hi) + d(a_exact, mid) + d(a_exact, lo)


def _dot_22(a, b):
    ah = a.astype(BF16)
    al = (a - ah.astype(F32)).astype(BF16)
    bh = b.astype(BF16)
    bl = (b - bh.astype(F32)).astype(BF16)
    d = functools.partial(jnp.dot, preferred_element_type=F32)
    return d(ah, bh) + d(ah, bl) + d(al, bh)


def _iota(shape, dim):
    return lax.broadcasted_iota(jnp.int32, shape, dim)


def _rms(x, w):
    return x * lax.rsqrt(jnp.mean(x * x, axis=-1, keepdims=True) + EPS) * w


def _inproj_kernel(h_ref, nw_ref, w_ref, wm_ref, o_ref, om_ref, u_scr, *, tm):
    @pl.when(pl.program_id(2) == 0)
    def _():
        x = h_ref[...]
        y = _rms(x, nw_ref[...])
        row = pl.program_id(1) * tm + _iota((tm, 1), 0)
        u_scr[...] = jnp.where(row >= PAD, y, 0.0).astype(BF16)
        om_ref[...] = jnp.dot(u_scr[...], wm_ref[...], preferred_element_type=F32)

    o_ref[...] = jnp.dot(u_scr[...], w_ref[...], preferred_element_type=F32).astype(o_ref.dtype)


def _inproj(h, norm_w, w_p):
    B, Lp, D = h.shape
    tm = Lp // 4
    tn = C_MISC // 4
    return pl.pallas_call(
        functools.partial(_inproj_kernel, tm=tm),
        grid=(B, Lp // tm, C_MISC // tn),
        in_specs=[
            pl.BlockSpec((None, tm, D), lambda b, i, j: (b, i, 0)),
            pl.BlockSpec((1, D), lambda b, i, j: (0, 0)),
            pl.BlockSpec((D, tn), lambda b, i, j: (0, j)),
            pl.BlockSpec((D, 128), lambda b, i, j: (0, C_MISC // 128)),
        ],
        out_specs=[
            pl.BlockSpec((None, tm, tn), lambda b, i, j: (b, i, j)),
            pl.BlockSpec((None, tm, 128), lambda b, i, j: (b, i, 0)),
        ],
        out_shape=[jax.ShapeDtypeStruct((B, Lp, C_MISC), BF16),
                   jax.ShapeDtypeStruct((B, Lp, 128), F32)],
        scratch_shapes=[pltpu.VMEM((tm, D), BF16)],
        compiler_params=_params(("arbitrary", "arbitrary", "arbitrary")),
        name="inproj",
    )(h, norm_w.reshape(1, D), w_p, w_p)


def _outproj_kernel(h_ref, a_ref, b_ref, c_ref, w_ref, o_ref):
    d = functools.partial(jnp.dot, preferred_element_type=F32)
    acc = d(a_ref[...], w_ref[0:GLA_WIDTH, :])
    acc += d(b_ref[...], w_ref[GLA_WIDTH:GLA_WIDTH + DIFF_WIDTH, :])
    acc += d(c_ref[...], w_ref[GLA_WIDTH + DIFF_WIDTH:, :])
    o_ref[...] = h_ref[...] + acc


def _outproj(h, o_gla, o_diff, o_ssd, w_out):
    B, Lp, D = h.shape
    tm = Lp // 4
    tn = D // 2
    return pl.pallas_call(
        _outproj_kernel,
        grid=(B, Lp // tm, D // tn),
        in_specs=[
            pl.BlockSpec((None, tm, tn), lambda b, i, j: (b, i, j)),
            pl.BlockSpec((None, tm, GLA_WIDTH), lambda b, i, j: (b, i, 0)),
            pl.BlockSpec((None, tm, DIFF_WIDTH), lambda b, i, j: (b, i, 0)),
            pl.BlockSpec((None, tm, SSD_WIDTH), lambda b, i, j: (b, i, 0)),
            pl.BlockSpec((D, tn), lambda b, i, j: (0, j)),
        ],
        out_specs=pl.BlockSpec((None, tm, tn), lambda b, i, j: (b, i, j)),
        out_shape=jax.ShapeDtypeStruct((B, Lp, D), F32),
        input_output_aliases={0: 0},
        compiler_params=_params(("arbitrary", "arbitrary", "arbitrary")),
        name="outproj",
    )(h, o_gla, o_diff, o_ssd, w_out)


def _final_kernel(h_ref, w_ref, o_ref):
    o_ref[...] = _rms(h_ref[...], w_ref[...])


def _final_norm(h, w):
    B, Lp, D = h.shape
    S = Lp - TOK0
    return pl.pallas_call(
        _final_kernel,
        grid=(B, S // BLK),
        in_specs=[
            pl.BlockSpec((None, BLK, D), lambda b, i: (b, i + 1, 0)),
            pl.BlockSpec((1, D), lambda b, i: (0, 0)),
        ],
        out_specs=pl.BlockSpec((None, BLK, D), lambda b, i: (b, i, 0)),
        out_shape=jax.ShapeDtypeStruct((B, S, D), F32),
        compiler_params=_params(("arbitrary", "arbitrary")),
        name="final_norm",
    )(h, w.reshape(1, D))


def _gla_block(q_ref, k_ref, v_ref, gate_ref, misc_ref, wa_ref, ba_ref, nw_ref, o_ref,
               of_scr, s_scr, *, d, blk):
    C = GLA_CHUNK
    row = _iota((BLK, 1), 0)
    valid = (blk * BLK + row) >= PAD
    x = _dot_22(misc_ref[...], wa_ref[d]) + ba_ref[d]
    g = jnp.where(valid, _log_sigmoid(x) * (1.0 / GLA_TAU), 0.0)

    r = _iota((BLK, BLK), 0)
    c = _iota((BLK, BLK), 1)
    same = (r >= C) == (c >= C)
    tri = (same & (c <= r)) if d == 0 else (same & (c >= r))
    b = _dot_r3(jnp.where(tri, 1.0, 0.0).astype(BF16), g)
    first, second = (C - 1, 2 * C - 1) if d == 0 else (0, C)
    blast = jnp.where(row < C, b[first:first + 1, :], b[second:second + 1, :])

    k = k_ref[...].astype(F32)
    q_in = q_ref[...].astype(F32) * (GLA_DK ** -0.5) * jnp.exp(b)
    kT = k.T
    bT = b.T
    lane = _iota((1, BLK), 1)
    blastT = jnp.where(lane < C, bT[:, first:first + 1], bT[:, second:second + 1])
    kinT = (kT * jnp.exp(-bT)).astype(BF16)
    koutT = kT * jnp.exp(blastT - bT)
    vb = v_ref[...]

    lane_k = _iota((1, GLA_HEADS * GLA_DK), 1)
    qm = [jnp.where((lane_k >= h * GLA_DK) & (lane_k < (h + 1) * GLA_DK), q_in, 0.0).astype(BF16)
          for h in range(GLA_HEADS)]
    o_intra = []
    for h in range(GLA_HEADS):
        att = jnp.dot(qm[h], kinT, preferred_element_type=F32)
        att = jnp.where(tri, att, 0.0)
        o_intra.append(_bdot(att, vb[:, h * GLA_DV:(h + 1) * GLA_DV]))

    S = s_scr[...]
    o_inter = [[None, None] for _ in range(GLA_HEADS)]
    for cc in ((0, 1) if d == 0 else (1, 0)):
        Sb = S.astype(BF16)
        for h in range(GLA_HEADS):
            o_inter[h][cc] = jnp.dot(qm[h][cc * C:(cc + 1) * C, :], Sb, preferred_element_type=F32)
        tot = first if cc == 0 else second
        dec = jnp.exp(bT[:, tot:tot + 1])
        kc = jnp.where((lane >= cc * C) & (lane < (cc + 1) * C), koutT, 0.0).astype(BF16)
        upd = [jnp.dot(kc[h * GLA_DK:(h + 1) * GLA_DK, :], vb[:, h * GLA_DV:(h + 1) * GLA_DV],
                       preferred_element_type=F32) for h in range(GLA_HEADS)]
        S = S * dec + jnp.concatenate(upd, axis=0)
    s_scr[...] = S

    o = jnp.concatenate(
        [o_intra[h] + jnp.concatenate(o_inter[h], axis=0) for h in range(GLA_HEADS)], axis=1)
    if d == 0:
        of_scr[blk] = o
    else:
        o = o + of_scr[blk]
        gate = gate_ref[...].astype(F32)
        nw = nw_ref[...]
        outs = []
        for h in range(GLA_HEADS):
            sl = slice(h * GLA_DV, (h + 1) * GLA_DV)
            outs.append(_rms(o[:, sl], nw) * _silu(gate[:, sl]))
        o_ref[...] = jnp.concatenate(outs, axis=1).astype(o_ref.dtype)


def _gla_kernel(*refs, nblk):
    s_scr = refs[-1]
    s = pl.program_id(1)

    @pl.when((s == 0) | (s == nblk))
    def _():
        s_scr[...] = jnp.zeros_like(s_scr)

    @pl.when(s < nblk)
    def _():
        _gla_block(*refs, d=0, blk=s)

    @pl.when(s >= nblk)
    def _():
        _gla_block(*refs, d=1, blk=2 * nblk - 1 - s)


def _scan_blk(s, nblk):
    return jnp.where(s < nblk, s, 2 * nblk - 1 - s)


def _scan_out_blk(s, nblk):
    return jnp.where(s < nblk, nblk - 1, 2 * nblk - 1 - s)


def _gla(proj, misc, wa_p, ba, nw):
    B, Lp, _ = proj.shape
    nblk = Lp // BLK
    kd = GLA_HEADS * GLA_DK
    im = lambda col: (lambda b, s: (b, _scan_blk(s, nblk), col))
    return pl.pallas_call(
        functools.partial(_gla_kernel, nblk=nblk),
        grid=(B, 2 * nblk),
        in_specs=[
            pl.BlockSpec((None, BLK, kd), im(C_GQ // kd)),
            pl.BlockSpec((None, BLK, kd), im(C_GK // kd)),
            pl.BlockSpec((None, BLK, GLA_WIDTH), im(C_GV // GLA_WIDTH)),
            pl.BlockSpec((None, BLK, GLA_WIDTH), im(C_GG // GLA_WIDTH)),
            pl.BlockSpec((None, BLK, 128), im(0)),
            pl.BlockSpec((2, 128, kd), lambda b, s: (0, 0, 0)),
            pl.BlockSpec((2, 1, kd), lambda b, s: (0, 0, 0)),
            pl.BlockSpec((1, GLA_DV), lambda b, s: (0, 0)),
        ],
        out_specs=pl.BlockSpec((None, BLK, GLA_WIDTH), lambda b, s: (b, _scan_out_blk(s, nblk), 0)),
        out_shape=jax.ShapeDtypeStruct((B, Lp, GLA_WIDTH), BF16),
        scratch_shapes=[pltpu.VMEM((nblk, BLK, GLA_WIDTH), F32), pltpu.VMEM((kd, GLA_DV), F32)],
        compiler_params=_params(("arbitrary", "arbitrary")),
        name="gla",
    )(proj, proj, proj, proj, misc, wa_p, ba, nw)


def _attn_kernel(q_ref, k_ref, v_ref, g_ref, band_ref, lam_ref, lc_ref, nw_ref, o_ref,
                 kT_scr, vb_scr, s_scr, e_scr, *, T, nk, RC, KC):
    i = pl.program_id(2)

    @pl.when(i == 0)
    def _():
        for j in range(nk):
            kT_scr[:, j * T:(j + 1) * T] = k_ref[j * T:(j + 1) * T, :].astype(F32).T.astype(BF16)
        vb_scr[...] = v_ref[...]

    def fold(x, op):
        r = x[:, 0:128]
        for t in range(1, KC // 128):
            r = op(r, x[:, t * 128:(t + 1) * 128])
        return r

    lane = _iota((1, 2 * DIFF_DQK), 1)
    q = q_ref[...].astype(F32) * (DIFF_DQK ** -0.5 * LOG2E)
    col0 = _iota((1, KC), 1)
    qcs = [jnp.where((lane >= c * DIFF_DQK) & (lane < (c + 1) * DIFF_DQK), q, 0.0).astype(BF16)
           for c in range(2)]
    groups = [(c, rc) for c in range(2) for rc in range(T // RC)]
    n_sub = nk * T // KC
    bidx = [jnp.clip(j - i, -2, 2) + 2 for j in range(nk)]

    def score_step(g, u, m_run):
        c, rc = g
        rows = slice(rc * RC, (rc + 1) * RC)
        cols = slice(u * KC, (u + 1) * KC)
        j, t = divmod(u, T // KC)
        s = jnp.dot(qcs[c][rows], kT_scr[:, cols], preferred_element_type=F32)
        s = s + band_ref[bidx[j], rows, t * KC:(t + 1) * KC]
        if u == 0:
            s = jnp.where(col0 >= PAD, s, -1e30)
        s_scr[c, rows, cols] = s
        mt = fold(s, jnp.maximum)
        return mt if m_run is None else jnp.maximum(m_run, mt)

    def prob_step(g, u, m, l_run):
        c, rc = g
        rows = slice(rc * RC, (rc + 1) * RC)
        cols = slice(u * KC, (u + 1) * KC)
        e = jnp.exp2(s_scr[c, rows, cols] - m)
        e_scr[c, rows, cols] = e.astype(BF16)
        lt = fold(e, jnp.add)
        return lt if l_run is None else l_run + lt

    def pv_step(g, j, pv):
        c, rc = g
        rows = slice(rc * RC, (rc + 1) * RC)
        cols = slice(j * T, (j + 1) * T)
        d = jnp.dot(e_scr[c, rows, cols], vb_scr[cols, :], preferred_element_type=F32)
        return d if pv is None else pv + d

    per_tile = T // KC
    m_run = None
    for u in range(n_sub):
        m_run = score_step(groups[0], u, m_run)
    parts = []
    for gi, g in enumerate(groups):
        m = jnp.max(m_run, axis=-1, keepdims=True)
        m_run = l_run = pv = None
        for u in range(n_sub):
            l_run = prob_step(g, u, m, l_run)
            if u % per_tile == per_tile - 1 and u >= 2 * per_tile - 1:
                pv = pv_step(g, u // per_tile - 1, pv)
            if gi + 1 < len(groups):
                m_run = score_step(groups[gi + 1], u, m_run)
        pv = pv_step(g, nk - 1, pv)
        parts.append(pv / jnp.sum(l_run, axis=-1, keepdims=True))
    per_half = len(groups) // 2
    outs = [jnp.concatenate(parts[:per_half], axis=0), jnp.concatenate(parts[per_half:], axis=0)]

    lp = lam_ref[...]
    lc = lc_ref[...]
    lam = (jnp.exp(jnp.sum(lp[0:1] * lp[1:2], axis=-1, keepdims=True))
           - jnp.exp(jnp.sum(lp[2:3] * lp[3:4], axis=-1, keepdims=True)) + lc[:, 0:1])
    o = outs[0] - lam * outs[1]
    y = _rms(o, nw_ref[...]) * lc[:, 1:2]
    o_ref[...] = (y * _silu(g_ref[...].astype(F32))).astype(o_ref.dtype)


def _attn(proj, band, diff_lambda, lcoef, nw, T):
    B, Lp, _ = proj.shape
    nk = Lp // T
    H = DIFF_HEADS
    RC = T // 2 if T > 128 else T
    KC = 128
    return pl.pallas_call(
        functools.partial(_attn_kernel, T=T, nk=nk, RC=RC, KC=KC),
        grid=(B, H, nk),
        in_specs=[
            pl.BlockSpec((None, T, 128), lambda b, h, i: (b, i, C_DQ // 128 + h)),
            pl.BlockSpec((None, Lp, 128), lambda b, h, i: (b, 0, C_DK // 128 + h)),
            pl.BlockSpec((None, Lp, 128), lambda b, h, i: (b, 0, C_DV // 128 + h)),
            pl.BlockSpec((None, T, 128), lambda b, h, i: (b, i, C_DG // 128 + h)),
            pl.BlockSpec((None, 5, T, T), lambda b, h, i: (h, 0, 0, 0)),
            pl.BlockSpec((4, DIFF_DQK), lambda b, h, i: (0, 0)),
            pl.BlockSpec((1, 128), lambda b, h, i: (0, 0)),
            pl.BlockSpec((1, DIFF_DV), lambda b, h, i: (0, 0)),
        ],
        out_specs=pl.BlockSpec((None, T, 128), lambda b, h, i: (b, i, h)),
        out_shape=jax.ShapeDtypeStruct((B, Lp, DIFF_WIDTH), BF16),
        scratch_shapes=[
            pltpu.VMEM((128, Lp), BF16),
            pltpu.VMEM((Lp, 128), BF16),
            pltpu.VMEM((2, T, Lp), F32),
            pltpu.VMEM((2, T, Lp), BF16),
        ],
        compiler_params=_params(("arbitrary", "arbitrary", "arbitrary")),
        name="diff_attn",
    )(proj, proj, proj, proj, band, diff_lambda, lcoef, nw)


def _bucket_table(T):
    nb = N_BUCKETS // 2
    max_exact = nb // 2
    rel = np.arange(6 * T) - (3 * T - 1)
    ret = np.where(rel > 0, nb, 0)
    n = np.abs(rel)
    nf = np.maximum(n, 1).astype(np.float64)
    large = max_exact + (np.log(nf / max_exact) / math.log(MAX_DISTANCE / max_exact)
                         * (nb - max_exact)).astype(np.int32)
    large = np.minimum(large, nb - 1)
    return (ret + np.where(n < max_exact, n, large)).astype(np.int32)


def _bias_band(rel_bias, T):
    n = 6 * T
    v = (rel_bias.astype(F32)[_bucket_table(T)] * LOG2E).T
    rows = jnp.tile(v, (1, T))[:, :T * (n - 1)].reshape(DIFF_HEADS, T, n - 1)
    band = rows[:, :, T - 1:n - 1].reshape(DIFF_HEADS, T, 5, T)
    return jnp.transpose(band, (0, 2, 1, 3))


def _conv_kernel(x_ref, w_ref, b_ref, o_ref):
    x = x_ref[...].astype(F32)
    w = w_ref[...]
    n = x.shape[0]
    half = (SSD_CONV - 1) // 2
    acc = x * w[half:half + 1, :] + b_ref[...]
    for kk in range(SSD_CONV):
        if kk != half:
            acc = acc + pltpu.roll(x, (half - kk) % n, 0) * w[kk:kk + 1, :]
    o_ref[...] = _silu(acc).astype(o_ref.dtype)


def _conv(proj, conv_w, conv_b):
    B, Lp, _ = proj.shape
    tn = 256
    return pl.pallas_call(
        _conv_kernel,
        grid=(B, SSD_CONV_DIM // tn),
        in_specs=[
            pl.BlockSpec((None, Lp, tn), lambda b, j: (b, 0, C_XBC // tn + j)),
            pl.BlockSpec((SSD_CONV, tn), lambda b, j: (0, j)),
            pl.BlockSpec((1, tn), lambda b, j: (0, j)),
        ],
        out_specs=pl.BlockSpec((None, Lp, tn), lambda b, j: (b, 0, j)),
        out_shape=jax.ShapeDtypeStruct((B, Lp, SSD_CONV_DIM), BF16),
        compiler_params=_params(("arbitrary", "arbitrary")),
        name="ssd_conv",
    )(proj, conv_w, conv_b.reshape(1, SSD_CONV_DIM))


def _ssd_block(xc_ref, z_ref, misc_ref, dtb_r_ref, dtb_c_ref, al_r_ref, al_c_ref, e_ref, dskip_ref,
               nw_ref, o_ref, yf_scr, s_scr, *, d, blk):
    G, N, P = SSD_GROUPS, SSD_STATE, SSD_HEADDIM
    R = SSD_HEADS // G
    xs = xc_ref[:, 0:SSD_WIDTH].astype(F32)
    Bm = xc_ref[:, SSD_WIDTH:SSD_WIDTH + G * N].astype(F32)
    Cb = xc_ref[:, SSD_WIDTH + G * N:]
    misc = misc_ref[...]
    miscT = misc.T
    row = _iota((BLK, 1), 0)
    lane = _iota((1, BLK), 1)
    dt = jnp.where((blk * BLK + row) >= PAD, _softplus(misc + dtb_r_ref[...]), 0.0)
    a = dt * (-jnp.exp(al_r_ref[...]))
    dtT = jnp.where((blk * BLK + lane) >= PAD, _softplus(miscT + dtb_c_ref[...]), 0.0)
    aT = dtT * (-jnp.exp(al_c_ref[...]))

    r = _iota((BLK, BLK), 0)
    c = _iota((BLK, BLK), 1)
    tri = (c <= r) if d == 0 else (c >= r)
    triT = (r <= c) if d == 0 else (r >= c)
    cum = _dot_r3(jnp.where(tri, 1.0, 0.0).astype(BF16), a)
    cumT = _dot_l3(aT, jnp.where(triT, 1.0, 0.0).astype(BF16))

    lo = DT_LANE0 + SSD_HEADS * d
    hm = (lane >= lo) & (lane < lo + SSD_HEADS)
    last = BLK - 1 if d == 0 else 0
    tot = cum[last:last + 1, :]
    E = e_ref[d]
    dt_x = _dot_l3(jnp.where(hm, dt, 0.0), E)
    ecum_x = _dot_l3(jnp.where(hm, jnp.exp(cum), 0.0), E)
    toend_x = _dot_l3(jnp.where(hm, jnp.exp(tot - cum), 0.0), E)
    etot_x = ecum_x[last:last + 1, :]
    xd = xs * dt_x
    xdb = xd.astype(BF16)
    xdw = (xd * toend_x).astype(BF16)

    y_parts = []
    for g in range(G):
        BgT = Bm[:, g * N:(g + 1) * N].T.astype(BF16)
        Cg = Cb[:, g * N:(g + 1) * N]
        cols = slice(g * R * P, (g + 1) * R * P)
        CB = jnp.dot(Cg, BgT, preferred_element_type=F32)
        S = s_scr[g]
        y_off = jnp.dot(Cg, S.astype(BF16), preferred_element_type=F32) * ecum_x[:, cols]
        s_scr[g] = S * etot_x[:, cols] + jnp.dot(BgT, xdw[:, cols], preferred_element_type=F32)
        for pr in range(R // 2):
            h0 = g * R + 2 * pr
            xpair = xdb[:, h0 * P:(h0 + 2) * P]
            acc = y_off[:, 2 * pr * P:(2 * pr + 2) * P]
            for hh in range(2):
                li = lo + h0 + hh
                seg = cum[:, li:li + 1] - cumT[li:li + 1, :]
                dec = jnp.exp(jnp.where(tri, seg, -jnp.inf))
                sc = (CB * dec).astype(BF16)
                xm = jnp.where((lane >= hh * P) & (lane < (hh + 1) * P), xpair, 0.0)
                acc = acc + jnp.dot(sc, xm, preferred_element_type=F32)
            y_parts.append(acc)
    y = jnp.concatenate(y_parts, axis=1)

    if d == 0:
        yf_scr[blk] = y
    else:
        y = (y + yf_scr[blk] + xs * dskip_ref[...]) * _silu(z_ref[...].astype(F32))
        nw = nw_ref[...]
        W = SSD_WIDTH // G
        outs = [_rms(y[:, g * W:(g + 1) * W], nw[:, g * W:(g + 1) * W]) for g in range(G)]
        o_ref[...] = jnp.concatenate(outs, axis=1).astype(o_ref.dtype)


def _ssd_kernel(*refs, nblk):
    s_scr = refs[-1]
    s = pl.program_id(1)

    @pl.when((s == 0) | (s == nblk))
    def _():
        s_scr[...] = jnp.zeros_like(s_scr)

    @pl.when(s < nblk)
    def _():
        _ssd_block(*refs, d=0, blk=s)

    @pl.when(s >= nblk)
    def _():
        _ssd_block(*refs, d=1, blk=2 * nblk - 1 - s)


def _ssd(proj, misc, xc, dtb_r, dtb_c, al_r, al_c, E, dskip, nw):
    B, Lp, _ = proj.shape
    nblk = Lp // BLK
    const2 = lambda b, s: (0, 0)
    return pl.pallas_call(
        functools.partial(_ssd_kernel, nblk=nblk),
        grid=(B, 2 * nblk),
        in_specs=[
            pl.BlockSpec((None, BLK, SSD_CONV_DIM), lambda b, s: (b, _scan_blk(s, nblk), 0)),
            pl.BlockSpec((None, BLK, SSD_WIDTH), lambda b, s: (b, _scan_blk(s, nblk), C_Z // SSD_WIDTH)),
            pl.BlockSpec((None, BLK, 128), lambda b, s: (b, _scan_blk(s, nblk), 0)),
            pl.BlockSpec((1, 128), const2),
            pl.BlockSpec((128, 128), const2),
            pl.BlockSpec((1, 128), const2),
            pl.BlockSpec((128, 128), const2),
            pl.BlockSpec((2, 128, SSD_WIDTH), lambda b, s: (0, 0, 0)),
            pl.BlockSpec((1, SSD_WIDTH), const2),
            pl.BlockSpec((1, SSD_WIDTH), const2),
        ],
        out_specs=pl.BlockSpec((None, BLK, SSD_WIDTH), lambda b, s: (b, _scan_out_blk(s, nblk), 0)),
        out_shape=jax.ShapeDtypeStruct((B, Lp, SSD_WIDTH), BF16),
        scratch_shapes=[
            pltpu.VMEM((nblk, BLK, SSD_WIDTH), F32),
            pltpu.VMEM((SSD_GROUPS, SSD_STATE, SSD_WIDTH // SSD_GROUPS), F32),
        ],
        compiler_params=_params(("arbitrary", "arbitrary")),
        name="ssd_scan",
    )(xc, proj, misc, dtb_r, dtb_c, al_r, al_c, E, dskip, nw)


def _permute_w_in(w):
    o = np.cumsum([0, 256, 256, 512, 512, 32, 512, 512, 512, 512, 1024, 1536, 32])
    gq, gk, gv, gg, gcode, dq, dk, dv, dg, z, xbc, dt, end = [int(v) for v in o]
    pieces = [w[..., gq:gcode], w[..., dq:dg], w[..., z:xbc], w[..., dg:z], w[..., xbc:dt],
              w[..., gcode:dq], w[..., dt:end],
              jnp.zeros(w.shape[:-1] + (N_PROJ - C_MISC - 64,), w.dtype)]
    return jnp.concatenate(pieces, axis=-1)


def _expansion():
    E = np.zeros((2, 128, SSD_WIDTH), np.float32)
    for d in range(2):
        for h in range(SSD_HEADS):
            E[d, DT_LANE0 + SSD_HEADS * d + h, h * SSD_HEADDIM:(h + 1) * SSD_HEADDIM] = 1.0
    return jnp.asarray(E, BF16)


def _misc_row(p):
    flat = p.reshape(p.shape[0], 1, 2 * SSD_HEADS).astype(F32)
    return jnp.pad(flat, ((0, 0), (0, 0), (DT_LANE0, 128 - DT_LANE0 - 2 * SSD_HEADS)))


def kernel(x, meta_tokens, rel_bias, final_norm_w, norm_w, w_in, w_out, gla_wa2, gla_ba, gla_norm_w,
           diff_lambda, diff_norm_w, conv_w, conv_b, ssd_A_log, ssd_dt_bias, ssd_D, ssd_norm_w):
    B, S, D = x.shape
    assert D == D_MODEL and S % BLK == 0
    Lp = TOK0 + S
    T = 384 if Lp % 384 == 0 else 128

    h = jnp.concatenate([
        jnp.zeros((B, PAD, D), x.dtype),
        jnp.broadcast_to(meta_tokens[None].astype(x.dtype), (B, N_META, D)),
        x], axis=1)

    band = _bias_band(rel_bias, T)
    E = _expansion()
    lam_init = np.array([0.8 - 0.6 * math.exp(-0.3 * l) for l in range(DEPTH)], np.float32)
    lcoef = np.zeros((DEPTH, 1, 128), np.float32)
    lcoef[:, 0, 0] = lam_init
    lcoef[:, 0, 1] = 1.0 - lam_init

    kd = GLA_HEADS * GLA_DK
    wa_p = jnp.zeros((DEPTH, 2, 128, kd), F32)
    wa_p = wa_p.at[:, 0, 0:GLA_RANK].set(gla_wa2[:, 0].astype(F32))
    wa_p = wa_p.at[:, 1, GLA_RANK:2 * GLA_RANK].set(gla_wa2[:, 1].astype(F32))
    dtb_r = _misc_row(ssd_dt_bias)
    al_r = _misc_row(ssd_A_log)
    layers = dict(
        norm_w=norm_w,
        w_in=_permute_w_in(w_in).astype(BF16),
        w_out=w_out.astype(BF16),
        wa_p=wa_p,
        ba=gla_ba.reshape(DEPTH, 2, 1, kd).astype(F32),
        gla_nw=gla_norm_w.reshape(DEPTH, 1, GLA_DV).astype(F32),
        lam=diff_lambda.astype(F32),
        lcoef=jnp.asarray(lcoef),
        diff_nw=diff_norm_w.reshape(DEPTH, 1, DIFF_DV).astype(F32),
        conv_w=conv_w.astype(F32),
        conv_b=conv_b.astype(F32),
        dtb_r=dtb_r,
        dtb_c=jnp.broadcast_to(jnp.swapaxes(dtb_r, 1, 2), (DEPTH, 128, 128)),
        al_r=al_r,
        al_c=jnp.broadcast_to(jnp.swapaxes(al_r, 1, 2), (DEPTH, 128, 128)),
        dskip=jnp.repeat(ssd_D.astype(F32), SSD_HEADDIM, axis=-1).reshape(DEPTH, 1, SSD_WIDTH),
        ssd_nw=ssd_norm_w.reshape(DEPTH, 1, SSD_WIDTH).astype(F32),
    )

    def layer(h, p):
        proj, misc = _inproj(h, p["norm_w"], p["w_in"])
        o_gla = _gla(proj, misc, p["wa_p"], p["ba"], p["gla_nw"])
        o_diff = _attn(proj, band, p["lam"], p["lcoef"], p["diff_nw"], T)
        xc = _conv(proj, p["conv_w"], p["conv_b"])
        o_ssd = _ssd(proj, misc, xc, p["dtb_r"], p["dtb_c"], p["al_r"], p["al_c"], E, p["dskip"], p["ssd_nw"])
        return _outproj(h, o_gla, o_diff, o_ssd, p["w_out"]), None

    h, _ = lax.scan(layer, h, layers)
    return _final_norm(h, final_norm_w)
```

```python
import functools
import math

import numpy as np
import jax
import jax.numpy as jnp
from jax import lax
from jax.experimental import pallas as pl
from jax.experimental.pallas import tpu as pltpu

F32 = jnp.float32
BF16 = jnp.bfloat16

D_MODEL = 2048
DEPTH = 4
N_META = 16
EPS = 1e-6
GLA_HEADS = 4
GLA_DK = 64
GLA_DV = 128
GLA_WIDTH = 512
GLA_RANK = 16
GLA_TAU = 16.0
GLA_CHUNK = 64
DIFF_HEADS = 4
DIFF_DQK = 64
DIFF_DV = 128
DIFF_WIDTH = 512
N_BUCKETS = 32
MAX_DISTANCE = 128
SSD_WIDTH = 1024
SSD_HEADDIM = 64
SSD_HEADS = 16
SSD_GROUPS = 2
SSD_STATE = 128
SSD_CONV = 5
SSD_CHUNK = 128
SSD_CONV_DIM = SSD_WIDTH + 2 * SSD_GROUPS * SSD_STATE

BLK = 128
TOK0 = BLK
PAD = TOK0 - N_META

C_GQ, C_GK, C_GV, C_GG = 0, 256, 512, 1024
C_DQ, C_DK, C_DV = 1536, 2048, 2560
C_Z, C_DG, C_XBC, C_MISC = 3072, 4096, 4608, 6144
N_PROJ = 6272
DT_LANE0 = 2 * GLA_RANK

VMEM_LIMIT = 56 * 1024 * 1024
LOG2E = 1.4426950408889634


def _params(sem, limit=VMEM_LIMIT):
    return pltpu.CompilerParams(dimension_semantics=sem, vmem_limit_bytes=limit)


def _silu(x):
    return x / (1.0 + jnp.exp(-x))


def _softplus(x):
    return jnp.maximum(x, 0.0) + jnp.log1p(jnp.exp(-jnp.abs(x)))


def _log_sigmoid(x):
    return jnp.minimum(x, 0.0) - jnp.log1p(jnp.exp(-jnp.abs(x)))


def _bdot(a, b):
    return jnp.dot(a.astype(BF16), b.astype(BF16), preferred_element_type=F32)


def _split3(x):
    hi = x.astype(BF16)
    r = x - hi.astype(F32)
    mid = r.astype(BF16)
    lo = (r - mid.astype(F32)).astype(BF16)
    return hi, mid, lo


def _dot_l3(a, b_exact):
    hi, mid, lo = _split3(a)
    d = functools.partial(jnp.dot, preferred_element_type=F32)
    return d(hi, b_exact) + d(mid, b_exact) + d(lo, b_exact)


def _dot_r3(a_exact, b):
    hi, mid, lo = _split3(b)
    d = functools.partial(jnp.dot, preferred_element_type=F32)
    return d(a_exact, hi) + d(a_exact, mid) + d(a_exact, lo)


def _dot_22(a, b):
    ah = a.astype(BF16)
    al = (a - ah.astype(F32)).astype(BF16)
    bh = b.astype(BF16)
    bl = (b - bh.astype(F32)).astype(BF16)
    d = functools.partial(jnp.dot, preferred_element_type=F32)
    return d(ah, bh) + d(ah, bl) + d(al, bh)


def _iota(shape, dim):
    return lax.broadcasted_iota(jnp.int32, shape, dim)


def _rms(x, w):
    return x * lax.rsqrt(jnp.mean(x * x, axis=-1, keepdims=True) + EPS) * w


def _inproj_kernel(h_ref, nw_ref, w_ref, wm_ref, o_ref, om_ref, u_scr, *, tm):
    @pl.when(pl.program_id(2) == 0)
    def _():
        x = h_ref[...]
        y = _rms(x, nw_ref[...])
        row = pl.program_id(1) * tm + _iota((tm, 1), 0)
        u_scr[...] = jnp.where(row >= PAD, y, 0.0).astype(BF16)
        om_ref[...] = jnp.dot(u_scr[...], wm_ref[...], preferred_element_type=F32)

    o_ref[...] = jnp.dot(u_scr[...], w_ref[...], preferred_element_type=F32).astype(o_ref.dtype)


def _inproj(h, norm_w, w_p):
    B, Lp, D = h.shape
    tm = Lp // 4
    tn = C_MISC // 4
    return pl.pallas_call(
        functools.partial(_inproj_kernel, tm=tm),
        grid=(B, Lp // tm, C_MISC // tn),
        in_specs=[
            pl.BlockSpec((None, tm, D), lambda b, i, j: (b, i, 0)),
            pl.BlockSpec((1, D), lambda b, i, j: (0, 0)),
            pl.BlockSpec((D, tn), lambda b, i, j: (0, j)),
            pl.BlockSpec((D, 128), lambda b, i, j: (0, C_MISC // 128)),
        ],
        out_specs=[
            pl.BlockSpec((None, tm, tn), lambda b, i, j: (b, i, j)),
            pl.BlockSpec((None, tm, 128), lambda b, i, j: (b, i, 0)),
        ],
        out_shape=[jax.ShapeDtypeStruct((B, Lp, C_MISC), BF16),
                   jax.ShapeDtypeStruct((B, Lp, 128), F32)],
        scratch_shapes=[pltpu.VMEM((tm, D), BF16)],
        compiler_params=_params(("arbitrary", "arbitrary", "arbitrary")),
        name="inproj",
    )(h, norm_w.reshape(1, D), w_p, w_p)


def _outproj_kernel(h_ref, a_ref, b_ref, c_ref, w_ref, o_ref):
    d = functools.partial(jnp.dot, preferred_element_type=F32)
    acc = d(a_ref[...], w_ref[0:GLA_WIDTH, :])
    acc += d(b_ref[...], w_ref[GLA_WIDTH:GLA_WIDTH + DIFF_WIDTH, :])
    acc += d(c_ref[...], w_ref[GLA_WIDTH + DIFF_WIDTH:, :])
    o_ref[...] = h_ref[...] + acc


def _outproj(h, o_gla, o_diff, o_ssd, w_out):
    B, Lp, D = h.shape
    tm = Lp // 4
    tn = D // 2
    return pl.pallas_call(
        _outproj_kernel,
        grid=(B, Lp // tm, D // tn),
        in_specs=[
            pl.BlockSpec((None, tm, tn), lambda b, i, j: (b, i, j)),
            pl.BlockSpec((None, tm, GLA_WIDTH), lambda b, i, j: (b, i, 0)),
            pl.BlockSpec((None, tm, DIFF_WIDTH), lambda b, i, j: (b, i, 0)),
            pl.BlockSpec((None, tm, SSD_WIDTH), lambda b, i, j: (b, i, 0)),
            pl.BlockSpec((D, tn), lambda b, i, j: (0, j)),
        ],
        out_specs=pl.BlockSpec((None, tm, tn), lambda b, i, j: (b, i, j)),
        out_shape=jax.ShapeDtypeStruct((B, Lp, D), F32),
        input_output_aliases={0: 0},
        compiler_params=_params(("arbitrary", "arbitrary", "arbitrary")),
        name="outproj",
    )(h, o_gla, o_diff, o_ssd, w_out)


def _final_kernel(h_ref, w_ref, o_ref):
    o_ref[...] = _rms(h_ref[...], w_ref[...])


def _final_norm(h, w):
    B, Lp, D = h.shape
    S = Lp - TOK0
    return pl.pallas_call(
        _final_kernel,
        grid=(B, S // BLK),
        in_specs=[
            pl.BlockSpec((None, BLK, D), lambda b, i: (b, i + 1, 0)),
            pl.BlockSpec((1, D), lambda b, i: (0, 0)),
        ],
        out_specs=pl.BlockSpec((None, BLK, D), lambda b, i: (b, i, 0)),
        out_shape=jax.ShapeDtypeStruct((B, S, D), F32),
        compiler_params=_params(("arbitrary", "arbitrary")),
        name="final_norm",
    )(h, w.reshape(1, D))


def _round_robin(gens):
    alive = list(gens)
    while alive:
        for g in list(alive):
            try:
                next(g)
            except StopIteration:
                alive.remove(g)


def _gla_block(bi, q_ref, k_ref, v_ref, gate_ref, misc_ref, wa_ref, ba_ref, nw_ref, o_ref,
               of_scr, s_scr, *, d, blk):
    C = GLA_CHUNK
    row = _iota((BLK, 1), 0)
    valid = (blk * BLK + row) >= PAD
    x = _dot_22(misc_ref[bi], wa_ref[d]) + ba_ref[d]
    yield
    g = jnp.where(valid, _log_sigmoid(x) * (1.0 / GLA_TAU), 0.0)

    r = _iota((BLK, BLK), 0)
    c = _iota((BLK, BLK), 1)
    same = (r >= C) == (c >= C)
    tri = (same & (c <= r)) if d == 0 else (same & (c >= r))
    b = _dot_r3(jnp.where(tri, 1.0, 0.0).astype(BF16), g)
    yield
    first, second = (C - 1, 2 * C - 1) if d == 0 else (0, C)

    k = k_ref[bi].astype(F32)
    q_in = q_ref[bi].astype(F32) * (GLA_DK ** -0.5) * jnp.exp(b)
    kT = k.T
    bT = b.T
    lane = _iota((1, BLK), 1)
    blastT = jnp.where(lane < C, bT[:, first:first + 1], bT[:, second:second + 1])
    kinT = (kT * jnp.exp(-bT)).astype(BF16)
    koutT = kT * jnp.exp(blastT - bT)
    vb = v_ref[bi]
    yield

    lane_k = _iota((1, GLA_HEADS * GLA_DK), 1)
    qm = [jnp.where((lane_k >= h * GLA_DK) & (lane_k < (h + 1) * GLA_DK), q_in, 0.0).astype(BF16)
          for h in range(GLA_HEADS)]
    att = [jnp.where(tri, jnp.dot(qm[h], kinT, preferred_element_type=F32), 0.0).astype(BF16)
           for h in range(GLA_HEADS)]
    yield
    o_intra = [jnp.dot(att[h], vb[:, h * GLA_DV:(h + 1) * GLA_DV], preferred_element_type=F32)
               for h in range(GLA_HEADS)]
    yield

    S = s_scr[bi]
    o_inter = [[None, None] for _ in range(GLA_HEADS)]
    for cc in ((0, 1) if d == 0 else (1, 0)):
        Sb = S.astype(BF16)
        for h in range(GLA_HEADS):
            o_inter[h][cc] = jnp.dot(qm[h][cc * C:(cc + 1) * C, :], Sb, preferred_element_type=F32)
        tot = first if cc == 0 else second
        dec = jnp.exp(bT[:, tot:tot + 1])
        kc = jnp.where((lane >= cc * C) & (lane < (cc + 1) * C), koutT, 0.0).astype(BF16)
        upd = [jnp.dot(kc[h * GLA_DK:(h + 1) * GLA_DK, :], vb[:, h * GLA_DV:(h + 1) * GLA_DV],
                       preferred_element_type=F32) for h in range(GLA_HEADS)]
        S = S * dec + jnp.concatenate(upd, axis=0)
        yield
    s_scr[bi] = S

    o = jnp.concatenate(
        [o_intra[h] + jnp.concatenate(o_inter[h], axis=0) for h in range(GLA_HEADS)], axis=1)
    if d == 0:
        of_scr[bi, blk] = o
    else:
        o = o + of_scr[bi, blk]
        gate = gate_ref[bi].astype(F32)
        nw = nw_ref[...]
        outs = []
        for h in range(GLA_HEADS):
            sl = slice(h * GLA_DV, (h + 1) * GLA_DV)
            outs.append(_rms(o[:, sl], nw) * _silu(gate[:, sl]))
        o_ref[bi] = jnp.concatenate(outs, axis=1).astype(o_ref.dtype)


def _gla_kernel(*refs, nblk, nb):
    s_scr = refs[-1]
    s = pl.program_id(1)

    @pl.when((s == 0) | (s == nblk))
    def _():
        s_scr[...] = jnp.zeros_like(s_scr)

    @pl.when(s < nblk)
    def _():
        _round_robin([_gla_block(bi, *refs, d=0, blk=s) for bi in range(nb)])

    @pl.when(s >= nblk)
    def _():
        _round_robin([_gla_block(bi, *refs, d=1, blk=2 * nblk - 1 - s) for bi in range(nb)])


def _scan_blk(s, nblk):
    return jnp.where(s < nblk, s, 2 * nblk - 1 - s)


def _scan_out_blk(s, nblk):
    return jnp.where(s < nblk, nblk - 1, 2 * nblk - 1 - s)


def _batch_slots(B, want):
    nb = want
    while B % nb:
        nb -= 1
    return nb


def _gla(proj, misc, wa_p, ba, nw):
    B, Lp, _ = proj.shape
    nblk = Lp // BLK
    nb = _batch_slots(B, 4)
    kd = GLA_HEADS * GLA_DK
    im = lambda col: (lambda b, s: (b, _scan_blk(s, nblk), col))
    return pl.pallas_call(
        functools.partial(_gla_kernel, nblk=nblk, nb=nb),
        grid=(B // nb, 2 * nblk),
        in_specs=[
            pl.BlockSpec((nb, BLK, kd), im(C_GQ // kd)),
            pl.BlockSpec((nb, BLK, kd), im(C_GK // kd)),
            pl.BlockSpec((nb, BLK, GLA_WIDTH), im(C_GV // GLA_WIDTH)),
            pl.BlockSpec((nb, BLK, GLA_WIDTH), im(C_GG // GLA_WIDTH)),
            pl.BlockSpec((nb, BLK, 128), im(0)),
            pl.BlockSpec((2, 128, kd), lambda b, s: (0, 0, 0)),
            pl.BlockSpec((2, 1, kd), lambda b, s: (0, 0, 0)),
            pl.BlockSpec((1, GLA_DV), lambda b, s: (0, 0)),
        ],
        out_specs=pl.BlockSpec((nb, BLK, GLA_WIDTH), lambda b, s: (b, _scan_out_blk(s, nblk), 0)),
        out_shape=jax.ShapeDtypeStruct((B, Lp, GLA_WIDTH), BF16),
        scratch_shapes=[pltpu.VMEM((nb, nblk, BLK, GLA_WIDTH), F32), pltpu.VMEM((nb, kd, GLA_DV), F32)],
        compiler_params=_params(("arbitrary", "arbitrary")),
        name="gla",
    )(proj, proj, proj, proj, misc, wa_p, ba, nw)


def _attn_kernel(q_ref, k_ref, v_ref, g_ref, band_ref, lam_ref, lc_ref, nw_ref, o_ref,
                 kT_scr, vb_scr, s_scr, e_scr, *, T, nk, RC, KC):
    i = pl.program_id(2)

    @pl.when(i == 0)
    def _():
        for j in range(nk):
            kT_scr[:, j * T:(j + 1) * T] = k_ref[j * T:(j + 1) * T, :].astype(F32).T.astype(BF16)
        vb_scr[...] = v_ref[...]

    def fold(x, op):
        r = x[:, 0:128]
        for t in range(1, KC // 128):
            r = op(r, x[:, t * 128:(t + 1) * 128])
        return r

    lane = _iota((1, 2 * DIFF_DQK), 1)
    q = q_ref[...].astype(F32) * (DIFF_DQK ** -0.5 * LOG2E)
    col0 = _iota((1, KC), 1)
    qcs = [jnp.where((lane >= c * DIFF_DQK) & (lane < (c + 1) * DIFF_DQK), q, 0.0).astype(BF16)
           for c in range(2)]
    groups = [(c, rc) for c in range(2) for rc in range(T // RC)]
    n_sub = nk * T // KC
    bidx = [jnp.clip(j - i, -2, 2) + 2 for j in range(nk)]

    def score_step(g, u, m_run):
        c, rc = g
        rows = slice(rc * RC, (rc + 1) * RC)
        cols = slice(u * KC, (u + 1) * KC)
        j, t = divmod(u, T // KC)
        s = jnp.dot(qcs[c][rows], kT_scr[:, cols], preferred_element_type=F32)
        s = s + band_ref[bidx[j], rows, t * KC:(t + 1) * KC]
        if u == 0:
            s = jnp.where(col0 >= PAD, s, -1e30)
        s_scr[c, rows, cols] = s
        mt = fold(s, jnp.maximum)
        return mt if m_run is None else jnp.maximum(m_run, mt)

    def prob_step(g, u, m, l_run):
        c, rc = g
        rows = slice(rc * RC, (rc + 1) * RC)
        cols = slice(u * KC, (u + 1) * KC)
        e = jnp.exp2(s_scr[c, rows, cols] - m)
        e_scr[c, rows, cols] = e.astype(BF16)
        lt = fold(e, jnp.add)
        return lt if l_run is None else l_run + lt

    def pv_step(g, j, pv):
        c, rc = g
        rows = slice(rc * RC, (rc + 1) * RC)
        cols = slice(j * T, (j + 1) * T)
        d = jnp.dot(e_scr[c, rows, cols], vb_scr[cols, :], preferred_element_type=F32)
        return d if pv is None else pv + d

    per_tile = T // KC
    m_run = None
    for u in range(n_sub):
        m_run = score_step(groups[0], u, m_run)
    parts = []
    for gi, g in enumerate(groups):
        m = jnp.max(m_run, axis=-1, keepdims=True)
        m_run = l_run = pv = None
        for u in range(n_sub):
            l_run = prob_step(g, u, m, l_run)
            if u % per_tile == per_tile - 1 and u >= 2 * per_tile - 1:
                pv = pv_step(g, u // per_tile - 1, pv)
            if gi + 1 < len(groups):
                m_run = score_step(groups[gi + 1], u, m_run)
        pv = pv_step(g, nk - 1, pv)
        parts.append(pv / jnp.sum(l_run, axis=-1, keepdims=True))
    per_half = len(groups) // 2
    outs = [jnp.concatenate(parts[:per_half], axis=0), jnp.concatenate(parts[per_half:], axis=0)]

    lp = lam_ref[...]
    lc = lc_ref[...]
    lam = (jnp.exp(jnp.sum(lp[0:1] * lp[1:2], axis=-1, keepdims=True))
           - jnp.exp(jnp.sum(lp[2:3] * lp[3:4], axis=-1, keepdims=True)) + lc[:, 0:1])
    o = outs[0] - lam * outs[1]
    y = _rms(o, nw_ref[...]) * lc[:, 1:2]
    o_ref[...] = (y * _silu(g_ref[...].astype(F32))).astype(o_ref.dtype)


def _attn(proj, band, diff_lambda, lcoef, nw, T):
    B, Lp, _ = proj.shape
    nk = Lp // T
    H = DIFF_HEADS
    RC = T // 2 if T > 128 else T
    KC = 128
    return pl.pallas_call(
        functools.partial(_attn_kernel, T=T, nk=nk, RC=RC, KC=KC),
        grid=(B, H, nk),
        in_specs=[
            pl.BlockSpec((None, T, 128), lambda b, h, i: (b, i, C_DQ // 128 + h)),
            pl.BlockSpec((None, Lp, 128), lambda b, h, i: (b, 0, C_DK // 128 + h)),
            pl.BlockSpec((None, Lp, 128), lambda b, h, i: (b, 0, C_DV // 128 + h)),
            pl.BlockSpec((None, T, 128), lambda b, h, i: (b, i, C_DG // 128 + h)),
            pl.BlockSpec((None, 5, T, T), lambda b, h, i: (h, 0, 0, 0)),
            pl.BlockSpec((4, DIFF_DQK), lambda b, h, i: (0, 0)),
            pl.BlockSpec((1, 128), lambda b, h, i: (0, 0)),
            pl.BlockSpec((1, DIFF_DV), lambda b, h, i: (0, 0)),
        ],
        out_specs=pl.BlockSpec((None, T, 128), lambda b, h, i: (b, i, h)),
        out_shape=jax.ShapeDtypeStruct((B, Lp, DIFF_WIDTH), BF16),
        scratch_shapes=[
            pltpu.VMEM((128, Lp), BF16),
            pltpu.VMEM((Lp, 128), BF16),
            pltpu.VMEM((2, T, Lp), F32),
            pltpu.VMEM((2, T, Lp), BF16),
        ],
        compiler_params=_params(("arbitrary", "arbitrary", "arbitrary")),
        name="diff_attn",
    )(proj, proj, proj, proj, band, diff_lambda, lcoef, nw)


def _bucket_table(T):
    nb = N_BUCKETS // 2
    max_exact = nb // 2
    rel = np.arange(6 * T) - (3 * T - 1)
    ret = np.where(rel > 0, nb, 0)
    n = np.abs(rel)
    nf = np.maximum(n, 1).astype(np.float64)
    large = max_exact + (np.log(nf / max_exact) / math.log(MAX_DISTANCE / max_exact)
                         * (nb - max_exact)).astype(np.int32)
    large = np.minimum(large, nb - 1)
    return (ret + np.where(n < max_exact, n, large)).astype(np.int32)


def _bias_band(rel_bias, T):
    n = 6 * T
    v = (rel_bias.astype(F32)[_bucket_table(T)] * LOG2E).T
    rows = jnp.tile(v, (1, T))[:, :T * (n - 1)].reshape(DIFF_HEADS, T, n - 1)
    band = rows[:, :, T - 1:n - 1].reshape(DIFF_HEADS, T, 5, T)
    return jnp.transpose(band, (0, 2, 1, 3))


def _conv_kernel(x_ref, w_ref, b_ref, o_ref):
    x = x_ref[...].astype(F32)
    w = w_ref[...]
    n = x.shape[0]
    half = (SSD_CONV - 1) // 2
    acc = x * w[half:half + 1, :] + b_ref[...]
    for kk in range(SSD_CONV):
        if kk != half:
            acc = acc + pltpu.roll(x, (half - kk) % n, 0) * w[kk:kk + 1, :]
    o_ref[...] = _silu(acc).astype(o_ref.dtype)


def _conv(proj, conv_w, conv_b):
    B, Lp, _ = proj.shape
    tn = 256
    return pl.pallas_call(
        _conv_kernel,
        grid=(B, SSD_CONV_DIM // tn),
        in_specs=[
            pl.BlockSpec((None, Lp, tn), lambda b, j: (b, 0, C_XBC // tn + j)),
            pl.BlockSpec((SSD_CONV, tn), lambda b, j: (0, j)),
            pl.BlockSpec((1, tn), lambda b, j: (0, j)),
        ],
        out_specs=pl.BlockSpec((None, Lp, tn), lambda b, j: (b, 0, j)),
        out_shape=jax.ShapeDtypeStruct((B, Lp, SSD_CONV_DIM), BF16),
        compiler_params=_params(("arbitrary", "arbitrary")),
        name="ssd_conv",
    )(proj, conv_w, conv_b.reshape(1, SSD_CONV_DIM))


def _ssd_block(bi, xc_ref, z_ref, misc_ref, dtb_r_ref, dtb_c_ref, al_r_ref, al_c_ref, e_ref, dskip_ref,
               nw_ref, o_ref, yf_scr, s_scr, *, d, blk):
    G, N, P = SSD_GROUPS, SSD_STATE, SSD_HEADDIM
    R = SSD_HEADS // G
    xs = xc_ref[bi, :, 0:SSD_WIDTH].astype(F32)
    Bm = xc_ref[bi, :, SSD_WIDTH:SSD_WIDTH + G * N].astype(F32)
    Cb = xc_ref[bi, :, SSD_WIDTH + G * N:]
    misc = misc_ref[bi]
    miscT = misc.T
    row = _iota((BLK, 1), 0)
    lane = _iota((1, BLK), 1)
    dt = jnp.where((blk * BLK + row) >= PAD, _softplus(misc + dtb_r_ref[...]), 0.0)
    a = dt * (-jnp.exp(al_r_ref[...]))
    dtT = jnp.where((blk * BLK + lane) >= PAD, _softplus(miscT + dtb_c_ref[...]), 0.0)
    aT = dtT * (-jnp.exp(al_c_ref[...]))

    r = _iota((BLK, BLK), 0)
    c = _iota((BLK, BLK), 1)
    tri = (c <= r) if d == 0 else (c >= r)
    triT = (r <= c) if d == 0 else (r >= c)
    cum = _dot_r3(jnp.where(tri, 1.0, 0.0).astype(BF16), a)
    cumT = _dot_l3(aT, jnp.where(triT, 1.0, 0.0).astype(BF16))
    yield

    lo = DT_LANE0 + SSD_HEADS * d
    hm = (lane >= lo) & (lane < lo + SSD_HEADS)
    last = BLK - 1 if d == 0 else 0
    tot = cum[last:last + 1, :]
    ecum = jnp.where(hm, jnp.exp(cum), 0.0)
    ecum_hi = ecum.astype(BF16)
    stack = jnp.concatenate([
        jnp.where(hm, dt, 0.0).astype(BF16),
        jnp.where(hm, jnp.exp(tot - cum), 0.0).astype(BF16),
        ecum_hi,
        (ecum - ecum_hi.astype(F32)).astype(BF16)], axis=0)
    ex = jnp.dot(stack, e_ref[d], preferred_element_type=F32)
    dt_x = ex[0:BLK]
    toend_x = ex[BLK:2 * BLK]
    ecum_x = ex[2 * BLK:3 * BLK] + ex[3 * BLK:4 * BLK]
    etot_x = ecum_x[last:last + 1, :]
    xd = xs * dt_x
    xdb = xd.astype(BF16)
    xdw = (xd * toend_x).astype(BF16)
    yield

    y_parts = []
    for g in range(G):
        BgT = Bm[:, g * N:(g + 1) * N].T.astype(BF16)
        Cg = Cb[:, g * N:(g + 1) * N]
        cols = slice(g * R * P, (g + 1) * R * P)
        CB = jnp.dot(Cg, BgT, preferred_element_type=F32)
        S = s_scr[bi, g]
        y_off = jnp.dot(Cg, S.astype(BF16), preferred_element_type=F32) * ecum_x[:, cols]
        s_scr[bi, g] = S * etot_x[:, cols] + jnp.dot(BgT, xdw[:, cols], preferred_element_type=F32)
        yield
        for pr in range(R // 2):
            h0 = g * R + 2 * pr
            xpair = xdb[:, h0 * P:(h0 + 2) * P]
            acc = y_off[:, 2 * pr * P:(2 * pr + 2) * P]
            for hh in range(2):
                li = lo + h0 + hh
                seg = cum[:, li:li + 1] - cumT[li:li + 1, :]
                dec = jnp.exp(jnp.where(tri, seg, -jnp.inf))
                sc = (CB * dec).astype(BF16)
                xm = jnp.where((lane >= hh * P) & (lane < (hh + 1) * P), xpair, 0.0)
                acc = acc + jnp.dot(sc, xm, preferred_element_type=F32)
            y_parts.append(acc)
            yield
    y = jnp.concatenate(y_parts, axis=1)

    if d == 0:
        yf_scr[bi, blk] = y
    else:
        y = (y + yf_scr[bi, blk] + xs * dskip_ref[...]) * _silu(z_ref[bi].astype(F32))
        nw = nw_ref[...]
        W = SSD_WIDTH // G
        outs = [_rms(y[:, g * W:(g + 1) * W], nw[:, g * W:(g + 1) * W]) for g in range(G)]
        o_ref[bi] = jnp.concatenate(outs, axis=1).astype(o_ref.dtype)


def _ssd_kernel(*refs, nblk, nb):
    s_scr = refs[-1]
    s = pl.program_id(1)

    @pl.when((s == 0) | (s == nblk))
    def _():
        s_scr[...] = jnp.zeros_like(s_scr)

    @pl.when(s < nblk)
    def _():
        _round_robin([_ssd_block(bi, *refs, d=0, blk=s) for bi in range(nb)])

    @pl.when(s >= nblk)
    def _():
        _round_robin([_ssd_block(bi, *refs, d=1, blk=2 * nblk - 1 - s) for bi in range(nb)])


def _ssd(proj, misc, xc, dtb_r, dtb_c, al_r, al_c, E, dskip, nw):
    B, Lp, _ = proj.shape
    nblk = Lp // BLK
    nb = _batch_slots(B, 2)
    const2 = lambda b, s: (0, 0)
    return pl.pallas_call(
        functools.partial(_ssd_kernel, nblk=nblk, nb=nb),
        grid=(B // nb, 2 * nblk),
        in_specs=[
            pl.BlockSpec((nb, BLK, SSD_CONV_DIM), lambda b, s: (b, _scan_blk(s, nblk), 0)),
            pl.BlockSpec((nb, BLK, SSD_WIDTH), lambda b, s: (b, _scan_blk(s, nblk), C_Z // SSD_WIDTH)),
            pl.BlockSpec((nb, BLK, 128), lambda b, s: (b, _scan_blk(s, nblk), 0)),
            pl.BlockSpec((1, 128), const2),
            pl.BlockSpec((128, 128), const2),
            pl.BlockSpec((1, 128), const2),
            pl.BlockSpec((128, 128), const2),
            pl.BlockSpec((2, 128, SSD_WIDTH), lambda b, s: (0, 0, 0)),
            pl.BlockSpec((1, SSD_WIDTH), const2),
            pl.BlockSpec((1, SSD_WIDTH), const2),
        ],
        out_specs=pl.BlockSpec((nb, BLK, SSD_WIDTH), lambda b, s: (b, _scan_out_blk(s, nblk), 0)),
        out_shape=jax.ShapeDtypeStruct((B, Lp, SSD_WIDTH), BF16),
        scratch_shapes=[
            pltpu.VMEM((nb, nblk, BLK, SSD_WIDTH), F32),
            pltpu.VMEM((nb, SSD_GROUPS, SSD_STATE, SSD_WIDTH // SSD_GROUPS), F32),
        ],
        compiler_params=_params(("arbitrary", "arbitrary")),
        name="ssd_scan",
    )(xc, proj, misc, dtb_r, dtb_c, al_r, al_c, E, dskip, nw)


def _permute_w_in(w):
    o = np.cumsum([0, 256, 256, 512, 512, 32, 512, 512, 512, 512, 1024, 1536, 32])
    gq, gk, gv, gg, gcode, dq, dk, dv, dg, z, xbc, dt, end = [int(v) for v in o]
    pieces = [w[..., gq:gcode], w[..., dq:dg], w[..., z:xbc], w[..., dg:z], w[..., xbc:dt],
              w[..., gcode:dq], w[..., dt:end],
              jnp.zeros(w.shape[:-1] + (N_PROJ - C_MISC - 64,), w.dtype)]
    return jnp.concatenate(pieces, axis=-1)


def _expansion():
    E = np.zeros((2, 128, SSD_WIDTH), np.float32)
    for d in range(2):
        for h in range(SSD_HEADS):
            E[d, DT_LANE0 + SSD_HEADS * d + h, h * SSD_HEADDIM:(h + 1) * SSD_HEADDIM] = 1.0
    return jnp.asarray(E, BF16)


def _misc_row(p):
    flat = p.reshape(p.shape[0], 1, 2 * SSD_HEADS).astype(F32)
    return jnp.pad(flat, ((0, 0), (0, 0), (DT_LANE0, 128 - DT_LANE0 - 2 * SSD_HEADS)))


def kernel(x, meta_tokens, rel_bias, final_norm_w, norm_w, w_in, w_out, gla_wa2, gla_ba, gla_norm_w,
           diff_lambda, diff_norm_w, conv_w, conv_b, ssd_A_log, ssd_dt_bias, ssd_D, ssd_norm_w):
    B, S, D = x.shape
    assert D == D_MODEL and S % BLK == 0
    Lp = TOK0 + S
    T = 384 if Lp % 384 == 0 else 128

    h = jnp.concatenate([
        jnp.zeros((B, PAD, D), x.dtype),
        jnp.broadcast_to(meta_tokens[None].astype(x.dtype), (B, N_META, D)),
        x], axis=1)

    band = _bias_band(rel_bias, T)
    E = _expansion()
    lam_init = np.array([0.8 - 0.6 * math.exp(-0.3 * l) for l in range(DEPTH)], np.float32)
    lcoef = np.zeros((DEPTH, 1, 128), np.float32)
    lcoef[:, 0, 0] = lam_init
    lcoef[:, 0, 1] = 1.0 - lam_init

    kd = GLA_HEADS * GLA_DK
    wa_p = jnp.zeros((DEPTH, 2, 128, kd), F32)
    wa_p = wa_p.at[:, 0, 0:GLA_RANK].set(gla_wa2[:, 0].astype(F32))
    wa_p = wa_p.at[:, 1, GLA_RANK:2 * GLA_RANK].set(gla_wa2[:, 1].astype(F32))
    dtb_r = _misc_row(ssd_dt_bias)
    al_r = _misc_row(ssd_A_log)
    layers = dict(
        norm_w=norm_w,
        w_in=_permute_w_in(w_in).astype(BF16),
        w_out=w_out.astype(BF16),
        wa_p=wa_p,
        ba=gla_ba.reshape(DEPTH, 2, 1, kd).astype(F32),
        gla_nw=gla_norm_w.reshape(DEPTH, 1, GLA_DV).astype(F32),
        lam=diff_lambda.astype(F32),
        lcoef=jnp.asarray(lcoef),
        diff_nw=diff_norm_w.reshape(DEPTH, 1, DIFF_DV).astype(F32),
        conv_w=conv_w.astype(F32),
        conv_b=conv_b.astype(F32),
        dtb_r=dtb_r,
        dtb_c=jnp.broadcast_to(jnp.swapaxes(dtb_r, 1, 2), (DEPTH, 128, 128)),
        al_r=al_r,
        al_c=jnp.broadcast_to(jnp.swapaxes(al_r, 1, 2), (DEPTH, 128, 128)),
        dskip=jnp.repeat(ssd_D.astype(F32), SSD_HEADDIM, axis=-1).reshape(DEPTH, 1, SSD_WIDTH),
        ssd_nw=ssd_norm_w.reshape(DEPTH, 1, SSD_WIDTH).astype(F32),
    )

    def layer(h, p):
        proj, misc = _inproj(h, p["norm_w"], p["w_in"])
        o_gla = _gla(proj, misc, p["wa_p"], p["ba"], p["gla_nw"])
        o_diff = _attn(proj, band, p["lam"], p["lcoef"], p["diff_nw"], T)
        xc = _conv(proj, p["conv_w"], p["conv_b"])
        o_ssd = _ssd(proj, misc, xc, p["dtb_r"], p["dtb_c"], p["al_r"], p["al_c"], E, p["dskip"], p["ssd_nw"])
        return _outproj(h, o_gla, o_diff, o_ssd, p["w_out"]), None

    h, _ = lax.scan(layer, h, layers)
    return _final_norm(h, final_norm_w)
```

```python
import functools
import math

import numpy as np
import jax
import jax.numpy as jnp
from jax import lax
from jax.experimental import pallas as pl
from jax.experimental.pallas import tpu as pltpu

F32 = jnp.float32
BF16 = jnp.bfloat16

D_MODEL = 2048
DEPTH = 4
N_META = 16
EPS = 1e-6
GLA_HEADS = 4
GLA_DK = 64
GLA_DV = 128
GLA_WIDTH = 512
GLA_RANK = 16
GLA_TAU = 16.0
GLA_CHUNK = 64
DIFF_HEADS = 4
DIFF_DQK = 64
DIFF_DV = 128
DIFF_WIDTH = 512
N_BUCKETS = 32
MAX_DISTANCE = 128
SSD_WIDTH = 1024
SSD_HEADDIM = 64
SSD_HEADS = 16
SSD_GROUPS = 2
SSD_STATE = 128
SSD_CONV = 5
SSD_CHUNK = 128
SSD_CONV_DIM = SSD_WIDTH + 2 * SSD_GROUPS * SSD_STATE

BLK = 128
TOK0 = BLK
PAD = TOK0 - N_META

C_GQ, C_GK, C_GV, C_GG = 0, 256, 512, 1024
C_DQ, C_DK, C_DV = 1536, 2048, 2560
C_Z, C_DG, C_XBC, C_MISC = 3072, 4096, 4608, 6144
N_PROJ = 6272
DT_LANE0 = 2 * GLA_RANK

VMEM_LIMIT = 56 * 1024 * 1024
LOG2E = 1.4426950408889634


def _params(sem, limit=VMEM_LIMIT):
    return pltpu.CompilerParams(dimension_semantics=sem, vmem_limit_bytes=limit)


def _silu(x):
    return x / (1.0 + jnp.exp(-x))


def _softplus(x):
    return jnp.maximum(x, 0.0) + jnp.log1p(jnp.exp(-jnp.abs(x)))


def _log_sigmoid(x):
    return jnp.minimum(x, 0.0) - jnp.log1p(jnp.exp(-jnp.abs(x)))


def _bdot(a, b):
    return jnp.dot(a.astype(BF16), b.astype(BF16), preferred_element_type=F32)


def _split3(x):
    hi = x.astype(BF16)
    r = x - hi.astype(F32)
    mid = r.astype(BF16)
    lo = (r - mid.astype(F32)).astype(BF16)
    return hi, mid, lo


def _dot_l3(a, b_exact):
    hi, mid, lo = _split3(a)
    d = functools.partial(jnp.dot, preferred_element_type=F32)
    return d(hi, b_exact) + d(mid, b_exact) + d(lo, b_exact)


def _dot_r3(a_exact, b):
    hi, mid, lo = _split3(b)
    d = functools.partial(jnp.dot, preferred_element_type=F32)
    return d(a_exact, hi) + d(a_exact, mid) + d(a_exact, lo)


def _dot_22(a, b):
    ah = a.astype(BF16)
    al = (a - ah.astype(F32)).astype(BF16)
    bh = b.astype(BF16)
    bl = (b - bh.astype(F32)).astype(BF16)
    d = functools.partial(jnp.dot, preferred_element_type=F32)
    return d(ah, bh) + d(ah, bl) + d(al, bh)


def _iota(shape, dim):
    return lax.broadcasted_iota(jnp.int32, shape, dim)


def _rms(x, w):
    return x * lax.rsqrt(jnp.mean(x * x, axis=-1, keepdims=True) + EPS) * w


def _inproj_kernel(h_ref, nw_ref, w_ref, wm_ref, o_ref, om_ref, u_scr, *, tm):
    @pl.when(pl.program_id(2) == 0)
    def _():
        x = h_ref[...]
        y = _rms(x, nw_ref[...])
        row = pl.program_id(1) * tm + _iota((tm, 1), 0)
        u_scr[...] = jnp.where(row >= PAD, y, 0.0).astype(BF16)
        om_ref[...] = jnp.dot(u_scr[...], wm_ref[...], preferred_element_type=F32)

    o_ref[...] = jnp.dot(u_scr[...], w_ref[...], preferred_element_type=F32).astype(o_ref.dtype)


def _inproj(h, norm_w, w_p):
    B, Lp, D = h.shape
    tm = Lp // 4
    tn = C_MISC // 4
    return pl.pallas_call(
        functools.partial(_inproj_kernel, tm=tm),
        grid=(B, Lp // tm, C_MISC // tn),
        in_specs=[
            pl.BlockSpec((None, tm, D), lambda b, i, j: (b, i, 0)),
            pl.BlockSpec((1, D), lambda b, i, j: (0, 0)),
            pl.BlockSpec((D, tn), lambda b, i, j: (0, j)),
            pl.BlockSpec((D, 128), lambda b, i, j: (0, C_MISC // 128)),
        ],
        out_specs=[
            pl.BlockSpec((None, tm, tn), lambda b, i, j: (b, i, j)),
            pl.BlockSpec((None, tm, 128), lambda b, i, j: (b, i, 0)),
        ],
        out_shape=[jax.ShapeDtypeStruct((B, Lp, C_MISC), BF16),
                   jax.ShapeDtypeStruct((B, Lp, 128), F32)],
        scratch_shapes=[pltpu.VMEM((tm, D), BF16)],
        compiler_params=_params(("arbitrary", "arbitrary", "arbitrary")),
        name="inproj",
    )(h, norm_w.reshape(1, D), w_p, w_p)


def _outproj_kernel(h_ref, a_ref, b_ref, c_ref, w_ref, o_ref):
    d = functools.partial(jnp.dot, preferred_element_type=F32)
    acc = d(a_ref[...], w_ref[0:GLA_WIDTH, :])
    acc += d(b_ref[...], w_ref[GLA_WIDTH:GLA_WIDTH + DIFF_WIDTH, :])
    acc += d(c_ref[...], w_ref[GLA_WIDTH + DIFF_WIDTH:, :])
    o_ref[...] = h_ref[...] + acc


def _outproj(h, o_gla, o_diff, o_ssd, w_out):
    B, Lp, D = h.shape
    tm = Lp // 4
    tn = D // 2
    return pl.pallas_call(
        _outproj_kernel,
        grid=(B, Lp // tm, D // tn),
        in_specs=[
            pl.BlockSpec((None, tm, tn), lambda b, i, j: (b, i, j)),
            pl.BlockSpec((None, tm, GLA_WIDTH), lambda b, i, j: (b, i, 0)),
            pl.BlockSpec((None, tm, DIFF_WIDTH), lambda b, i, j: (b, i, 0)),
            pl.BlockSpec((None, tm, SSD_WIDTH), lambda b, i, j: (b, i, 0)),
            pl.BlockSpec((D, tn), lambda b, i, j: (0, j)),
        ],
        out_specs=pl.BlockSpec((None, tm, tn), lambda b, i, j: (b, i, j)),
        out_shape=jax.ShapeDtypeStruct((B, Lp, D), F32),
        input_output_aliases={0: 0},
        compiler_params=_params(("arbitrary", "arbitrary", "arbitrary")),
        name="outproj",
    )(h, o_gla, o_diff, o_ssd, w_out)


def _final_kernel(h_ref, w_ref, o_ref):
    o_ref[...] = _rms(h_ref[...], w_ref[...])


def _final_norm(h, w):
    B, Lp, D = h.shape
    S = Lp - TOK0
    rb = 512 if S % 512 == 0 else BLK
    return pl.pallas_call(
        _final_kernel,
        grid=(B, S // rb),
        in_specs=[
            pl.BlockSpec((pl.Element(rb), pl.Element(D)),
                         lambda b, i: (pl.multiple_of(b * Lp + TOK0 + i * rb, BLK), 0)),
            pl.BlockSpec((1, D), lambda b, i: (0, 0)),
        ],
        out_specs=pl.BlockSpec((None, rb, D), lambda b, i: (b, i, 0)),
        out_shape=jax.ShapeDtypeStruct((B, S, D), F32),
        compiler_params=_params(("arbitrary", "arbitrary")),
        name="final_norm",
    )(h.reshape(B * Lp, D), w.reshape(1, D))


def _round_robin(gens):
    alive = list(gens)
    while alive:
        for g in list(alive):
            try:
                next(g)
            except StopIteration:
                alive.remove(g)


def _gla_block(bi, q_ref, k_ref, v_ref, gate_ref, misc_ref, wa_ref, ba_ref, nw_ref, o_ref,
               of_scr, s_scr, *, d, blk):
    C = GLA_CHUNK
    row = _iota((BLK, 1), 0)
    valid = (blk * BLK + row) >= PAD
    x = _dot_22(misc_ref[bi], wa_ref[d]) + ba_ref[d]
    yield
    g = jnp.where(valid, _log_sigmoid(x) * (1.0 / GLA_TAU), 0.0)

    r = _iota((BLK, BLK), 0)
    c = _iota((BLK, BLK), 1)
    same = (r >= C) == (c >= C)
    tri = (same & (c <= r)) if d == 0 else (same & (c >= r))
    b = _dot_r3(jnp.where(tri, 1.0, 0.0).astype(BF16), g)
    yield
    first, second = (C - 1, 2 * C - 1) if d == 0 else (0, C)

    k = k_ref[bi].astype(F32)
    q_in = q_ref[bi].astype(F32) * (GLA_DK ** -0.5) * jnp.exp(b)
    kT = k.T
    bT = b.T
    lane = _iota((1, BLK), 1)
    blastT = jnp.where(lane < C, bT[:, first:first + 1], bT[:, second:second + 1])
    kinT = (kT * jnp.exp(-bT)).astype(BF16)
    koutT = kT * jnp.exp(blastT - bT)
    vb = v_ref[bi]
    yield

    lane_k = _iota((1, GLA_HEADS * GLA_DK), 1)
    qm = [jnp.where((lane_k >= h * GLA_DK) & (lane_k < (h + 1) * GLA_DK), q_in, 0.0).astype(BF16)
          for h in range(GLA_HEADS)]
    att = [jnp.where(tri, jnp.dot(qm[h], kinT, preferred_element_type=F32), 0.0).astype(BF16)
           for h in range(GLA_HEADS)]
    yield
    o_intra = [jnp.dot(att[h], vb[:, h * GLA_DV:(h + 1) * GLA_DV], preferred_element_type=F32)
               for h in range(GLA_HEADS)]
    yield

    S = s_scr[bi]
    o_inter = [[None, None] for _ in range(GLA_HEADS)]
    for cc in ((0, 1) if d == 0 else (1, 0)):
        Sb = S.astype(BF16)
        for h in range(GLA_HEADS):
            o_inter[h][cc] = jnp.dot(qm[h][cc * C:(cc + 1) * C, :], Sb, preferred_element_type=F32)
        tot = first if cc == 0 else second
        dec = jnp.exp(bT[:, tot:tot + 1])
        kc = jnp.where((lane >= cc * C) & (lane < (cc + 1) * C), koutT, 0.0).astype(BF16)
        upd = [jnp.dot(kc[h * GLA_DK:(h + 1) * GLA_DK, :], vb[:, h * GLA_DV:(h + 1) * GLA_DV],
                       preferred_element_type=F32) for h in range(GLA_HEADS)]
        S = S * dec + jnp.concatenate(upd, axis=0)
        yield
    s_scr[bi] = S

    o = jnp.concatenate(
        [o_intra[h] + jnp.concatenate(o_inter[h], axis=0) for h in range(GLA_HEADS)], axis=1)
    if d == 0:
        of_scr[bi, blk] = o
    else:
        o = o + of_scr[bi, blk]
        gate = gate_ref[bi].astype(F32)
        nw = nw_ref[...]
        outs = []
        for h in range(GLA_HEADS):
            sl = slice(h * GLA_DV, (h + 1) * GLA_DV)
            outs.append(_rms(o[:, sl], nw) * _silu(gate[:, sl]))
        o_ref[bi] = jnp.concatenate(outs, axis=1).astype(o_ref.dtype)


def _gla_kernel(*refs, nblk, nb):
    s_scr = refs[-1]
    s = pl.program_id(1)

    @pl.when((s == 0) | (s == nblk))
    def _():
        s_scr[...] = jnp.zeros_like(s_scr)

    @pl.when(s < nblk)
    def _():
        _round_robin([_gla_block(bi, *refs, d=0, blk=s) for bi in range(nb)])

    @pl.when(s >= nblk)
    def _():
        _round_robin([_gla_block(bi, *refs, d=1, blk=2 * nblk - 1 - s) for bi in range(nb)])


def _scan_blk(s, nblk):
    return jnp.where(s < nblk, s, 2 * nblk - 1 - s)


def _scan_out_blk(s, nblk):
    return jnp.where(s < nblk, nblk - 1, 2 * nblk - 1 - s)


def _batch_slots(B, want):
    nb = want
    while B % nb:
        nb -= 1
    return nb


def _gla(proj, misc, wa_p, ba, nw):
    B, Lp, _ = proj.shape
    nblk = Lp // BLK
    nb = _batch_slots(B, 4)
    kd = GLA_HEADS * GLA_DK
    im = lambda col: (lambda b, s: (b, _scan_blk(s, nblk), col))
    return pl.pallas_call(
        functools.partial(_gla_kernel, nblk=nblk, nb=nb),
        grid=(B // nb, 2 * nblk),
        in_specs=[
            pl.BlockSpec((nb, BLK, kd), im(C_GQ // kd)),
            pl.BlockSpec((nb, BLK, kd), im(C_GK // kd)),
            pl.BlockSpec((nb, BLK, GLA_WIDTH), im(C_GV // GLA_WIDTH)),
            pl.BlockSpec((nb, BLK, GLA_WIDTH), im(C_GG // GLA_WIDTH)),
            pl.BlockSpec((nb, BLK, 128), im(0)),
            pl.BlockSpec((2, 128, kd), lambda b, s: (0, 0, 0)),
            pl.BlockSpec((2, 1, kd), lambda b, s: (0, 0, 0)),
            pl.BlockSpec((1, GLA_DV), lambda b, s: (0, 0)),
        ],
        out_specs=pl.BlockSpec((nb, BLK, GLA_WIDTH), lambda b, s: (b, _scan_out_blk(s, nblk), 0)),
        out_shape=jax.ShapeDtypeStruct((B, Lp, GLA_WIDTH), BF16),
        scratch_shapes=[pltpu.VMEM((nb, nblk, BLK, GLA_WIDTH), F32), pltpu.VMEM((nb, kd, GLA_DV), F32)],
        compiler_params=_params(("arbitrary", "arbitrary")),
        name="gla",
    )(proj, proj, proj, proj, misc, wa_p, ba, nw)


def _attn_slot(bi, i, lam, lc, q_ref, k_ref, v_ref, g_ref, band_ref, nw_ref, o_ref,
               kT_scr, vb_scr, s_scr, e_scr, *, T, nk, RC, KC):
    def fold(x, op):
        r = x[:, 0:128]
        for t in range(1, KC // 128):
            r = op(r, x[:, t * 128:(t + 1) * 128])
        return r

    lane = _iota((1, 2 * DIFF_DQK), 1)
    q = q_ref[bi].astype(F32) * (DIFF_DQK ** -0.5 * LOG2E)
    col0 = _iota((1, KC), 1)
    qcs = [jnp.where((lane >= c * DIFF_DQK) & (lane < (c + 1) * DIFF_DQK), q, 0.0).astype(BF16)
           for c in range(2)]
    groups = [(c, rc) for c in range(2) for rc in range(T // RC)]
    n_sub = nk * T // KC
    bidx = [jnp.clip(j - i, -2, 2) + 2 for j in range(nk)]

    def score_step(gi, u, m_run):
        c, rc = groups[gi]
        rows = slice(rc * RC, (rc + 1) * RC)
        cols = slice(u * KC, (u + 1) * KC)
        j, t = divmod(u, T // KC)
        s = jnp.dot(qcs[c][rows], kT_scr[bi, :, cols], preferred_element_type=F32)
        s = s + band_ref[bidx[j], rows, t * KC:(t + 1) * KC]
        if u == 0:
            s = jnp.where(col0 >= PAD, s, -1e30)
        s_scr[bi, gi % 2, :, cols] = s
        mt = fold(s, jnp.maximum)
        return mt if m_run is None else jnp.maximum(m_run, mt)

    def prob_step(gi, u, m, l_run):
        cols = slice(u * KC, (u + 1) * KC)
        e = jnp.exp2(s_scr[bi, gi % 2, :, cols] - m)
        e_scr[bi, :, cols] = e.astype(BF16)
        lt = fold(e, jnp.add)
        return lt if l_run is None else l_run + lt

    def pv_step(j, pv):
        cols = slice(j * T, (j + 1) * T)
        d = jnp.dot(e_scr[bi, :, cols], vb_scr[bi, cols, :], preferred_element_type=F32)
        return d if pv is None else pv + d

    per_tile = T // KC
    m_run = None
    for u in range(n_sub):
        m_run = score_step(0, u, m_run)
        yield
    parts = []
    for gi in range(len(groups)):
        m = jnp.max(m_run, axis=-1, keepdims=True)
        m_run = l_run = pv = None
        for u in range(n_sub):
            l_run = prob_step(gi, u, m, l_run)
            if u % per_tile == per_tile - 1 and u >= 2 * per_tile - 1:
                pv = pv_step(u // per_tile - 1, pv)
            if gi + 1 < len(groups):
                m_run = score_step(gi + 1, u, m_run)
            yield
        pv = pv_step(nk - 1, pv)
        parts.append(pv / jnp.sum(l_run, axis=-1, keepdims=True))
    per_half = len(groups) // 2
    outs = [jnp.concatenate(parts[:per_half], axis=0), jnp.concatenate(parts[per_half:], axis=0)]

    o = outs[0] - lam * outs[1]
    y = _rms(o, nw_ref[...]) * lc[:, 1:2]
    o_ref[bi] = (y * _silu(g_ref[bi].astype(F32))).astype(o_ref.dtype)


def _attn_kernel(q_ref, k_ref, v_ref, g_ref, band_ref, lam_ref, lc_ref, nw_ref, o_ref,
                 kT_scr, vb_scr, s_scr, e_scr, *, nb, T, nk, RC, KC):
    i = pl.program_id(2)

    @pl.when(i == 0)
    def _():
        for bi in range(nb):
            for j in range(nk):
                kT_scr[bi, :, j * T:(j + 1) * T] = k_ref[bi, j * T:(j + 1) * T, :].astype(F32).T.astype(BF16)
        vb_scr[...] = v_ref[...]

    lp = lam_ref[...]
    lc = lc_ref[...]
    lam = (jnp.exp(jnp.sum(lp[0:1] * lp[1:2], axis=-1, keepdims=True))
           - jnp.exp(jnp.sum(lp[2:3] * lp[3:4], axis=-1, keepdims=True)) + lc[:, 0:1])
    _round_robin([
        _attn_slot(bi, i, lam, lc, q_ref, k_ref, v_ref, g_ref, band_ref, nw_ref, o_ref,
                   kT_scr, vb_scr, s_scr, e_scr, T=T, nk=nk, RC=RC, KC=KC)
        for bi in range(nb)])


def _attn(proj, band, diff_lambda, lcoef, nw, T):
    B, Lp, _ = proj.shape
    nk = Lp // T
    H = DIFF_HEADS
    RC = 128
    KC = 128
    nb = _batch_slots(B, 2)
    return pl.pallas_call(
        functools.partial(_attn_kernel, nb=nb, T=T, nk=nk, RC=RC, KC=KC),
        grid=(B // nb, H, nk),
        in_specs=[
            pl.BlockSpec((nb, T, 128), lambda b, h, i: (b, i, C_DQ // 128 + h)),
            pl.BlockSpec((nb, Lp, 128), lambda b, h, i: (b, 0, C_DK // 128 + h)),
            pl.BlockSpec((nb, Lp, 128), lambda b, h, i: (b, 0, C_DV // 128 + h)),
            pl.BlockSpec((nb, T, 128), lambda b, h, i: (b, i, C_DG // 128 + h)),
            pl.BlockSpec((None, 5, T, T), lambda b, h, i: (h, 0, 0, 0)),
            pl.BlockSpec((4, DIFF_DQK), lambda b, h, i: (0, 0)),
            pl.BlockSpec((1, 128), lambda b, h, i: (0, 0)),
            pl.BlockSpec((1, DIFF_DV), lambda b, h, i: (0, 0)),
        ],
        out_specs=pl.BlockSpec((nb, T, 128), lambda b, h, i: (b, i, h)),
        out_shape=jax.ShapeDtypeStruct((B, Lp, DIFF_WIDTH), BF16),
        scratch_shapes=[
            pltpu.VMEM((nb, 128, Lp), BF16),
            pltpu.VMEM((nb, Lp, 128), BF16),
            pltpu.VMEM((nb, 2, RC, Lp), F32),
            pltpu.VMEM((nb, RC, Lp), BF16),
        ],
        compiler_params=_params(("arbitrary", "arbitrary", "arbitrary")),
        name="diff_attn",
    )(proj, proj, proj, proj, band, diff_lambda, lcoef, nw)


def _bucket_table(T):
    nb = N_BUCKETS // 2
    max_exact = nb // 2
    rel = np.arange(6 * T) - (3 * T - 1)
    ret = np.where(rel > 0, nb, 0)
    n = np.abs(rel)
    nf = np.maximum(n, 1).astype(np.float64)
    large = max_exact + (np.log(nf / max_exact) / math.log(MAX_DISTANCE / max_exact)
                         * (nb - max_exact)).astype(np.int32)
    large = np.minimum(large, nb - 1)
    return (ret + np.where(n < max_exact, n, large)).astype(np.int32)


def _bias_band(rel_bias, T):
    n = 6 * T
    v = (rel_bias.astype(F32)[_bucket_table(T)] * LOG2E).T
    rows = jnp.tile(v, (1, T))[:, :T * (n - 1)].reshape(DIFF_HEADS, T, n - 1)
    band = rows[:, :, T - 1:n - 1].reshape(DIFF_HEADS, T, 5, T)
    return jnp.transpose(band, (0, 2, 1, 3))


def _conv_kernel(x_ref, w_ref, b_ref, o_ref):
    x = x_ref[...].astype(F32)
    w = w_ref[...]
    n = x.shape[0]
    half = (SSD_CONV - 1) // 2
    acc = x * w[half:half + 1, :] + b_ref[...]
    for kk in range(SSD_CONV):
        if kk != half:
            acc = acc + pltpu.roll(x, (half - kk) % n, 0) * w[kk:kk + 1, :]
    o_ref[...] = _silu(acc).astype(o_ref.dtype)


def _conv(proj, conv_w, conv_b):
    B, Lp, _ = proj.shape
    tn = 256
    return pl.pallas_call(
        _conv_kernel,
        grid=(B, SSD_CONV_DIM // tn),
        in_specs=[
            pl.BlockSpec((None, Lp, tn), lambda b, j: (b, 0, C_XBC // tn + j)),
            pl.BlockSpec((SSD_CONV, tn), lambda b, j: (0, j)),
            pl.BlockSpec((1, tn), lambda b, j: (0, j)),
        ],
        out_specs=pl.BlockSpec((None, Lp, tn), lambda b, j: (b, 0, j)),
        out_shape=jax.ShapeDtypeStruct((B, Lp, SSD_CONV_DIM), BF16),
        compiler_params=_params(("arbitrary", "arbitrary")),
        name="ssd_conv",
    )(proj, conv_w, conv_b.reshape(1, SSD_CONV_DIM))


def _ssd_block(bi, xc_ref, z_ref, misc_ref, dtb_r_ref, dtb_c_ref, al_r_ref, al_c_ref, e_ref, dskip_ref,
               nw_ref, o_ref, yf_scr, s_scr, *, d, blk):
    G, N, P = SSD_GROUPS, SSD_STATE, SSD_HEADDIM
    R = SSD_HEADS // G
    xs = xc_ref[bi, :, 0:SSD_WIDTH].astype(F32)
    Bm = xc_ref[bi, :, SSD_WIDTH:SSD_WIDTH + G * N].astype(F32)
    Cb = xc_ref[bi, :, SSD_WIDTH + G * N:]
    misc = misc_ref[bi]
    miscT = misc.T
    row = _iota((BLK, 1), 0)
    lane = _iota((1, BLK), 1)
    dt = jnp.where((blk * BLK + row) >= PAD, _softplus(misc + dtb_r_ref[...]), 0.0)
    a = dt * (-jnp.exp(al_r_ref[...]))
    dtT = jnp.where((blk * BLK + lane) >= PAD, _softplus(miscT + dtb_c_ref[...]), 0.0)
    aT = dtT * (-jnp.exp(al_c_ref[...]))

    r = _iota((BLK, BLK), 0)
    c = _iota((BLK, BLK), 1)
    tri = (c <= r) if d == 0 else (c >= r)
    triT = (r <= c) if d == 0 else (r >= c)
    cum = _dot_r3(jnp.where(tri, 1.0, 0.0).astype(BF16), a)
    cumT = _dot_l3(aT, jnp.where(triT, 1.0, 0.0).astype(BF16))
    yield

    lo = DT_LANE0 + SSD_HEADS * d
    hm = (lane >= lo) & (lane < lo + SSD_HEADS)
    last = BLK - 1 if d == 0 else 0
    tot = cum[last:last + 1, :]
    ecum = jnp.where(hm, jnp.exp(cum), 0.0)
    ecum_hi = ecum.astype(BF16)
    stack = jnp.concatenate([
        jnp.where(hm, dt, 0.0).astype(BF16),
        jnp.where(hm, jnp.exp(tot - cum), 0.0).astype(BF16),
        ecum_hi,
        (ecum - ecum_hi.astype(F32)).astype(BF16)], axis=0)
    ex = jnp.dot(stack, e_ref[d], preferred_element_type=F32)
    dt_x = ex[0:BLK]
    toend_x = ex[BLK:2 * BLK]
    ecum_x = ex[2 * BLK:3 * BLK] + ex[3 * BLK:4 * BLK]
    etot_x = ecum_x[last:last + 1, :]
    xd = xs * dt_x
    xdb = xd.astype(BF16)
    xdw = (xd * toend_x).astype(BF16)
    yield

    y_parts = []
    for g in range(G):
        BgT = Bm[:, g * N:(g + 1) * N].T.astype(BF16)
        Cg = Cb[:, g * N:(g + 1) * N]
        cols = slice(g * R * P, (g + 1) * R * P)
        CB = jnp.dot(Cg, BgT, preferred_element_type=F32)
        S = s_scr[bi, g]
        y_off = jnp.dot(Cg, S.astype(BF16), preferred_element_type=F32) * ecum_x[:, cols]
        s_scr[bi, g] = S * etot_x[:, cols] + jnp.dot(BgT, xdw[:, cols], preferred_element_type=F32)
        yield
        for pr in range(R // 2):
            h0 = g * R + 2 * pr
            xpair = xdb[:, h0 * P:(h0 + 2) * P]
            acc = y_off[:, 2 * pr * P:(2 * pr + 2) * P]
            for hh in range(2):
                li = lo + h0 + hh
                seg = cum[:, li:li + 1] - cumT[li:li + 1, :]
                dec = jnp.exp(jnp.where(tri, seg, -jnp.inf))
                sc = (CB * dec).astype(BF16)
                xm = jnp.where((lane >= hh * P) & (lane < (hh + 1) * P), xpair, 0.0)
                acc = acc + jnp.dot(sc, xm, preferred_element_type=F32)
            y_parts.append(acc)
            yield
    y = jnp.concatenate(y_parts, axis=1)

    if d == 0:
        yf_scr[bi, blk] = y
    else:
        y = (y + yf_scr[bi, blk] + xs * dskip_ref[...]) * _silu(z_ref[bi].astype(F32))
        nw = nw_ref[...]
        W = SSD_WIDTH // G
        outs = [_rms(y[:, g * W:(g + 1) * W], nw[:, g * W:(g + 1) * W]) for g in range(G)]
        o_ref[bi] = jnp.concatenate(outs, axis=1).astype(o_ref.dtype)


def _ssd_kernel(*refs, nblk, nb):
    s_scr = refs[-1]
    s = pl.program_id(1)

    @pl.when((s == 0) | (s == nblk))
    def _():
        s_scr[...] = jnp.zeros_like(s_scr)

    @pl.when(s < nblk)
    def _():
        _round_robin([_ssd_block(bi, *refs, d=0, blk=s) for bi in range(nb)])

    @pl.when(s >= nblk)
    def _():
        _round_robin([_ssd_block(bi, *refs, d=1, blk=2 * nblk - 1 - s) for bi in range(nb)])


def _ssd(proj, misc, xc, dtb_r, dtb_c, al_r, al_c, E, dskip, nw):
    B, Lp, _ = proj.shape
    nblk = Lp // BLK
    nb = _batch_slots(B, 2)
    const2 = lambda b, s: (0, 0)
    return pl.pallas_call(
        functools.partial(_ssd_kernel, nblk=nblk, nb=nb),
        grid=(B // nb, 2 * nblk),
        in_specs=[
            pl.BlockSpec((nb, BLK, SSD_CONV_DIM), lambda b, s: (b, _scan_blk(s, nblk), 0)),
            pl.BlockSpec((nb, BLK, SSD_WIDTH), lambda b, s: (b, _scan_blk(s, nblk), C_Z // SSD_WIDTH)),
            pl.BlockSpec((nb, BLK, 128), lambda b, s: (b, _scan_blk(s, nblk), 0)),
            pl.BlockSpec((1, 128), const2),
            pl.BlockSpec((128, 128), const2),
            pl.BlockSpec((1, 128), const2),
            pl.BlockSpec((128, 128), const2),
            pl.BlockSpec((2, 128, SSD_WIDTH), lambda b, s: (0, 0, 0)),
            pl.BlockSpec((1, SSD_WIDTH), const2),
            pl.BlockSpec((1, SSD_WIDTH), const2),
        ],
        out_specs=pl.BlockSpec((nb, BLK, SSD_WIDTH), lambda b, s: (b, _scan_out_blk(s, nblk), 0)),
        out_shape=jax.ShapeDtypeStruct((B, Lp, SSD_WIDTH), BF16),
        scratch_shapes=[
            pltpu.VMEM((nb, nblk, BLK, SSD_WIDTH), F32),
            pltpu.VMEM((nb, SSD_GROUPS, SSD_STATE, SSD_WIDTH // SSD_GROUPS), F32),
        ],
        compiler_params=_params(("arbitrary", "arbitrary")),
        name="ssd_scan",
    )(xc, proj, misc, dtb_r, dtb_c, al_r, al_c, E, dskip, nw)


def _permute_w_in(w):
    o = np.cumsum([0, 256, 256, 512, 512, 32, 512, 512, 512, 512, 1024, 1536, 32])
    gq, gk, gv, gg, gcode, dq, dk, dv, dg, z, xbc, dt, end = [int(v) for v in o]
    pieces = [w[..., gq:gcode], w[..., dq:dg], w[..., z:xbc], w[..., dg:z], w[..., xbc:dt],
              w[..., gcode:dq], w[..., dt:end],
              jnp.zeros(w.shape[:-1] + (N_PROJ - C_MISC - 64,), w.dtype)]
    return jnp.concatenate(pieces, axis=-1)


def _expansion():
    E = np.zeros((2, 128, SSD_WIDTH), np.float32)
    for d in range(2):
        for h in range(SSD_HEADS):
            E[d, DT_LANE0 + SSD_HEADS * d + h, h * SSD_HEADDIM:(h + 1) * SSD_HEADDIM] = 1.0
    return jnp.asarray(E, BF16)


def _misc_row(p):
    flat = p.reshape(p.shape[0], 1, 2 * SSD_HEADS).astype(F32)
    return jnp.pad(flat, ((0, 0), (0, 0), (DT_LANE0, 128 - DT_LANE0 - 2 * SSD_HEADS)))


def kernel(x, meta_tokens, rel_bias, final_norm_w, norm_w, w_in, w_out, gla_wa2, gla_ba, gla_norm_w,
           diff_lambda, diff_norm_w, conv_w, conv_b, ssd_A_log, ssd_dt_bias, ssd_D, ssd_norm_w):
    B, S, D = x.shape
    assert D == D_MODEL and S % BLK == 0
    Lp = TOK0 + S
    T = 384 if Lp % 384 == 0 else 128

    h = jnp.concatenate([
        jnp.zeros((B, PAD, D), x.dtype),
        jnp.broadcast_to(meta_tokens[None].astype(x.dtype), (B, N_META, D)),
        x], axis=1)

    band = _bias_band(rel_bias, T)
    E = _expansion()
    lam_init = np.array([0.8 - 0.6 * math.exp(-0.3 * l) for l in range(DEPTH)], np.float32)
    lcoef = np.zeros((DEPTH, 1, 128), np.float32)
    lcoef[:, 0, 0] = lam_init
    lcoef[:, 0, 1] = 1.0 - lam_init

    kd = GLA_HEADS * GLA_DK
    wa_p = jnp.zeros((DEPTH, 2, 128, kd), F32)
    wa_p = wa_p.at[:, 0, 0:GLA_RANK].set(gla_wa2[:, 0].astype(F32))
    wa_p = wa_p.at[:, 1, GLA_RANK:2 * GLA_RANK].set(gla_wa2[:, 1].astype(F32))
    dtb_r = _misc_row(ssd_dt_bias)
    al_r = _misc_row(ssd_A_log)
    layers = dict(
        norm_w=norm_w,
        w_in=_permute_w_in(w_in).astype(BF16),
        w_out=w_out.astype(BF16),
        wa_p=wa_p,
        ba=gla_ba.reshape(DEPTH, 2, 1, kd).astype(F32),
        gla_nw=gla_norm_w.reshape(DEPTH, 1, GLA_DV).astype(F32),
        lam=diff_lambda.astype(F32),
        lcoef=jnp.asarray(lcoef),
        diff_nw=diff_norm_w.reshape(DEPTH, 1, DIFF_DV).astype(F32),
        conv_w=conv_w.astype(F32),
        conv_b=conv_b.astype(F32),
        dtb_r=dtb_r,
        dtb_c=jnp.broadcast_to(jnp.swapaxes(dtb_r, 1, 2), (DEPTH, 128, 128)),
        al_r=al_r,
        al_c=jnp.broadcast_to(jnp.swapaxes(al_r, 1, 2), (DEPTH, 128, 128)),
        dskip=jnp.repeat(ssd_D.astype(F32), SSD_HEADDIM, axis=-1).reshape(DEPTH, 1, SSD_WIDTH),
        ssd_nw=ssd_norm_w.reshape(DEPTH, 1, SSD_WIDTH).astype(F32),
    )

    def layer(h, p):
        proj, misc = _inproj(h, p["norm_w"], p["w_in"])
        o_gla = _gla(proj, misc, p["wa_p"], p["ba"], p["gla_nw"])
        o_diff = _attn(proj, band, p["lam"], p["lcoef"], p["diff_nw"], T)
        xc = _conv(proj, p["conv_w"], p["conv_b"])
        o_ssd = _ssd(proj, misc, xc, p["dtb_r"], p["dtb_c"], p["al_r"], p["al_c"], E, p["dskip"], p["ssd_nw"])
        return _outproj(h, o_gla, o_diff, o_ssd, p["w_out"]), None

    h, _ = lax.scan(layer, h, layers)
    return _final_norm(h, final_norm_w)
```

```python
import functools
import math

import numpy as np
import jax
import jax.numpy as jnp
from jax import lax
from jax.experimental import pallas as pl
from jax.experimental.pallas import tpu as pltpu

F32 = jnp.float32
BF16 = jnp.bfloat16

D_MODEL = 2048
DEPTH = 4
N_META = 16
EPS = 1e-6
GLA_HEADS = 4
GLA_DK = 64
GLA_DV = 128
GLA_WIDTH = 512
GLA_RANK = 16
GLA_TAU = 16.0
GLA_CHUNK = 64
DIFF_HEADS = 4
DIFF_DQK = 64
DIFF_DV = 128
DIFF_WIDTH = 512
N_BUCKETS = 32
MAX_DISTANCE = 128
SSD_WIDTH = 1024
SSD_HEADDIM = 64
SSD_HEADS = 16
SSD_GROUPS = 2
SSD_STATE = 128
SSD_CONV = 5
SSD_CHUNK = 128
SSD_CONV_DIM = SSD_WIDTH + 2 * SSD_GROUPS * SSD_STATE

BLK = 128
TOK0 = BLK
PAD = TOK0 - N_META

C_GQ, C_GK, C_GV, C_GG = 0, 256, 512, 1024
C_DQ, C_DK, C_DV = 1536, 2048, 2560
C_Z, C_DG, C_XBC, C_MISC = 3072, 4096, 4608, 6144
N_PROJ = 6272
DT_LANE0 = 2 * GLA_RANK

VMEM_LIMIT = 56 * 1024 * 1024
LOG2E = 1.4426950408889634


def _params(sem, limit=VMEM_LIMIT):
    return pltpu.CompilerParams(dimension_semantics=sem, vmem_limit_bytes=limit)


def _silu(x):
    return x / (1.0 + jnp.exp(-x))


def _softplus(x):
    return jnp.maximum(x, 0.0) + jnp.log(1.0 + jnp.exp(-jnp.abs(x)))


def _log_sigmoid(x):
    return jnp.minimum(x, 0.0) - jnp.log(1.0 + jnp.exp(-jnp.abs(x)))


def _bdot(a, b):
    return jnp.dot(a.astype(BF16), b.astype(BF16), preferred_element_type=F32)


def _split3(x):
    hi = x.astype(BF16)
    r = x - hi.astype(F32)
    mid = r.astype(BF16)
    lo = (r - mid.astype(F32)).astype(BF16)
    return hi, mid, lo


def _dot_l3(a, b_exact):
    hi, mid, lo = _split3(a)
    d = functools.partial(jnp.dot, preferred_element_type=F32)
    return d(hi, b_exact) + d(mid, b_exact) + d(lo, b_exact)


def _dot_r3(a_exact, b):
    hi, mid, lo = _split3(b)
    d = functools.partial(jnp.dot, preferred_element_type=F32)
    return d(a_exact, hi) + d(a_exact, mid) + d(a_exact, lo)


def _dot_22(a, b):
    ah = a.astype(BF16)
    al = (a - ah.astype(F32)).astype(BF16)
    bh = b.astype(BF16)
    bl = (b - bh.astype(F32)).astype(BF16)
    d = functools.partial(jnp.dot, preferred_element_type=F32)
    return d(ah, bh) + d(ah, bl) + d(al, bh)


def _iota(shape, dim):
    return lax.broadcasted_iota(jnp.int32, shape, dim)


def _rms(x, w):
    return x * lax.rsqrt(jnp.mean(x * x, axis=-1, keepdims=True) + EPS) * w


def _inproj_kernel(h_ref, nw_ref, w_ref, wm_ref, o_ref, om_ref, u_scr, *, tm):
    @pl.when(pl.program_id(2) == 0)
    def _():
        x = h_ref[...]
        y = _rms(x, nw_ref[...])
        row = pl.program_id(1) * tm + _iota((tm, 1), 0)
        u_scr[...] = jnp.where(row >= PAD, y, 0.0).astype(BF16)
        om_ref[...] = jnp.dot(u_scr[...], wm_ref[...], preferred_element_type=F32)

    o_ref[...] = jnp.dot(u_scr[...], w_ref[...], preferred_element_type=F32).astype(o_ref.dtype)


def _inproj(h, norm_w, w_p):
    B, Lp, D = h.shape
    tm = Lp // 4
    tn = C_MISC // 4
    return pl.pallas_call(
        functools.partial(_inproj_kernel, tm=tm),
        grid=(B, Lp // tm, C_MISC // tn),
        in_specs=[
            pl.BlockSpec((None, tm, D), lambda b, i, j: (b, i, 0)),
            pl.BlockSpec((1, D), lambda b, i, j: (0, 0)),
            pl.BlockSpec((D, tn), lambda b, i, j: (0, j)),
            pl.BlockSpec((D, 128), lambda b, i, j: (0, C_MISC // 128)),
        ],
        out_specs=[
            pl.BlockSpec((None, tm, tn), lambda b, i, j: (b, i, j)),
            pl.BlockSpec((None, tm, 128), lambda b, i, j: (b, i, 0)),
        ],
        out_shape=[jax.ShapeDtypeStruct((B, Lp, C_MISC), BF16),
                   jax.ShapeDtypeStruct((B, Lp, 128), F32)],
        scratch_shapes=[pltpu.VMEM((tm, D), BF16)],
        compiler_params=_params(("arbitrary", "arbitrary", "arbitrary")),
        name="inproj",
    )(h, norm_w.reshape(1, D), w_p, w_p)


def _outproj_kernel(h_ref, a_ref, b_ref, c_ref, w_ref, o_ref):
    d = functools.partial(jnp.dot, preferred_element_type=F32)
    acc = d(a_ref[...], w_ref[0:GLA_WIDTH, :])
    acc += d(b_ref[...], w_ref[GLA_WIDTH:GLA_WIDTH + DIFF_WIDTH, :])
    acc += d(c_ref[...], w_ref[GLA_WIDTH + DIFF_WIDTH:, :])
    o_ref[...] = h_ref[...] + acc


def _outproj(h, o_gla, o_diff, o_ssd, w_out):
    B, Lp, D = h.shape
    tm = Lp // 8 if (Lp // 8) % 16 == 0 else Lp // 4
    tn = D
    return pl.pallas_call(
        _outproj_kernel,
        grid=(B, Lp // tm, D // tn),
        in_specs=[
            pl.BlockSpec((None, tm, tn), lambda b, i, j: (b, i, j)),
            pl.BlockSpec((None, tm, GLA_WIDTH), lambda b, i, j: (b, i, 0)),
            pl.BlockSpec((None, tm, DIFF_WIDTH), lambda b, i, j: (b, i, 0)),
            pl.BlockSpec((None, tm, SSD_WIDTH), lambda b, i, j: (b, i, 0)),
            pl.BlockSpec((D, tn), lambda b, i, j: (0, j)),
        ],
        out_specs=pl.BlockSpec((None, tm, tn), lambda b, i, j: (b, i, j)),
        out_shape=jax.ShapeDtypeStruct((B, Lp, D), F32),
        input_output_aliases={0: 0},
        compiler_params=_params(("arbitrary", "arbitrary", "arbitrary")),
        name="outproj",
    )(h, o_gla, o_diff, o_ssd, w_out)


def _final_kernel(h_ref, w_ref, o_ref):
    o_ref[...] = _rms(h_ref[...], w_ref[...])


def _final_norm(h, w):
    B, Lp, D = h.shape
    S = Lp - TOK0
    rb = 512 if S % 512 == 0 else BLK
    return pl.pallas_call(
        _final_kernel,
        grid=(B, S // rb),
        in_specs=[
            pl.BlockSpec((pl.Element(rb), pl.Element(D)),
                         lambda b, i: (pl.multiple_of(b * Lp + TOK0 + i * rb, BLK), 0)),
            pl.BlockSpec((1, D), lambda b, i: (0, 0)),
        ],
        out_specs=pl.BlockSpec((None, rb, D), lambda b, i: (b, i, 0)),
        out_shape=jax.ShapeDtypeStruct((B, S, D), F32),
        compiler_params=_params(("arbitrary", "arbitrary")),
        name="final_norm",
    )(h.reshape(B * Lp, D), w.reshape(1, D))


def _round_robin(gens):
    alive = list(gens)
    while alive:
        for g in list(alive):
            try:
                next(g)
            except StopIteration:
                alive.remove(g)


def _gla_block(bi, q_ref, k_ref, v_ref, gate_ref, misc_ref, wa_ref, ba_ref, nw_ref, o_ref,
               of_scr, s_scr, *, d, blk):
    C = GLA_CHUNK
    row = _iota((BLK, 1), 0)
    valid = (blk * BLK + row) >= PAD
    x = _dot_22(misc_ref[bi], wa_ref[d]) + ba_ref[d]
    yield
    g = jnp.where(valid, _log_sigmoid(x) * (1.0 / GLA_TAU), 0.0)

    r = _iota((BLK, BLK), 0)
    c = _iota((BLK, BLK), 1)
    same = (r >= C) == (c >= C)
    tri = (same & (c <= r)) if d == 0 else (same & (c >= r))
    b = _dot_r3(jnp.where(tri, 1.0, 0.0).astype(BF16), g)
    yield
    first, second = (C - 1, 2 * C - 1) if d == 0 else (0, C)

    k = k_ref[bi].astype(F32)
    q_in = q_ref[bi].astype(F32) * (GLA_DK ** -0.5) * jnp.exp(b)
    kT = k.T
    bT = b.T
    lane = _iota((1, BLK), 1)
    blastT = jnp.where(lane < C, bT[:, first:first + 1], bT[:, second:second + 1])
    kinT = (kT * jnp.exp(-bT)).astype(BF16)
    koutT = kT * jnp.exp(blastT - bT)
    vb = v_ref[bi]
    yield

    lane_k = _iota((1, GLA_HEADS * GLA_DK), 1)
    qm = [jnp.where((lane_k >= h * GLA_DK) & (lane_k < (h + 1) * GLA_DK), q_in, 0.0).astype(BF16)
          for h in range(GLA_HEADS)]
    att = [jnp.where(tri, jnp.dot(qm[h], kinT, preferred_element_type=F32), 0.0).astype(BF16)
           for h in range(GLA_HEADS)]
    yield
    o_intra = [jnp.dot(att[h], vb[:, h * GLA_DV:(h + 1) * GLA_DV], preferred_element_type=F32)
               for h in range(GLA_HEADS)]
    yield

    S = s_scr[bi]
    o_inter = [[None, None] for _ in range(GLA_HEADS)]
    for cc in ((0, 1) if d == 0 else (1, 0)):
        Sb = S.astype(BF16)
        for h in range(GLA_HEADS):
            o_inter[h][cc] = jnp.dot(qm[h][cc * C:(cc + 1) * C, :], Sb, preferred_element_type=F32)
        tot = first if cc == 0 else second
        dec = jnp.exp(bT[:, tot:tot + 1])
        kc = jnp.where((lane >= cc * C) & (lane < (cc + 1) * C), koutT, 0.0).astype(BF16)
        upd = [jnp.dot(kc[h * GLA_DK:(h + 1) * GLA_DK, :], vb[:, h * GLA_DV:(h + 1) * GLA_DV],
                       preferred_element_type=F32) for h in range(GLA_HEADS)]
        S = S * dec + jnp.concatenate(upd, axis=0)
        yield
    s_scr[bi] = S

    o = jnp.concatenate(
        [o_intra[h] + jnp.concatenate(o_inter[h], axis=0) for h in range(GLA_HEADS)], axis=1)
    if d == 0:
        of_scr[bi, blk] = o
    else:
        o = o + of_scr[bi, blk]
        gate = gate_ref[bi].astype(F32)
        nw = nw_ref[...]
        outs = []
        for h in range(GLA_HEADS):
            sl = slice(h * GLA_DV, (h + 1) * GLA_DV)
            outs.append(_rms(o[:, sl], nw) * _silu(gate[:, sl]))
        o_ref[bi] = jnp.concatenate(outs, axis=1).astype(o_ref.dtype)


def _gla_kernel(*refs, nblk, nb):
    s_scr = refs[-1]
    s = pl.program_id(1)

    @pl.when((s == 0) | (s == nblk))
    def _():
        s_scr[...] = jnp.zeros_like(s_scr)

    @pl.when(s < nblk)
    def _():
        _round_robin([_gla_block(bi, *refs, d=0, blk=s) for bi in range(nb)])

    @pl.when(s >= nblk)
    def _():
        _round_robin([_gla_block(bi, *refs, d=1, blk=2 * nblk - 1 - s) for bi in range(nb)])


def _scan_blk(s, nblk):
    return jnp.where(s < nblk, s, 2 * nblk - 1 - s)


def _scan_out_blk(s, nblk):
    return jnp.where(s < nblk, nblk - 1, 2 * nblk - 1 - s)


def _batch_slots(B, want):
    nb = want
    while B % nb:
        nb -= 1
    return nb


def _gla(proj, misc, wa_p, ba, nw):
    B, Lp, _ = proj.shape
    nblk = Lp // BLK
    nb = _batch_slots(B, 4)
    kd = GLA_HEADS * GLA_DK
    im = lambda col: (lambda b, s: (b, _scan_blk(s, nblk), col))
    return pl.pallas_call(
        functools.partial(_gla_kernel, nblk=nblk, nb=nb),
        grid=(B // nb, 2 * nblk),
        in_specs=[
            pl.BlockSpec((nb, BLK, kd), im(C_GQ // kd)),
            pl.BlockSpec((nb, BLK, kd), im(C_GK // kd)),
            pl.BlockSpec((nb, BLK, GLA_WIDTH), im(C_GV // GLA_WIDTH)),
            pl.BlockSpec((nb, BLK, GLA_WIDTH), im(C_GG // GLA_WIDTH)),
            pl.BlockSpec((nb, BLK, 128), im(0)),
            pl.BlockSpec((2, 128, kd), lambda b, s: (0, 0, 0)),
            pl.BlockSpec((2, 1, kd), lambda b, s: (0, 0, 0)),
            pl.BlockSpec((1, GLA_DV), lambda b, s: (0, 0)),
        ],
        out_specs=pl.BlockSpec((nb, BLK, GLA_WIDTH), lambda b, s: (b, _scan_out_blk(s, nblk), 0)),
        out_shape=jax.ShapeDtypeStruct((B, Lp, GLA_WIDTH), BF16),
        scratch_shapes=[pltpu.VMEM((nb, nblk, BLK, GLA_WIDTH), F32), pltpu.VMEM((nb, kd, GLA_DV), F32)],
        compiler_params=_params(("arbitrary", "arbitrary")),
        name="gla",
    )(proj, proj, proj, proj, misc, wa_p, ba, nw)


def _attn_slot(bi, i, lam, lc, q_ref, k_ref, v_ref, g_ref, band_ref, nw_ref, o_ref,
               kT_scr, vb_scr, s_scr, e_scr, *, T, nk, RC, KC):
    def fold(x, op):
        r = x[:, 0:128]
        for t in range(1, KC // 128):
            r = op(r, x[:, t * 128:(t + 1) * 128])
        return r

    lane = _iota((1, 2 * DIFF_DQK), 1)
    q = q_ref[bi].astype(F32) * (DIFF_DQK ** -0.5 * LOG2E)
    col0 = _iota((1, KC), 1)
    qcs = [jnp.where((lane >= c * DIFF_DQK) & (lane < (c + 1) * DIFF_DQK), q, 0.0).astype(BF16)
           for c in range(2)]
    groups = [(c, rc) for c in range(2) for rc in range(T // RC)]
    n_sub = nk * T // KC
    bidx = [jnp.clip(j - i, -2, 2) + 2 for j in range(nk)]

    def score_step(gi, u, m_run):
        c, rc = groups[gi]
        rows = slice(rc * RC, (rc + 1) * RC)
        cols = slice(u * KC, (u + 1) * KC)
        j, t = divmod(u, T // KC)
        s = jnp.dot(qcs[c][rows], kT_scr[bi, :, cols], preferred_element_type=F32)
        s = s + band_ref[bidx[j], rows, t * KC:(t + 1) * KC]
        if u == 0:
            s = jnp.where(col0 >= PAD, s, -1e30)
        s_scr[bi, gi % 2, :, cols] = s
        mt = fold(s, jnp.maximum)
        return mt if m_run is None else jnp.maximum(m_run, mt)

    def prob_step(gi, u, m, l_run):
        cols = slice(u * KC, (u + 1) * KC)
        e = jnp.exp2(s_scr[bi, gi % 2, :, cols] - m)
        e_scr[bi, :, cols] = e.astype(BF16)
        lt = fold(e, jnp.add)
        return lt if l_run is None else l_run + lt

    def pv_step(j, pv):
        cols = slice(j * T, (j + 1) * T)
        d = jnp.dot(e_scr[bi, :, cols], vb_scr[bi, cols, :], preferred_element_type=F32)
        return d if pv is None else pv + d

    per_tile = T // KC
    m_run = None
    for u in range(n_sub):
        m_run = score_step(0, u, m_run)
        yield
    parts = []
    for gi in range(len(groups)):
        m = jnp.max(m_run, axis=-1, keepdims=True)
        m_run = l_run = pv = None
        for u in range(n_sub):
            l_run = prob_step(gi, u, m, l_run)
            if u % per_tile == per_tile - 1 and u >= 2 * per_tile - 1:
                pv = pv_step(u // per_tile - 1, pv)
            if gi + 1 < len(groups):
                m_run = score_step(gi + 1, u, m_run)
            yield
        pv = pv_step(nk - 1, pv)
        parts.append(pv / jnp.sum(l_run, axis=-1, keepdims=True))
    per_half = len(groups) // 2
    outs = [jnp.concatenate(parts[:per_half], axis=0), jnp.concatenate(parts[per_half:], axis=0)]

    o = outs[0] - lam * outs[1]
    y = _rms(o, nw_ref[...]) * lc[:, 1:2]
    o_ref[bi] = (y * _silu(g_ref[bi].astype(F32))).astype(o_ref.dtype)


def _attn_kernel(q_ref, k_ref, v_ref, g_ref, band_ref, lam_ref, lc_ref, nw_ref, o_ref,
                 kT_scr, vb_scr, s_scr, e_scr, *, nb, T, nk, RC, KC):
    i = pl.program_id(2)

    @pl.when(i == 0)
    def _():
        for bi in range(nb):
            for j in range(nk):
                kT_scr[bi, :, j * T:(j + 1) * T] = k_ref[bi, j * T:(j + 1) * T, :].astype(F32).T.astype(BF16)
        vb_scr[...] = v_ref[...]

    lp = lam_ref[...]
    lc = lc_ref[...]
    lam = (jnp.exp(jnp.sum(lp[0:1] * lp[1:2], axis=-1, keepdims=True))
           - jnp.exp(jnp.sum(lp[2:3] * lp[3:4], axis=-1, keepdims=True)) + lc[:, 0:1])
    _round_robin([
        _attn_slot(bi, i, lam, lc, q_ref, k_ref, v_ref, g_ref, band_ref, nw_ref, o_ref,
                   kT_scr, vb_scr, s_scr, e_scr, T=T, nk=nk, RC=RC, KC=KC)
        for bi in range(nb)])


def _attn(proj, band, diff_lambda, lcoef, nw, T):
    B, Lp, _ = proj.shape
    nk = Lp // T
    H = DIFF_HEADS
    RC = 128
    KC = 128
    nb = _batch_slots(B, 2)
    return pl.pallas_call(
        functools.partial(_attn_kernel, nb=nb, T=T, nk=nk, RC=RC, KC=KC),
        grid=(B // nb, H, nk),
        in_specs=[
            pl.BlockSpec((nb, T, 128), lambda b, h, i: (b, i, C_DQ // 128 + h)),
            pl.BlockSpec((nb, Lp, 128), lambda b, h, i: (b, 0, C_DK // 128 + h)),
            pl.BlockSpec((nb, Lp, 128), lambda b, h, i: (b, 0, C_DV // 128 + h)),
            pl.BlockSpec((nb, T, 128), lambda b, h, i: (b, i, C_DG // 128 + h)),
            pl.BlockSpec((None, 5, T, T), lambda b, h, i: (h, 0, 0, 0)),
            pl.BlockSpec((4, DIFF_DQK), lambda b, h, i: (0, 0)),
            pl.BlockSpec((1, 128), lambda b, h, i: (0, 0)),
            pl.BlockSpec((1, DIFF_DV), lambda b, h, i: (0, 0)),
        ],
        out_specs=pl.BlockSpec((nb, T, 128), lambda b, h, i: (b, i, h)),
        out_shape=jax.ShapeDtypeStruct((B, Lp, DIFF_WIDTH), BF16),
        scratch_shapes=[
            pltpu.VMEM((nb, 128, Lp), BF16),
            pltpu.VMEM((nb, Lp, 128), BF16),
            pltpu.VMEM((nb, 2, RC, Lp), F32),
            pltpu.VMEM((nb, RC, Lp), BF16),
        ],
        compiler_params=_params(("arbitrary", "arbitrary", "arbitrary")),
        name="diff_attn",
    )(proj, proj, proj, proj, band, diff_lambda, lcoef, nw)


def _bucket_table(T):
    nb = N_BUCKETS // 2
    max_exact = nb // 2
    rel = np.arange(6 * T) - (3 * T - 1)
    ret = np.where(rel > 0, nb, 0)
    n = np.abs(rel)
    nf = np.maximum(n, 1).astype(np.float64)
    large = max_exact + (np.log(nf / max_exact) / math.log(MAX_DISTANCE / max_exact)
                         * (nb - max_exact)).astype(np.int32)
    large = np.minimum(large, nb - 1)
    return (ret + np.where(n < max_exact, n, large)).astype(np.int32)


def _bias_band(rel_bias, T):
    n = 6 * T
    v = (rel_bias.astype(F32)[_bucket_table(T)] * LOG2E).T
    rows = jnp.tile(v, (1, T))[:, :T * (n - 1)].reshape(DIFF_HEADS, T, n - 1)
    band = rows[:, :, T - 1:n - 1].reshape(DIFF_HEADS, T, 5, T)
    return jnp.transpose(band, (0, 2, 1, 3))


def _conv_kernel(x_ref, w_ref, b_ref, o_ref):
    x = x_ref[...].astype(F32)
    w = w_ref[...]
    n = x.shape[0]
    half = (SSD_CONV - 1) // 2
    acc = x * w[half:half + 1, :] + b_ref[...]
    for kk in range(SSD_CONV):
        if kk != half:
            acc = acc + pltpu.roll(x, (half - kk) % n, 0) * w[kk:kk + 1, :]
    o_ref[...] = _silu(acc).astype(o_ref.dtype)


def _conv(proj, conv_w, conv_b):
    B, Lp, _ = proj.shape
    tn = 256
    return pl.pallas_call(
        _conv_kernel,
        grid=(B, SSD_CONV_DIM // tn),
        in_specs=[
            pl.BlockSpec((None, Lp, tn), lambda b, j: (b, 0, C_XBC // tn + j)),
            pl.BlockSpec((SSD_CONV, tn), lambda b, j: (0, j)),
            pl.BlockSpec((1, tn), lambda b, j: (0, j)),
        ],
        out_specs=pl.BlockSpec((None, Lp, tn), lambda b, j: (b, 0, j)),
        out_shape=jax.ShapeDtypeStruct((B, Lp, SSD_CONV_DIM), BF16),
        compiler_params=_params(("arbitrary", "arbitrary")),
        name="ssd_conv",
    )(proj, conv_w, conv_b.reshape(1, SSD_CONV_DIM))


def _ssd_block(bi, xc_ref, z_ref, misc_ref, dtb_r_ref, al_r_ref, e_ref, dskip_ref,
               nw_ref, o_ref, yf_scr, s_scr, *, d, blk):
    G, N, P = SSD_GROUPS, SSD_STATE, SSD_HEADDIM
    R = SSD_HEADS // G
    xs = xc_ref[bi, :, 0:SSD_WIDTH].astype(F32)
    Bm = xc_ref[bi, :, SSD_WIDTH:SSD_WIDTH + G * N].astype(F32)
    Cb = xc_ref[bi, :, SSD_WIDTH + G * N:]
    row = _iota((BLK, 1), 0)
    lane = _iota((1, BLK), 1)
    dt = jnp.where((blk * BLK + row) >= PAD, _softplus(misc_ref[bi] + dtb_r_ref[...]), 0.0)
    a = dt * (-jnp.exp(al_r_ref[...]))
    aT = a.T

    r = _iota((BLK, BLK), 0)
    c = _iota((BLK, BLK), 1)
    tri = (c <= r) if d == 0 else (c >= r)
    triT = (r <= c) if d == 0 else (r >= c)
    cum = _dot_r3(jnp.where(tri, 1.0, 0.0).astype(BF16), a)
    cumT = _dot_l3(aT, jnp.where(triT, 1.0, 0.0).astype(BF16))
    yield

    lo = DT_LANE0 + SSD_HEADS * d
    hm = (lane >= lo) & (lane < lo + SSD_HEADS)
    last = BLK - 1 if d == 0 else 0
    tot = cum[last:last + 1, :]
    ecum = jnp.where(hm, jnp.exp(cum), 0.0)
    ecum_hi = ecum.astype(BF16)
    stack = jnp.concatenate([
        jnp.where(hm, dt, 0.0).astype(BF16),
        jnp.where(hm, jnp.exp(tot - cum), 0.0).astype(BF16),
        ecum_hi,
        (ecum - ecum_hi.astype(F32)).astype(BF16)], axis=0)
    ex = jnp.dot(stack, e_ref[d], preferred_element_type=F32)
    dt_x = ex[0:BLK]
    toend_x = ex[BLK:2 * BLK]
    ecum_x = ex[2 * BLK:3 * BLK] + ex[3 * BLK:4 * BLK]
    etot_x = ecum_x[last:last + 1, :]
    xd = xs * dt_x
    xdb = xd.astype(BF16)
    xdw = (xd * toend_x).astype(BF16)
    yield

    y_parts = []
    for g in range(G):
        BgT = Bm[:, g * N:(g + 1) * N].T.astype(BF16)
        Cg = Cb[:, g * N:(g + 1) * N]
        cols = slice(g * R * P, (g + 1) * R * P)
        CB = jnp.dot(Cg, BgT, preferred_element_type=F32)
        S = s_scr[bi, g]
        y_off = jnp.dot(Cg, S.astype(BF16), preferred_element_type=F32) * ecum_x[:, cols]
        s_scr[bi, g] = S * etot_x[:, cols] + jnp.dot(BgT, xdw[:, cols], preferred_element_type=F32)
        yield
        for pr in range(R // 2):
            h0 = g * R + 2 * pr
            xpair = xdb[:, h0 * P:(h0 + 2) * P]
            acc = y_off[:, 2 * pr * P:(2 * pr + 2) * P]
            for hh in range(2):
                li = lo + h0 + hh
                seg = cum[:, li:li + 1] - cumT[li:li + 1, :]
                dec = jnp.exp(jnp.where(tri, seg, -jnp.inf))
                sc = (CB * dec).astype(BF16)
                xm = jnp.where((lane >= hh * P) & (lane < (hh + 1) * P), xpair, 0.0)
                acc = acc + jnp.dot(sc, xm, preferred_element_type=F32)
            y_parts.append(acc)
            yield
    y = jnp.concatenate(y_parts, axis=1)

    if d == 0:
        yf_scr[bi, blk] = y
    else:
        y = (y + yf_scr[bi, blk] + xs * dskip_ref[...]) * _silu(z_ref[bi].astype(F32))
        nw = nw_ref[...]
        W = SSD_WIDTH // G
        outs = [_rms(y[:, g * W:(g + 1) * W], nw[:, g * W:(g + 1) * W]) for g in range(G)]
        o_ref[bi] = jnp.concatenate(outs, axis=1).astype(o_ref.dtype)


def _ssd_kernel(*refs, nblk, nb):
    s_scr = refs[-1]
    s = pl.program_id(1)

    @pl.when((s == 0) | (s == nblk))
    def _():
        s_scr[...] = jnp.zeros_like(s_scr)

    @pl.when(s < nblk)
    def _():
        _round_robin([_ssd_block(bi, *refs, d=0, blk=s) for bi in range(nb)])

    @pl.when(s >= nblk)
    def _():
        _round_robin([_ssd_block(bi, *refs, d=1, blk=2 * nblk - 1 - s) for bi in range(nb)])


def _ssd(proj, misc, xc, dtb_r, al_r, E, dskip, nw):
    B, Lp, _ = proj.shape
    nblk = Lp // BLK
    nb = _batch_slots(B, 2)
    const2 = lambda b, s: (0, 0)
    return pl.pallas_call(
        functools.partial(_ssd_kernel, nblk=nblk, nb=nb),
        grid=(B // nb, 2 * nblk),
        in_specs=[
            pl.BlockSpec((nb, BLK, SSD_CONV_DIM), lambda b, s: (b, _scan_blk(s, nblk), 0)),
            pl.BlockSpec((nb, BLK, SSD_WIDTH), lambda b, s: (b, _scan_blk(s, nblk), C_Z // SSD_WIDTH)),
            pl.BlockSpec((nb, BLK, 128), lambda b, s: (b, _scan_blk(s, nblk), 0)),
            pl.BlockSpec((1, 128), const2),
            pl.BlockSpec((1, 128), const2),
            pl.BlockSpec((2, 128, SSD_WIDTH), lambda b, s: (0, 0, 0)),
            pl.BlockSpec((1, SSD_WIDTH), const2),
            pl.BlockSpec((1, SSD_WIDTH), const2),
        ],
        out_specs=pl.BlockSpec((nb, BLK, SSD_WIDTH), lambda b, s: (b, _scan_out_blk(s, nblk), 0)),
        out_shape=jax.ShapeDtypeStruct((B, Lp, SSD_WIDTH), BF16),
        scratch_shapes=[
            pltpu.VMEM((nb, nblk, BLK, SSD_WIDTH), F32),
            pltpu.VMEM((nb, SSD_GROUPS, SSD_STATE, SSD_WIDTH // SSD_GROUPS), F32),
        ],
        compiler_params=_params(("arbitrary", "arbitrary")),
        name="ssd_scan",
    )(xc, proj, misc, dtb_r, al_r, E, dskip, nw)


def _permute_w_in(w):
    o = np.cumsum([0, 256, 256, 512, 512, 32, 512, 512, 512, 512, 1024, 1536, 32])
    gq, gk, gv, gg, gcode, dq, dk, dv, dg, z, xbc, dt, end = [int(v) for v in o]
    pieces = [w[..., gq:gcode], w[..., dq:dg], w[..., z:xbc], w[..., dg:z], w[..., xbc:dt],
              w[..., gcode:dq], w[..., dt:end],
              jnp.zeros(w.shape[:-1] + (N_PROJ - C_MISC - 64,), w.dtype)]
    return jnp.concatenate(pieces, axis=-1)


def _expansion():
    E = np.zeros((2, 128, SSD_WIDTH), np.float32)
    for d in range(2):
        for h in range(SSD_HEADS):
            E[d, DT_LANE0 + SSD_HEADS * d + h, h * SSD_HEADDIM:(h + 1) * SSD_HEADDIM] = 1.0
    return jnp.asarray(E, BF16)


def _misc_row(p):
    flat = p.reshape(p.shape[0], 1, 2 * SSD_HEADS).astype(F32)
    return jnp.pad(flat, ((0, 0), (0, 0), (DT_LANE0, 128 - DT_LANE0 - 2 * SSD_HEADS)))


def kernel(x, meta_tokens, rel_bias, final_norm_w, norm_w, w_in, w_out, gla_wa2, gla_ba, gla_norm_w,
           diff_lambda, diff_norm_w, conv_w, conv_b, ssd_A_log, ssd_dt_bias, ssd_D, ssd_norm_w):
    B, S, D = x.shape
    assert D == D_MODEL and S % BLK == 0
    Lp = TOK0 + S
    T = 384 if Lp % 384 == 0 else 128

    h = jnp.concatenate([
        jnp.zeros((B, PAD, D), x.dtype),
        jnp.broadcast_to(meta_tokens[None].astype(x.dtype), (B, N_META, D)),
        x], axis=1)

    band = _bias_band(rel_bias, T)
    E = _expansion()
    lam_init = np.array([0.8 - 0.6 * math.exp(-0.3 * l) for l in range(DEPTH)], np.float32)
    lcoef = np.zeros((DEPTH, 1, 128), np.float32)
    lcoef[:, 0, 0] = lam_init
    lcoef[:, 0, 1] = 1.0 - lam_init

    kd = GLA_HEADS * GLA_DK
    wa_p = jnp.zeros((DEPTH, 2, 128, kd), F32)
    wa_p = wa_p.at[:, 0, 0:GLA_RANK].set(gla_wa2[:, 0].astype(F32))
    wa_p = wa_p.at[:, 1, GLA_RANK:2 * GLA_RANK].set(gla_wa2[:, 1].astype(F32))
    dtb_r = _misc_row(ssd_dt_bias)
    al_r = _misc_row(ssd_A_log)
    layers = dict(
        norm_w=norm_w,
        w_in=_permute_w_in(w_in).astype(BF16),
        w_out=w_out.astype(BF16),
        wa_p=wa_p,
        ba=gla_ba.reshape(DEPTH, 2, 1, kd).astype(F32),
        gla_nw=gla_norm_w.reshape(DEPTH, 1, GLA_DV).astype(F32),
        lam=diff_lambda.astype(F32),
        lcoef=jnp.asarray(lcoef),
        diff_nw=diff_norm_w.reshape(DEPTH, 1, DIFF_DV).astype(F32),
        conv_w=conv_w.astype(F32),
        conv_b=conv_b.astype(F32),
        dtb_r=dtb_r,
        al_r=al_r,
        dskip=jnp.repeat(ssd_D.astype(F32), SSD_HEADDIM, axis=-1).reshape(DEPTH, 1, SSD_WIDTH),
        ssd_nw=ssd_norm_w.reshape(DEPTH, 1, SSD_WIDTH).astype(F32),
    )

    def layer(h, p):
        proj, misc = _inproj(h, p["norm_w"], p["w_in"])
        o_gla = _gla(proj, misc, p["wa_p"], p["ba"], p["gla_nw"])
        o_diff = _attn(proj, band, p["lam"], p["lcoef"], p["diff_nw"], T)
        xc = _conv(proj, p["conv_w"], p["conv_b"])
        o_ssd = _ssd(proj, misc, xc, p["dtb_r"], p["al_r"], E, p["dskip"], p["ssd_nw"])
        return _outproj(h, o_gla, o_diff, o_ssd, p["w_out"]), None

    h, _ = lax.scan(layer, h, layers)
    return _final_norm(h, final_norm_w)
```

```python
import functools
import math

import numpy as np
import jax
import jax.numpy as jnp
from jax import lax
from jax.experimental import pallas as pl
from jax.experimental.pallas import tpu as pltpu

F32 = jnp.float32
BF16 = jnp.bfloat16

D_MODEL = 2048
DEPTH = 4
N_META = 16
EPS = 1e-6
GLA_HEADS = 4
GLA_DK = 64
GLA_DV = 128
GLA_WIDTH = 512
GLA_RANK = 16
GLA_TAU = 16.0
GLA_CHUNK = 64
DIFF_HEADS = 4
DIFF_DQK = 64
DIFF_DV = 128
DIFF_WIDTH = 512
N_BUCKETS = 32
MAX_DISTANCE = 128
SSD_WIDTH = 1024
SSD_HEADDIM = 64
SSD_HEADS = 16
SSD_GROUPS = 2
SSD_STATE = 128
SSD_CONV = 5
SSD_CHUNK = 128
SSD_CONV_DIM = SSD_WIDTH + 2 * SSD_GROUPS * SSD_STATE

BLK = 128
TOK0 = BLK
PAD = TOK0 - N_META

C_GQ, C_GK, C_GV, C_GG = 0, 256, 512, 1024
C_DQ, C_DK, C_DV = 1536, 2048, 2560
C_Z, C_DG, C_XBC, C_MISC = 3072, 4096, 4608, 6144
N_PROJ = 6272
DT_LANE0 = 2 * GLA_RANK

VMEM_LIMIT = 56 * 1024 * 1024
LOG2E = 1.4426950408889634


def _params(sem, limit=VMEM_LIMIT):
    return pltpu.CompilerParams(dimension_semantics=sem, vmem_limit_bytes=limit)


def _silu(x):
    return x / (1.0 + jnp.exp(-x))


def _softplus(x):
    return jnp.maximum(x, 0.0) + jnp.log(1.0 + jnp.exp(-jnp.abs(x)))


def _log_sigmoid(x):
    return jnp.minimum(x, 0.0) - jnp.log(1.0 + jnp.exp(-jnp.abs(x)))


def _bdot(a, b):
    return jnp.dot(a.astype(BF16), b.astype(BF16), preferred_element_type=F32)


def _split3(x):
    hi = x.astype(BF16)
    r = x - hi.astype(F32)
    mid = r.astype(BF16)
    lo = (r - mid.astype(F32)).astype(BF16)
    return hi, mid, lo


def _dot_l3(a, b_exact):
    hi, mid, lo = _split3(a)
    d = functools.partial(jnp.dot, preferred_element_type=F32)
    return d(hi, b_exact) + d(mid, b_exact) + d(lo, b_exact)


def _dot_r3(a_exact, b):
    hi, mid, lo = _split3(b)
    d = functools.partial(jnp.dot, preferred_element_type=F32)
    return d(a_exact, hi) + d(a_exact, mid) + d(a_exact, lo)


def _dot_22(a, b):
    ah = a.astype(BF16)
    al = (a - ah.astype(F32)).astype(BF16)
    bh = b.astype(BF16)
    bl = (b - bh.astype(F32)).astype(BF16)
    d = functools.partial(jnp.dot, preferred_element_type=F32)
    return d(ah, bh) + d(ah, bl) + d(al, bh)


def _iota(shape, dim):
    return lax.broadcasted_iota(jnp.int32, shape, dim)


def _rms(x, w):
    return x * lax.rsqrt(jnp.mean(x * x, axis=-1, keepdims=True) + EPS) * w


def _inproj_kernel(h_ref, nw_ref, w_ref, wm_ref, o_ref, om_ref, u_scr, *, tm):
    @pl.when(pl.program_id(2) == 0)
    def _():
        x = h_ref[...]
        y = _rms(x, nw_ref[...])
        row = pl.program_id(1) * tm + _iota((tm, 1), 0)
        u_scr[...] = jnp.where(row >= PAD, y, 0.0).astype(BF16)
        om_ref[...] = jnp.dot(u_scr[...], wm_ref[...], preferred_element_type=F32)

    o_ref[...] = jnp.dot(u_scr[...], w_ref[...], preferred_element_type=F32).astype(o_ref.dtype)


def _inproj(h, norm_w, w_p):
    B, Lp, D = h.shape
    tm = Lp // 4
    tn = C_MISC // 4
    return pl.pallas_call(
        functools.partial(_inproj_kernel, tm=tm),
        grid=(B, Lp // tm, C_MISC // tn),
        in_specs=[
            pl.BlockSpec((None, tm, D), lambda b, i, j: (b, i, 0)),
            pl.BlockSpec((1, D), lambda b, i, j: (0, 0)),
            pl.BlockSpec((D, tn), lambda b, i, j: (0, j)),
            pl.BlockSpec((D, 128), lambda b, i, j: (0, C_MISC // 128)),
        ],
        out_specs=[
            pl.BlockSpec((None, tm, tn), lambda b, i, j: (b, i, j)),
            pl.BlockSpec((None, tm, 128), lambda b, i, j: (b, i, 0)),
        ],
        out_shape=[jax.ShapeDtypeStruct((B, Lp, C_MISC), BF16),
                   jax.ShapeDtypeStruct((B, Lp, 128), F32)],
        scratch_shapes=[pltpu.VMEM((tm, D), BF16)],
        compiler_params=_params(("arbitrary", "arbitrary", "arbitrary")),
        name="inproj",
    )(h, norm_w.reshape(1, D), w_p, w_p)


def _outproj_kernel(h_ref, a_ref, b_ref, c_ref, w_ref, o_ref):
    d = functools.partial(jnp.dot, preferred_element_type=F32)
    acc = d(a_ref[...], w_ref[0:GLA_WIDTH, :])
    acc += d(b_ref[...], w_ref[GLA_WIDTH:GLA_WIDTH + DIFF_WIDTH, :])
    acc += d(c_ref[...], w_ref[GLA_WIDTH + DIFF_WIDTH:, :])
    o_ref[...] = h_ref[...] + acc


def _outproj(h, o_gla, o_diff, o_ssd, w_out):
    B, Lp, D = h.shape
    tm = Lp // 8 if (Lp // 8) % 16 == 0 else Lp // 4
    tn = D
    return pl.pallas_call(
        _outproj_kernel,
        grid=(B, Lp // tm, D // tn),
        in_specs=[
            pl.BlockSpec((None, tm, tn), lambda b, i, j: (b, i, j)),
            pl.BlockSpec((None, tm, GLA_WIDTH), lambda b, i, j: (b, i, 0)),
            pl.BlockSpec((None, tm, DIFF_WIDTH), lambda b, i, j: (b, i, 0)),
            pl.BlockSpec((None, tm, SSD_WIDTH), lambda b, i, j: (b, i, 0)),
            pl.BlockSpec((D, tn), lambda b, i, j: (0, j)),
        ],
        out_specs=pl.BlockSpec((None, tm, tn), lambda b, i, j: (b, i, j)),
        out_shape=jax.ShapeDtypeStruct((B, Lp, D), F32),
        input_output_aliases={0: 0},
        compiler_params=_params(("arbitrary", "arbitrary", "arbitrary")),
        name="outproj",
    )(h, o_gla, o_diff, o_ssd, w_out)


def _final_kernel(h_ref, w_ref, o_ref):
    o_ref[...] = _rms(h_ref[...], w_ref[...])


def _final_norm(h, w):
    B, Lp, D = h.shape
    S = Lp - TOK0
    rb = 512 if S % 512 == 0 else BLK
    return pl.pallas_call(
        _final_kernel,
        grid=(B, S // rb),
        in_specs=[
            pl.BlockSpec((pl.Element(rb), pl.Element(D)),
                         lambda b, i: (pl.multiple_of(b * Lp + TOK0 + i * rb, BLK), 0)),
            pl.BlockSpec((1, D), lambda b, i: (0, 0)),
        ],
        out_specs=pl.BlockSpec((None, rb, D), lambda b, i: (b, i, 0)),
        out_shape=jax.ShapeDtypeStruct((B, S, D), F32),
        compiler_params=_params(("arbitrary", "arbitrary")),
        name="final_norm",
    )(h.reshape(B * Lp, D), w.reshape(1, D))


def _round_robin(gens):
    alive = list(gens)
    while alive:
        for g in list(alive):
            try:
                next(g)
            except StopIteration:
                alive.remove(g)


def _gla_block(bi, q_ref, k_ref, v_ref, gate_ref, misc_ref, wa_ref, ba_ref, nw_ref, o_ref,
               of_scr, s_scr, *, d, blk):
    C = GLA_CHUNK
    row = _iota((BLK, 1), 0)
    valid = (blk * BLK + row) >= PAD
    x = _dot_22(misc_ref[bi], wa_ref[d]) + ba_ref[d]
    yield
    g = jnp.where(valid, _log_sigmoid(x) * (1.0 / GLA_TAU), 0.0)

    r = _iota((BLK, BLK), 0)
    c = _iota((BLK, BLK), 1)
    same = (r >= C) == (c >= C)
    tri = (same & (c <= r)) if d == 0 else (same & (c >= r))
    b = _dot_r3(jnp.where(tri, 1.0, 0.0).astype(BF16), g)
    yield
    first, second = (C - 1, 2 * C - 1) if d == 0 else (0, C)

    k = k_ref[bi].astype(F32)
    q_in = q_ref[bi].astype(F32) * (GLA_DK ** -0.5) * jnp.exp(b)
    kT = k.T
    bT = b.T
    lane = _iota((1, BLK), 1)
    blastT = jnp.where(lane < C, bT[:, first:first + 1], bT[:, second:second + 1])
    kinT = (kT * jnp.exp(-bT)).astype(BF16)
    koutT = kT * jnp.exp(blastT - bT)
    vb = v_ref[bi]
    yield

    lane_k = _iota((1, GLA_HEADS * GLA_DK), 1)
    qm = [jnp.where((lane_k >= h * GLA_DK) & (lane_k < (h + 1) * GLA_DK), q_in, 0.0).astype(BF16)
          for h in range(GLA_HEADS)]
    att = [jnp.where(tri, jnp.dot(qm[h], kinT, preferred_element_type=F32), 0.0).astype(BF16)
           for h in range(GLA_HEADS)]
    yield
    o_intra = [jnp.dot(att[h], vb[:, h * GLA_DV:(h + 1) * GLA_DV], preferred_element_type=F32)
               for h in range(GLA_HEADS)]
    yield

    S = s_scr[bi]
    o_inter = [[None, None] for _ in range(GLA_HEADS)]
    for cc in ((0, 1) if d == 0 else (1, 0)):
        Sb = S.astype(BF16)
        for h in range(GLA_HEADS):
            o_inter[h][cc] = jnp.dot(qm[h][cc * C:(cc + 1) * C, :], Sb, preferred_element_type=F32)
        tot = first if cc == 0 else second
        dec = jnp.exp(bT[:, tot:tot + 1])
        kc = jnp.where((lane >= cc * C) & (lane < (cc + 1) * C), koutT, 0.0).astype(BF16)
        upd = [jnp.dot(kc[h * GLA_DK:(h + 1) * GLA_DK, :], vb[:, h * GLA_DV:(h + 1) * GLA_DV],
                       preferred_element_type=F32) for h in range(GLA_HEADS)]
        S = S * dec + jnp.concatenate(upd, axis=0)
        yield
    s_scr[bi] = S

    o = jnp.concatenate(
        [o_intra[h] + jnp.concatenate(o_inter[h], axis=0) for h in range(GLA_HEADS)], axis=1)
    if d == 0:
        of_scr[bi, blk] = o
    else:
        o = o + of_scr[bi, blk]
        gate = gate_ref[bi].astype(F32)
        nw = nw_ref[...]
        outs = []
        for h in range(GLA_HEADS):
            sl = slice(h * GLA_DV, (h + 1) * GLA_DV)
            outs.append(_rms(o[:, sl], nw) * _silu(gate[:, sl]))
        o_ref[bi] = jnp.concatenate(outs, axis=1).astype(o_ref.dtype)


def _gla_kernel(*refs, nblk, nb):
    s_scr = refs[-1]
    s = pl.program_id(1)

    @pl.when((s == 0) | (s == nblk))
    def _():
        s_scr[...] = jnp.zeros_like(s_scr)

    @pl.when(s < nblk)
    def _():
        _round_robin([_gla_block(bi, *refs, d=0, blk=s) for bi in range(nb)])

    @pl.when(s >= nblk)
    def _():
        _round_robin([_gla_block(bi, *refs, d=1, blk=2 * nblk - 1 - s) for bi in range(nb)])


def _scan_blk(s, nblk):
    return jnp.where(s < nblk, s, 2 * nblk - 1 - s)


def _scan_out_blk(s, nblk):
    return jnp.where(s < nblk, nblk - 1, 2 * nblk - 1 - s)


def _batch_slots(B, want):
    nb = want
    while B % nb:
        nb -= 1
    return nb


def _gla(proj, misc, wa_p, ba, nw):
    B, Lp, _ = proj.shape
    nblk = Lp // BLK
    nb = _batch_slots(B, 4)
    kd = GLA_HEADS * GLA_DK
    im = lambda col: (lambda b, s: (b, _scan_blk(s, nblk), col))
    return pl.pallas_call(
        functools.partial(_gla_kernel, nblk=nblk, nb=nb),
        grid=(B // nb, 2 * nblk),
        in_specs=[
            pl.BlockSpec((nb, BLK, kd), im(C_GQ // kd)),
            pl.BlockSpec((nb, BLK, kd), im(C_GK // kd)),
            pl.BlockSpec((nb, BLK, GLA_WIDTH), im(C_GV // GLA_WIDTH)),
            pl.BlockSpec((nb, BLK, GLA_WIDTH), im(C_GG // GLA_WIDTH)),
            pl.BlockSpec((nb, BLK, 128), im(0)),
            pl.BlockSpec((2, 128, kd), lambda b, s: (0, 0, 0)),
            pl.BlockSpec((2, 1, kd), lambda b, s: (0, 0, 0)),
            pl.BlockSpec((1, GLA_DV), lambda b, s: (0, 0)),
        ],
        out_specs=pl.BlockSpec((nb, BLK, GLA_WIDTH), lambda b, s: (b, _scan_out_blk(s, nblk), 0)),
        out_shape=jax.ShapeDtypeStruct((B, Lp, GLA_WIDTH), BF16),
        scratch_shapes=[pltpu.VMEM((nb, nblk, BLK, GLA_WIDTH), F32), pltpu.VMEM((nb, kd, GLA_DV), F32)],
        compiler_params=_params(("arbitrary", "arbitrary")),
        name="gla",
    )(proj, proj, proj, proj, misc, wa_p, ba, nw)


ATTN_BAND_TILES = 3
ATTN_L_FLOOR = 2.0 ** -60


def _q_halves(q_ref, bi):
    lane = _iota((1, 2 * DIFF_DQK), 1)
    q = q_ref[bi].astype(F32) * (DIFF_DQK ** -0.5 * LOG2E)
    return [jnp.where((lane >= c * DIFF_DQK) & (lane < (c + 1) * DIFF_DQK), q, 0.0).astype(BF16)
            for c in range(2)]


def _attn_slot(bi, i, q_ref, v_ref, band_ref, bmax_ref, kT_scr, kn_scr, l_scr, acc_scr, *, T, nk):
    sub = T // 128
    qb = _q_halves(q_ref, bi)
    qstack = jnp.concatenate(qb, axis=0)
    kn = kn_scr[bi]
    m = jnp.concatenate([
        jnp.sqrt(jnp.sum(x.astype(F32) ** 2, axis=-1, keepdims=True)) * kn[c:c + 1, :] + bmax_ref[...]
        for c, x in enumerate(qb)], axis=0)
    off_l = m - band_ref[0, 0:1, 0:128]
    off_r = m - band_ref[4, 0:1, 0:128]
    b0 = jnp.clip(i - 1, 0, nk - ATTN_BAND_TILES)
    padrow = jnp.where(_iota((1, 128), 1) < PAD, 3.0e4, 0.0)
    l_scr[bi] = jnp.zeros(l_scr.shape[1:], F32)
    acc_scr[bi] = jnp.zeros(acc_scr.shape[1:], F32)

    def sub_tile(j, t, off, bidx):
        cols = slice(t * 128, (t + 1) * 128)
        x = jnp.dot(qstack, kT_scr[bi, j, :, cols], preferred_element_type=F32) - off
        if bidx is not None:
            bt = band_ref[bidx, :, cols]
            x = x + jnp.concatenate([bt, bt], axis=0)
        if t == 0:
            x = x - jnp.where(j == 0, padrow, 0.0)
        e = jnp.exp2(x)
        l_scr[bi] += e
        vt = v_ref[bi, pl.ds(pl.multiple_of(j * T + t * 128, 128), 128), :]
        acc_scr[bi] += jnp.dot(e.astype(BF16), vt, preferred_element_type=F32)

    for jj in range(nk - ATTN_BAND_TILES):
        j = jj + jnp.where(jj >= b0, ATTN_BAND_TILES, 0)
        off = jnp.where(j < i, off_l, off_r)
        for t in range(sub):
            sub_tile(j, t, off, None)
            yield
    for b in range(ATTN_BAND_TILES):
        j = b0 + b
        for t in range(sub):
            sub_tile(j, t, m, j - i + 2)
            yield


def _attn_exact(bi, i, q_ref, v_ref, band_ref, kT_scr, l_scr, acc_scr, *, T, nk):
    sub = T // 128
    qb = _q_halves(q_ref, bi)
    col = _iota((1, T), 1)

    def fold(x, op):
        r = x[:, 0:128]
        for t in range(1, sub):
            r = op(r, x[:, t * 128:(t + 1) * 128])
        return r

    for c in range(2):
        for r0 in range(0, T, 128):
            rows = slice(r0, r0 + 128)
            qc = qb[c][rows]

            def scores(j):
                s = jnp.dot(qc, kT_scr[bi, j], preferred_element_type=F32)
                s = s + band_ref[jnp.clip(j - i, -2, 2) + 2, rows, :]
                return jnp.where((j == 0) & (col < PAD), -1e30, s)

            m_run = lax.fori_loop(0, nk, lambda j, mr: jnp.maximum(mr, fold(scores(j), jnp.maximum)),
                                  jnp.full((128, 128), -jnp.inf, F32))
            m = jnp.max(m_run, axis=-1, keepdims=True)

            def body(j, carry):
                l_run, acc = carry
                e = jnp.exp2(scores(j) - m)
                vt = v_ref[bi, pl.ds(pl.multiple_of(j * T, 128), T), :]
                return l_run + fold(e, jnp.add), acc + jnp.dot(e.astype(BF16), vt, preferred_element_type=F32)

            l_run, acc = lax.fori_loop(0, nk, body, (jnp.zeros((128, 128), F32), jnp.zeros((128, 128), F32)))
            l_scr[bi, c * T + r0:c * T + r0 + 128, :] = l_run
            acc_scr[bi, c * T + r0:c * T + r0 + 128, :] = acc


def _attn_kernel(q_ref, k_ref, v_ref, g_ref, band_ref, bmax_ref, lam_ref, lc_ref, nw_ref, o_ref,
                 kT_scr, kn_scr, l_scr, acc_scr, *, nb, T, nk):
    i = pl.program_id(2)

    @pl.when(i == 0)
    def _():
        lane = _iota((1, 2 * DIFF_DQK), 1)
        for bi in range(nb):
            for j in range(nk):
                kT_scr[bi, j] = k_ref[bi, j * T:(j + 1) * T, :].astype(F32).T.astype(BF16)
            k2 = k_ref[bi].astype(F32) ** 2
            for c in range(2):
                half = (lane >= c * DIFF_DQK) & (lane < (c + 1) * DIFF_DQK)
                n2 = jnp.max(jnp.sum(jnp.where(half, k2, 0.0), axis=-1, keepdims=True), axis=0, keepdims=True)
                kn_scr[bi, c:c + 1, :] = jnp.broadcast_to(jnp.sqrt(n2), (1, 128))

    _round_robin([
        _attn_slot(bi, i, q_ref, v_ref, band_ref, bmax_ref, kT_scr, kn_scr, l_scr, acc_scr, T=T, nk=nk)
        for bi in range(nb)])

    lp = lam_ref[...]
    lc = lc_ref[...]
    lam = (jnp.exp(jnp.sum(lp[0:1] * lp[1:2], axis=-1, keepdims=True))
           - jnp.exp(jnp.sum(lp[2:3] * lp[3:4], axis=-1, keepdims=True)) + lc[:, 0:1])
    for bi in range(nb):
        @pl.when(jnp.min(jnp.sum(l_scr[bi], axis=-1, keepdims=True)) < ATTN_L_FLOOR)
        def _():
            _attn_exact(bi, i, q_ref, v_ref, band_ref, kT_scr, l_scr, acc_scr, T=T, nk=nk)

        out = acc_scr[bi] / jnp.sum(l_scr[bi], axis=-1, keepdims=True)
        o = out[0:T] - lam * out[T:2 * T]
        y = _rms(o, nw_ref[...]) * lc[:, 1:2]
        o_ref[bi] = (y * _silu(g_ref[bi].astype(F32))).astype(o_ref.dtype)


def _attn(proj, band, bmax, diff_lambda, lcoef, nw, T):
    B, Lp, _ = proj.shape
    nk = Lp // T
    H = DIFF_HEADS
    assert nk >= ATTN_BAND_TILES
    nb = _batch_slots(B, 2)
    return pl.pallas_call(
        functools.partial(_attn_kernel, nb=nb, T=T, nk=nk),
        grid=(B // nb, H, nk),
        in_specs=[
            pl.BlockSpec((nb, T, 128), lambda b, h, i: (b, i, C_DQ // 128 + h)),
            pl.BlockSpec((nb, Lp, 128), lambda b, h, i: (b, 0, C_DK // 128 + h)),
            pl.BlockSpec((nb, Lp, 128), lambda b, h, i: (b, 0, C_DV // 128 + h)),
            pl.BlockSpec((nb, T, 128), lambda b, h, i: (b, i, C_DG // 128 + h)),
            pl.BlockSpec((None, 5, T, T), lambda b, h, i: (h, 0, 0, 0)),
            pl.BlockSpec((None, 1, 128), lambda b, h, i: (h, 0, 0)),
            pl.BlockSpec((4, DIFF_DQK), lambda b, h, i: (0, 0)),
            pl.BlockSpec((1, 128), lambda b, h, i: (0, 0)),
            pl.BlockSpec((1, DIFF_DV), lambda b, h, i: (0, 0)),
        ],
        out_specs=pl.BlockSpec((nb, T, 128), lambda b, h, i: (b, i, h)),
        out_shape=jax.ShapeDtypeStruct((B, Lp, DIFF_WIDTH), BF16),
        scratch_shapes=[
            pltpu.VMEM((nb, nk, 128, T), BF16),
            pltpu.VMEM((nb, 8, 128), F32),
            pltpu.VMEM((nb, 2 * T, 128), F32),
            pltpu.VMEM((nb, 2 * T, 128), F32),
        ],
        compiler_params=_params(("arbitrary", "arbitrary", "arbitrary")),
        name="diff_attn",
    )(proj, proj, proj, proj, band, bmax, diff_lambda, lcoef, nw)


def _bucket_table(T):
    nb = N_BUCKETS // 2
    max_exact = nb // 2
    rel = np.arange(6 * T) - (3 * T - 1)
    ret = np.where(rel > 0, nb, 0)
    n = np.abs(rel)
    nf = np.maximum(n, 1).astype(np.float64)
    large = max_exact + (np.log(nf / max_exact) / math.log(MAX_DISTANCE / max_exact)
                         * (nb - max_exact)).astype(np.int32)
    large = np.minimum(large, nb - 1)
    return (ret + np.where(n < max_exact, n, large)).astype(np.int32)


def _bias_band(rel_bias, T):
    n = 6 * T
    v = (rel_bias.astype(F32)[_bucket_table(T)] * LOG2E).T
    rows = jnp.tile(v, (1, T))[:, :T * (n - 1)].reshape(DIFF_HEADS, T, n - 1)
    band = rows[:, :, T - 1:n - 1].reshape(DIFF_HEADS, T, 5, T)
    return jnp.transpose(band, (0, 2, 1, 3))


def _conv_kernel(x_ref, w_ref, b_ref, o_ref):
    x = x_ref[...].astype(F32)
    w = w_ref[...]
    n = x.shape[0]
    half = (SSD_CONV - 1) // 2
    acc = x * w[half:half + 1, :] + b_ref[...]
    for kk in range(SSD_CONV):
        if kk != half:
            acc = acc + pltpu.roll(x, (half - kk) % n, 0) * w[kk:kk + 1, :]
    o_ref[...] = _silu(acc).astype(o_ref.dtype)


def _conv(proj, conv_w, conv_b):
    B, Lp, _ = proj.shape
    tn = 256
    return pl.pallas_call(
        _conv_kernel,
        grid=(B, SSD_CONV_DIM // tn),
        in_specs=[
            pl.BlockSpec((None, Lp, tn), lambda b, j: (b, 0, C_XBC // tn + j)),
            pl.BlockSpec((SSD_CONV, tn), lambda b, j: (0, j)),
            pl.BlockSpec((1, tn), lambda b, j: (0, j)),
        ],
        out_specs=pl.BlockSpec((None, Lp, tn), lambda b, j: (b, 0, j)),
        out_shape=jax.ShapeDtypeStruct((B, Lp, SSD_CONV_DIM), BF16),
        compiler_params=_params(("arbitrary", "arbitrary")),
        name="ssd_conv",
    )(proj, conv_w, conv_b.reshape(1, SSD_CONV_DIM))


def _ssd_block(bi, xc_ref, z_ref, misc_ref, dtb_r_ref, al_r_ref, e_ref, dskip_ref,
               nw_ref, o_ref, yf_scr, s_scr, *, d, blk):
    G, N, P = SSD_GROUPS, SSD_STATE, SSD_HEADDIM
    R = SSD_HEADS // G
    xs = xc_ref[bi, :, 0:SSD_WIDTH].astype(F32)
    Bm = xc_ref[bi, :, SSD_WIDTH:SSD_WIDTH + G * N].astype(F32)
    Cb = xc_ref[bi, :, SSD_WIDTH + G * N:]
    row = _iota((BLK, 1), 0)
    lane = _iota((1, BLK), 1)
    dt = jnp.where((blk * BLK + row) >= PAD, _softplus(misc_ref[bi] + dtb_r_ref[...]), 0.0)
    a = dt * (-jnp.exp(al_r_ref[...]))
    aT = a.T

    r = _iota((BLK, BLK), 0)
    c = _iota((BLK, BLK), 1)
    tri = (c <= r) if d == 0 else (c >= r)
    triT = (r <= c) if d == 0 else (r >= c)
    cum = _dot_r3(jnp.where(tri, 1.0, 0.0).astype(BF16), a)
    cumT = _dot_l3(aT, jnp.where(triT, 1.0, 0.0).astype(BF16))
    yield

    lo = DT_LANE0 + SSD_HEADS * d
    hm = (lane >= lo) & (lane < lo + SSD_HEADS)
    last = BLK - 1 if d == 0 else 0
    tot = cum[last:last + 1, :]
    ecum = jnp.where(hm, jnp.exp(cum), 0.0)
    ecum_hi = ecum.astype(BF16)
    stack = jnp.concatenate([
        jnp.where(hm, dt, 0.0).astype(BF16),
        jnp.where(hm, jnp.exp(tot - cum), 0.0).astype(BF16),
        ecum_hi,
        (ecum - ecum_hi.astype(F32)).astype(BF16)], axis=0)
    ex = jnp.dot(stack, e_ref[d], preferred_element_type=F32)
    dt_x = ex[0:BLK]
    toend_x = ex[BLK:2 * BLK]
    ecum_x = ex[2 * BLK:3 * BLK] + ex[3 * BLK:4 * BLK]
    etot_x = ecum_x[last:last + 1, :]
    xd = xs * dt_x
    xdb = xd.astype(BF16)
    xdw = (xd * toend_x).astype(BF16)
    yield

    y_parts = []
    for g in range(G):
        BgT = Bm[:, g * N:(g + 1) * N].T.astype(BF16)
        Cg = Cb[:, g * N:(g + 1) * N]
        cols = slice(g * R * P, (g + 1) * R * P)
        CB = jnp.dot(Cg, BgT, preferred_element_type=F32)
        S = s_scr[bi, g]
        y_off = jnp.dot(Cg, S.astype(BF16), preferred_element_type=F32) * ecum_x[:, cols]
        s_scr[bi, g] = S * etot_x[:, cols] + jnp.dot(BgT, xdw[:, cols], preferred_element_type=F32)
        yield
        for pr in range(R // 2):
            h0 = g * R + 2 * pr
            xpair = xdb[:, h0 * P:(h0 + 2) * P]
            acc = y_off[:, 2 * pr * P:(2 * pr + 2) * P]
            for hh in range(2):
                li = lo + h0 + hh
                seg = cum[:, li:li + 1] - cumT[li:li + 1, :]
                dec = jnp.exp(jnp.where(tri, seg, -jnp.inf))
                sc = (CB * dec).astype(BF16)
                xm = jnp.where((lane >= hh * P) & (lane < (hh + 1) * P), xpair, 0.0)
                acc = acc + jnp.dot(sc, xm, preferred_element_type=F32)
            y_parts.append(acc)
            yield
    y = jnp.concatenate(y_parts, axis=1)

    if d == 0:
        yf_scr[bi, blk] = y
    else:
        y = (y + yf_scr[bi, blk] + xs * dskip_ref[...]) * _silu(z_ref[bi].astype(F32))
        nw = nw_ref[...]
        W = SSD_WIDTH // G
        outs = [_rms(y[:, g * W:(g + 1) * W], nw[:, g * W:(g + 1) * W]) for g in range(G)]
        o_ref[bi] = jnp.concatenate(outs, axis=1).astype(o_ref.dtype)


def _ssd_kernel(*refs, nblk, nb):
    s_scr = refs[-1]
    s = pl.program_id(1)

    @pl.when((s == 0) | (s == nblk))
    def _():
        s_scr[...] = jnp.zeros_like(s_scr)

    @pl.when(s < nblk)
    def _():
        _round_robin([_ssd_block(bi, *refs, d=0, blk=s) for bi in range(nb)])

    @pl.when(s >= nblk)
    def _():
        _round_robin([_ssd_block(bi, *refs, d=1, blk=2 * nblk - 1 - s) for bi in range(nb)])


def _ssd(proj, misc, xc, dtb_r, al_r, E, dskip, nw):
    B, Lp, _ = proj.shape
    nblk = Lp // BLK
    nb = _batch_slots(B, 2)
    const2 = lambda b, s: (0, 0)
    return pl.pallas_call(
        functools.partial(_ssd_kernel, nblk=nblk, nb=nb),
        grid=(B // nb, 2 * nblk),
        in_specs=[
            pl.BlockSpec((nb, BLK, SSD_CONV_DIM), lambda b, s: (b, _scan_blk(s, nblk), 0)),
            pl.BlockSpec((nb, BLK, SSD_WIDTH), lambda b, s: (b, _scan_blk(s, nblk), C_Z // SSD_WIDTH)),
            pl.BlockSpec((nb, BLK, 128), lambda b, s: (b, _scan_blk(s, nblk), 0)),
            pl.BlockSpec((1, 128), const2),
            pl.BlockSpec((1, 128), const2),
            pl.BlockSpec((2, 128, SSD_WIDTH), lambda b, s: (0, 0, 0)),
            pl.BlockSpec((1, SSD_WIDTH), const2),
            pl.BlockSpec((1, SSD_WIDTH), const2),
        ],
        out_specs=pl.BlockSpec((nb, BLK, SSD_WIDTH), lambda b, s: (b, _scan_out_blk(s, nblk), 0)),
        out_shape=jax.ShapeDtypeStruct((B, Lp, SSD_WIDTH), BF16),
        scratch_shapes=[
            pltpu.VMEM((nb, nblk, BLK, SSD_WIDTH), F32),
            pltpu.VMEM((nb, SSD_GROUPS, SSD_STATE, SSD_WIDTH // SSD_GROUPS), F32),
        ],
        compiler_params=_params(("arbitrary", "arbitrary")),
        name="ssd_scan",
    )(xc, proj, misc, dtb_r, al_r, E, dskip, nw)


def _permute_w_in(w):
    o = np.cumsum([0, 256, 256, 512, 512, 32, 512, 512, 512, 512, 1024, 1536, 32])
    gq, gk, gv, gg, gcode, dq, dk, dv, dg, z, xbc, dt, end = [int(v) for v in o]
    pieces = [w[..., gq:gcode], w[..., dq:dg], w[..., z:xbc], w[..., dg:z], w[..., xbc:dt],
              w[..., gcode:dq], w[..., dt:end],
              jnp.zeros(w.shape[:-1] + (N_PROJ - C_MISC - 64,), w.dtype)]
    return jnp.concatenate(pieces, axis=-1)


def _expansion():
    E = np.zeros((2, 128, SSD_WIDTH), np.float32)
    for d in range(2):
        for h in range(SSD_HEADS):
            E[d, DT_LANE0 + SSD_HEADS * d + h, h * SSD_HEADDIM:(h + 1) * SSD_HEADDIM] = 1.0
    return jnp.asarray(E, BF16)


def _misc_row(p):
    flat = p.reshape(p.shape[0], 1, 2 * SSD_HEADS).astype(F32)
    return jnp.pad(flat, ((0, 0), (0, 0), (DT_LANE0, 128 - DT_LANE0 - 2 * SSD_HEADS)))


def kernel(x, meta_tokens, rel_bias, final_norm_w, norm_w, w_in, w_out, gla_wa2, gla_ba, gla_norm_w,
           diff_lambda, diff_norm_w, conv_w, conv_b, ssd_A_log, ssd_dt_bias, ssd_D, ssd_norm_w):
    B, S, D = x.shape
    assert D == D_MODEL and S % BLK == 0
    Lp = TOK0 + S
    T = 384 if Lp % 384 == 0 else 128

    h = jnp.concatenate([
        jnp.zeros((B, PAD, D), x.dtype),
        jnp.broadcast_to(meta_tokens[None].astype(x.dtype), (B, N_META, D)),
        x], axis=1)

    band = _bias_band(rel_bias, T)
    bmax = jnp.broadcast_to((jnp.max(rel_bias.astype(F32), axis=0) * LOG2E)[:, None, None], (DIFF_HEADS, 1, 128))
    E = _expansion()
    lam_init = np.array([0.8 - 0.6 * math.exp(-0.3 * l) for l in range(DEPTH)], np.float32)
    lcoef = np.zeros((DEPTH, 1, 128), np.float32)
    lcoef[:, 0, 0] = lam_init
    lcoef[:, 0, 1] = 1.0 - lam_init

    kd = GLA_HEADS * GLA_DK
    wa_p = jnp.zeros((DEPTH, 2, 128, kd), F32)
    wa_p = wa_p.at[:, 0, 0:GLA_RANK].set(gla_wa2[:, 0].astype(F32))
    wa_p = wa_p.at[:, 1, GLA_RANK:2 * GLA_RANK].set(gla_wa2[:, 1].astype(F32))
    dtb_r = _misc_row(ssd_dt_bias)
    al_r = _misc_row(ssd_A_log)
    layers = dict(
        norm_w=norm_w,
        w_in=_permute_w_in(w_in).astype(BF16),
        w_out=w_out.astype(BF16),
        wa_p=wa_p,
        ba=gla_ba.reshape(DEPTH, 2, 1, kd).astype(F32),
        gla_nw=gla_norm_w.reshape(DEPTH, 1, GLA_DV).astype(F32),
        lam=diff_lambda.astype(F32),
        lcoef=jnp.asarray(lcoef),
        diff_nw=diff_norm_w.reshape(DEPTH, 1, DIFF_DV).astype(F32),
        conv_w=conv_w.astype(F32),
        conv_b=conv_b.astype(F32),
        dtb_r=dtb_r,
        al_r=al_r,
        dskip=jnp.repeat(ssd_D.astype(F32), SSD_HEADDIM, axis=-1).reshape(DEPTH, 1, SSD_WIDTH),
        ssd_nw=ssd_norm_w.reshape(DEPTH, 1, SSD_WIDTH).astype(F32),
    )

    def layer(h, p):
        proj, misc = _inproj(h, p["norm_w"], p["w_in"])
        o_gla = _gla(proj, misc, p["wa_p"], p["ba"], p["gla_nw"])
        o_diff = _attn(proj, band, bmax, p["lam"], p["lcoef"], p["diff_nw"], T)
        xc = _conv(proj, p["conv_w"], p["conv_b"])
        o_ssd = _ssd(proj, misc, xc, p["dtb_r"], p["al_r"], E, p["dskip"], p["ssd_nw"])
        return _outproj(h, o_gla, o_diff, o_ssd, p["w_out"]), None

    h, _ = lax.scan(layer, h, layers)
    return _final_norm(h, final_norm_w)
```

```python
import functools
import math

import numpy as np
import jax
import jax.numpy as jnp
from jax import lax
from jax.experimental import pallas as pl
from jax.experimental.pallas import tpu as pltpu

F32 = jnp.float32
BF16 = jnp.bfloat16

D_MODEL = 2048
DEPTH = 4
N_META = 16
EPS = 1e-6
GLA_HEADS = 4
GLA_DK = 64
GLA_DV = 128
GLA_WIDTH = 512
GLA_RANK = 16
GLA_TAU = 16.0
GLA_CHUNK = 64
DIFF_HEADS = 4
DIFF_DQK = 64
DIFF_DV = 128
DIFF_WIDTH = 512
N_BUCKETS = 32
MAX_DISTANCE = 128
SSD_WIDTH = 1024
SSD_HEADDIM = 64
SSD_HEADS = 16
SSD_GROUPS = 2
SSD_STATE = 128
SSD_CONV = 5
SSD_CHUNK = 128
SSD_CONV_DIM = SSD_WIDTH + 2 * SSD_GROUPS * SSD_STATE

BLK = 128
TOK0 = BLK
PAD = TOK0 - N_META

C_GQ, C_GK, C_GV, C_GG = 0, 256, 512, 1024
C_DQ, C_DK, C_DV = 1536, 2048, 2560
C_Z, C_DG, C_XBC, C_MISC = 3072, 4096, 4608, 6144
N_PROJ = 6272
DT_LANE0 = 2 * GLA_RANK

VMEM_LIMIT = 56 * 1024 * 1024
LOG2E = 1.4426950408889634


def _params(sem, limit=VMEM_LIMIT):
    return pltpu.CompilerParams(dimension_semantics=sem, vmem_limit_bytes=limit)


def _silu(x):
    return x / (1.0 + jnp.exp(-x))


def _softplus(x):
    return jnp.maximum(x, 0.0) + jnp.log(1.0 + jnp.exp(-jnp.abs(x)))


def _log_sigmoid(x):
    return jnp.minimum(x, 0.0) - jnp.log(1.0 + jnp.exp(-jnp.abs(x)))


def _bdot(a, b):
    return jnp.dot(a.astype(BF16), b.astype(BF16), preferred_element_type=F32)


def _split3(x):
    hi = x.astype(BF16)
    r = x - hi.astype(F32)
    mid = r.astype(BF16)
    lo = (r - mid.astype(F32)).astype(BF16)
    return hi, mid, lo


def _dot_l3(a, b_exact):
    hi, mid, lo = _split3(a)
    d = functools.partial(jnp.dot, preferred_element_type=F32)
    return d(hi, b_exact) + d(mid, b_exact) + d(lo, b_exact)


def _dot_r3(a_exact, b):
    hi, mid, lo = _split3(b)
    d = functools.partial(jnp.dot, preferred_element_type=F32)
    return d(a_exact, hi) + d(a_exact, mid) + d(a_exact, lo)


def _dot_22(a, b):
    ah = a.astype(BF16)
    al = (a - ah.astype(F32)).astype(BF16)
    bh = b.astype(BF16)
    bl = (b - bh.astype(F32)).astype(BF16)
    d = functools.partial(jnp.dot, preferred_element_type=F32)
    return d(ah, bh) + d(ah, bl) + d(al, bh)


def _iota(shape, dim):
    return lax.broadcasted_iota(jnp.int32, shape, dim)


def _rms(x, w):
    return x * lax.rsqrt(jnp.mean(x * x, axis=-1, keepdims=True) + EPS) * w


def _inproj_kernel(h_ref, nw_ref, w_ref, wm_ref, o_ref, om_ref, u_scr, *, tm):
    @pl.when(pl.program_id(2) == 0)
    def _():
        x = h_ref[...]
        y = _rms(x, nw_ref[...])
        row = pl.program_id(1) * tm + _iota((tm, 1), 0)
        u_scr[...] = jnp.where(row >= PAD, y, 0.0).astype(BF16)
        om_ref[...] = jnp.dot(u_scr[...], wm_ref[...], preferred_element_type=F32)

    o_ref[...] = jnp.dot(u_scr[...], w_ref[...], preferred_element_type=F32).astype(o_ref.dtype)


def _inproj(h, norm_w, w_p):
    B, Lp, D = h.shape
    tm = Lp // 4
    tn = C_MISC // 4
    return pl.pallas_call(
        functools.partial(_inproj_kernel, tm=tm),
        grid=(B, Lp // tm, C_MISC // tn),
        in_specs=[
            pl.BlockSpec((None, tm, D), lambda b, i, j: (b, i, 0)),
            pl.BlockSpec((1, D), lambda b, i, j: (0, 0)),
            pl.BlockSpec((D, tn), lambda b, i, j: (0, j)),
            pl.BlockSpec((D, 128), lambda b, i, j: (0, C_MISC // 128)),
        ],
        out_specs=[
            pl.BlockSpec((None, tm, tn), lambda b, i, j: (b, i, j)),
            pl.BlockSpec((None, tm, 128), lambda b, i, j: (b, i, 0)),
        ],
        out_shape=[jax.ShapeDtypeStruct((B, Lp, C_MISC), BF16),
                   jax.ShapeDtypeStruct((B, Lp, 128), F32)],
        scratch_shapes=[pltpu.VMEM((tm, D), BF16)],
        compiler_params=_params(("arbitrary", "arbitrary", "arbitrary")),
        name="inproj",
    )(h, norm_w.reshape(1, D), w_p, w_p)


def _outproj_kernel(h_ref, a_ref, b_ref, c_ref, w_ref, o_ref):
    d = functools.partial(jnp.dot, preferred_element_type=F32)
    acc = d(a_ref[...], w_ref[0:GLA_WIDTH, :])
    acc += d(b_ref[...], w_ref[GLA_WIDTH:GLA_WIDTH + DIFF_WIDTH, :])
    acc += d(c_ref[...], w_ref[GLA_WIDTH + DIFF_WIDTH:, :])
    o_ref[...] = h_ref[...] + acc


def _outproj(h, o_gla, o_diff, o_ssd, w_out):
    B, Lp, D = h.shape
    tm = Lp // 8 if (Lp // 8) % 16 == 0 else Lp // 4
    tn = D
    return pl.pallas_call(
        _outproj_kernel,
        grid=(B, Lp // tm, D // tn),
        in_specs=[
            pl.BlockSpec((None, tm, tn), lambda b, i, j: (b, i, j)),
            pl.BlockSpec((None, tm, GLA_WIDTH), lambda b, i, j: (b, i, 0)),
            pl.BlockSpec((None, tm, DIFF_WIDTH), lambda b, i, j: (b, i, 0)),
            pl.BlockSpec((None, tm, SSD_WIDTH), lambda b, i, j: (b, i, 0)),
            pl.BlockSpec((D, tn), lambda b, i, j: (0, j)),
        ],
        out_specs=pl.BlockSpec((None, tm, tn), lambda b, i, j: (b, i, j)),
        out_shape=jax.ShapeDtypeStruct((B, Lp, D), F32),
        input_output_aliases={0: 0},
        compiler_params=_params(("arbitrary", "arbitrary", "arbitrary")),
        name="outproj",
    )(h, o_gla, o_diff, o_ssd, w_out)


def _final_kernel(h_ref, w_ref, o_ref):
    o_ref[...] = _rms(h_ref[...], w_ref[...])


def _final_norm(h, w):
    B, Lp, D = h.shape
    S = Lp - TOK0
    rb = 512 if S % 512 == 0 else BLK
    return pl.pallas_call(
        _final_kernel,
        grid=(B, S // rb),
        in_specs=[
            pl.BlockSpec((pl.Element(rb), pl.Element(D)),
                         lambda b, i: (pl.multiple_of(b * Lp + TOK0 + i * rb, BLK), 0)),
            pl.BlockSpec((1, D), lambda b, i: (0, 0)),
        ],
        out_specs=pl.BlockSpec((None, rb, D), lambda b, i: (b, i, 0)),
        out_shape=jax.ShapeDtypeStruct((B, S, D), F32),
        compiler_params=_params(("arbitrary", "arbitrary")),
        name="final_norm",
    )(h.reshape(B * Lp, D), w.reshape(1, D))


def _round_robin(gens):
    alive = list(gens)
    while alive:
        for g in list(alive):
            try:
                next(g)
            except StopIteration:
                alive.remove(g)


def _gla_block(bi, q_ref, k_ref, v_ref, gate_ref, misc_ref, wa_ref, ba_ref, nw_ref, o_ref,
               of_scr, s_scr, *, d, blk):
    C = GLA_CHUNK
    row = _iota((BLK, 1), 0)
    valid = (blk * BLK + row) >= PAD
    x = _dot_22(misc_ref[bi], wa_ref[d]) + ba_ref[d]
    yield
    g = jnp.where(valid, _log_sigmoid(x) * (1.0 / GLA_TAU), 0.0)

    r = _iota((BLK, BLK), 0)
    c = _iota((BLK, BLK), 1)
    same = (r >= C) == (c >= C)
    tri = (same & (c <= r)) if d == 0 else (same & (c >= r))
    b = _dot_r3(jnp.where(tri, 1.0, 0.0).astype(BF16), g)
    yield
    first, second = (C - 1, 2 * C - 1) if d == 0 else (0, C)

    k = k_ref[bi].astype(F32)
    q_in = q_ref[bi].astype(F32) * (GLA_DK ** -0.5) * jnp.exp(b)
    kT = k.T
    bT = b.T
    lane = _iota((1, BLK), 1)
    blastT = jnp.where(lane < C, bT[:, first:first + 1], bT[:, second:second + 1])
    kinT = (kT * jnp.exp(-bT)).astype(BF16)
    koutT = kT * jnp.exp(blastT - bT)
    vb = v_ref[bi]
    yield

    lane_k = _iota((1, GLA_HEADS * GLA_DK), 1)
    qm = [jnp.where((lane_k >= h * GLA_DK) & (lane_k < (h + 1) * GLA_DK), q_in, 0.0).astype(BF16)
          for h in range(GLA_HEADS)]
    att = [jnp.where(tri, jnp.dot(qm[h], kinT, preferred_element_type=F32), 0.0).astype(BF16)
           for h in range(GLA_HEADS)]
    yield
    o_intra = [jnp.dot(att[h], vb[:, h * GLA_DV:(h + 1) * GLA_DV], preferred_element_type=F32)
               for h in range(GLA_HEADS)]
    yield

    S = s_scr[bi]
    o_inter = [[None, None] for _ in range(GLA_HEADS)]
    for cc in ((0, 1) if d == 0 else (1, 0)):
        Sb = S.astype(BF16)
        for h in range(GLA_HEADS):
            o_inter[h][cc] = jnp.dot(qm[h][cc * C:(cc + 1) * C, :], Sb, preferred_element_type=F32)
        tot = first if cc == 0 else second
        dec = jnp.exp(bT[:, tot:tot + 1])
        kc = jnp.where((lane >= cc * C) & (lane < (cc + 1) * C), koutT, 0.0).astype(BF16)
        upd = [jnp.dot(kc[h * GLA_DK:(h + 1) * GLA_DK, :], vb[:, h * GLA_DV:(h + 1) * GLA_DV],
                       preferred_element_type=F32) for h in range(GLA_HEADS)]
        S = S * dec + jnp.concatenate(upd, axis=0)
        yield
    s_scr[bi] = S

    o = jnp.concatenate(
        [o_intra[h] + jnp.concatenate(o_inter[h], axis=0) for h in range(GLA_HEADS)], axis=1)
    if d == 0:
        of_scr[bi, blk] = o
    else:
        o = o + of_scr[bi, blk]
        gate = gate_ref[bi].astype(F32)
        nw = nw_ref[...]
        outs = []
        for h in range(GLA_HEADS):
            sl = slice(h * GLA_DV, (h + 1) * GLA_DV)
            outs.append(_rms(o[:, sl], nw) * _silu(gate[:, sl]))
        o_ref[bi] = jnp.concatenate(outs, axis=1).astype(o_ref.dtype)


def _gla_kernel(*refs, nblk, nb):
    s_scr = refs[-1]
    s = pl.program_id(1)

    @pl.when((s == 0) | (s == nblk))
    def _():
        s_scr[...] = jnp.zeros_like(s_scr)

    @pl.when(s < nblk)
    def _():
        _round_robin([_gla_block(bi, *refs, d=0, blk=s) for bi in range(nb)])

    @pl.when(s >= nblk)
    def _():
        _round_robin([_gla_block(bi, *refs, d=1, blk=2 * nblk - 1 - s) for bi in range(nb)])


def _scan_blk(s, nblk):
    return jnp.where(s < nblk, s, 2 * nblk - 1 - s)


def _scan_out_blk(s, nblk):
    return jnp.where(s < nblk, nblk - 1, 2 * nblk - 1 - s)


def _batch_slots(B, want):
    nb = want
    while B % nb:
        nb -= 1
    return nb


def _gla(proj, misc, wa_p, ba, nw):
    B, Lp, _ = proj.shape
    nblk = Lp // BLK
    nb = _batch_slots(B, 4)
    kd = GLA_HEADS * GLA_DK
    im = lambda col: (lambda b, s: (b, _scan_blk(s, nblk), col))
    return pl.pallas_call(
        functools.partial(_gla_kernel, nblk=nblk, nb=nb),
        grid=(B // nb, 2 * nblk),
        in_specs=[
            pl.BlockSpec((nb, BLK, kd), im(C_GQ // kd)),
            pl.BlockSpec((nb, BLK, kd), im(C_GK // kd)),
            pl.BlockSpec((nb, BLK, GLA_WIDTH), im(C_GV // GLA_WIDTH)),
            pl.BlockSpec((nb, BLK, GLA_WIDTH), im(C_GG // GLA_WIDTH)),
            pl.BlockSpec((nb, BLK, 128), im(0)),
            pl.BlockSpec((2, 128, kd), lambda b, s: (0, 0, 0)),
            pl.BlockSpec((2, 1, kd), lambda b, s: (0, 0, 0)),
            pl.BlockSpec((1, GLA_DV), lambda b, s: (0, 0)),
        ],
        out_specs=pl.BlockSpec((nb, BLK, GLA_WIDTH), lambda b, s: (b, _scan_out_blk(s, nblk), 0)),
        out_shape=jax.ShapeDtypeStruct((B, Lp, GLA_WIDTH), BF16),
        scratch_shapes=[pltpu.VMEM((nb, nblk, BLK, GLA_WIDTH), F32), pltpu.VMEM((nb, kd, GLA_DV), F32)],
        compiler_params=_params(("arbitrary", "arbitrary")),
        name="gla",
    )(proj, proj, proj, proj, misc, wa_p, ba, nw)


ATTN_BAND_TILES = 3
ATTN_L_FLOOR = 2.0 ** -60


def _q_halves(q_ref, bi):
    lane = _iota((1, 2 * DIFF_DQK), 1)
    q = q_ref[bi].astype(F32) * (DIFF_DQK ** -0.5 * LOG2E)
    return [jnp.where((lane >= c * DIFF_DQK) & (lane < (c + 1) * DIFF_DQK), q, 0.0).astype(BF16)
            for c in range(2)]


def _attn_slot(bi, i, q_ref, v_ref, band_ref, bmax_ref, kT_scr, kn_scr, l_scr, acc_scr, *, T, nk):
    sub = T // 128
    qb = _q_halves(q_ref, bi)
    qstack = jnp.concatenate(qb, axis=0)
    kn = kn_scr[bi]
    m = jnp.concatenate([
        jnp.sqrt(jnp.sum(x.astype(F32) ** 2, axis=-1, keepdims=True)) * kn[c:c + 1, :] + bmax_ref[...]
        for c, x in enumerate(qb)], axis=0)
    off_l = m - band_ref[0, 0:1, 0:128]
    off_r = m - band_ref[4, 0:1, 0:128]
    b0 = jnp.clip(i - 1, 0, nk - ATTN_BAND_TILES)
    padrow = jnp.where(_iota((1, 128), 1) < PAD, 3.0e4, 0.0)
    l_scr[bi] = jnp.zeros(l_scr.shape[1:], F32)
    acc_scr[bi] = jnp.zeros(acc_scr.shape[1:], F32)

    def sub_tile(j, t, off, bidx):
        cols = slice(t * 128, (t + 1) * 128)
        x = jnp.dot(qstack, kT_scr[bi, j, :, cols], preferred_element_type=F32) - off
        if bidx is not None:
            bt = band_ref[bidx, :, cols]
            x = x + jnp.concatenate([bt, bt], axis=0)
        if t == 0:
            x = x - jnp.where(j == 0, padrow, 0.0)
        e = jnp.exp2(x)
        l_scr[bi] += e
        vt = v_ref[bi, pl.ds(pl.multiple_of(j * T + t * 128, 128), 128), :]
        acc_scr[bi] += jnp.dot(e.astype(BF16), vt, preferred_element_type=F32)

    for jj in range(nk - ATTN_BAND_TILES):
        j = jj + jnp.where(jj >= b0, ATTN_BAND_TILES, 0)
        off = jnp.where(j < i, off_l, off_r)
        for t in range(sub):
            sub_tile(j, t, off, None)
            yield
    for b in range(ATTN_BAND_TILES):
        j = b0 + b
        for t in range(sub):
            sub_tile(j, t, m, j - i + 2)
            yield


def _attn_exact(bi, i, q_ref, v_ref, band_ref, kT_scr, l_scr, acc_scr, *, T, nk):
    sub = T // 128
    qb = _q_halves(q_ref, bi)
    col = _iota((1, T), 1)

    def fold(x, op):
        r = x[:, 0:128]
        for t in range(1, sub):
            r = op(r, x[:, t * 128:(t + 1) * 128])
        return r

    for c in range(2):
        for r0 in range(0, T, 128):
            rows = slice(r0, r0 + 128)
            qc = qb[c][rows]

            def scores(j):
                s = jnp.dot(qc, kT_scr[bi, j], preferred_element_type=F32)
                s = s + band_ref[jnp.clip(j - i, -2, 2) + 2, rows, :]
                return jnp.where((j == 0) & (col < PAD), -1e30, s)

            m_run = lax.fori_loop(0, nk, lambda j, mr: jnp.maximum(mr, fold(scores(j), jnp.maximum)),
                                  jnp.full((128, 128), -jnp.inf, F32))
            m = jnp.max(m_run, axis=-1, keepdims=True)

            def body(j, carry):
                l_run, acc = carry
                e = jnp.exp2(scores(j) - m)
                vt = v_ref[bi, pl.ds(pl.multiple_of(j * T, 128), T), :]
                return l_run + fold(e, jnp.add), acc + jnp.dot(e.astype(BF16), vt, preferred_element_type=F32)

            l_run, acc = lax.fori_loop(0, nk, body, (jnp.zeros((128, 128), F32), jnp.zeros((128, 128), F32)))
            l_scr[bi, c * T + r0:c * T + r0 + 128, :] = l_run
            acc_scr[bi, c * T + r0:c * T + r0 + 128, :] = acc


def _attn_kernel(q_ref, k_ref, v_ref, g_ref, band_ref, bmax_ref, lam_ref, lc_ref, nw_ref, o_ref,
                 kT_scr, kn_scr, l_scr, acc_scr, *, nb, T, nk):
    i = pl.program_id(2)

    @pl.when(i == 0)
    def _():
        lane = _iota((1, 2 * DIFF_DQK), 1)
        for bi in range(nb):
            for j in range(nk):
                kT_scr[bi, j] = k_ref[bi, j * T:(j + 1) * T, :].astype(F32).T.astype(BF16)
            k2 = k_ref[bi].astype(F32) ** 2
            for c in range(2):
                half = (lane >= c * DIFF_DQK) & (lane < (c + 1) * DIFF_DQK)
                n2 = jnp.max(jnp.sum(jnp.where(half, k2, 0.0), axis=-1, keepdims=True), axis=0, keepdims=True)
                kn_scr[bi, c:c + 1, :] = jnp.broadcast_to(jnp.sqrt(n2), (1, 128))

    _round_robin([
        _attn_slot(bi, i, q_ref, v_ref, band_ref, bmax_ref, kT_scr, kn_scr, l_scr, acc_scr, T=T, nk=nk)
        for bi in range(nb)])

    lp = lam_ref[...]
    lc = lc_ref[...]
    lam = (jnp.exp(jnp.sum(lp[0:1] * lp[1:2], axis=-1, keepdims=True))
           - jnp.exp(jnp.sum(lp[2:3] * lp[3:4], axis=-1, keepdims=True)) + lc[:, 0:1])
    for bi in range(nb):
        @pl.when(jnp.min(jnp.sum(l_scr[bi], axis=-1, keepdims=True)) < ATTN_L_FLOOR)
        def _():
            _attn_exact(bi, i, q_ref, v_ref, band_ref, kT_scr, l_scr, acc_scr, T=T, nk=nk)

        out = acc_scr[bi] / jnp.sum(l_scr[bi], axis=-1, keepdims=True)
        o = out[0:T] - lam * out[T:2 * T]
        y = _rms(o, nw_ref[...]) * lc[:, 1:2]
        o_ref[bi] = (y * _silu(g_ref[bi].astype(F32))).astype(o_ref.dtype)


def _attn(proj, band, bmax, diff_lambda, lcoef, nw, T):
    B, Lp, _ = proj.shape
    nk = Lp // T
    H = DIFF_HEADS
    assert nk >= ATTN_BAND_TILES
    nb = _batch_slots(B, 2)
    return pl.pallas_call(
        functools.partial(_attn_kernel, nb=nb, T=T, nk=nk),
        grid=(B // nb, H, nk),
        in_specs=[
            pl.BlockSpec((nb, T, 128), lambda b, h, i: (b, i, C_DQ // 128 + h)),
            pl.BlockSpec((nb, Lp, 128), lambda b, h, i: (b, 0, C_DK // 128 + h)),
            pl.BlockSpec((nb, Lp, 128), lambda b, h, i: (b, 0, C_DV // 128 + h)),
            pl.BlockSpec((nb, T, 128), lambda b, h, i: (b, i, C_DG // 128 + h)),
            pl.BlockSpec((None, 5, T, T), lambda b, h, i: (h, 0, 0, 0)),
            pl.BlockSpec((None, 1, 128), lambda b, h, i: (h, 0, 0)),
            pl.BlockSpec((4, DIFF_DQK), lambda b, h, i: (0, 0)),
            pl.BlockSpec((1, 128), lambda b, h, i: (0, 0)),
            pl.BlockSpec((1, DIFF_DV), lambda b, h, i: (0, 0)),
        ],
        out_specs=pl.BlockSpec((nb, T, 128), lambda b, h, i: (b, i, h)),
        out_shape=jax.ShapeDtypeStruct((B, Lp, DIFF_WIDTH), BF16),
        scratch_shapes=[
            pltpu.VMEM((nb, nk, 128, T), BF16),
            pltpu.VMEM((nb, 8, 128), F32),
            pltpu.VMEM((nb, 2 * T, 128), F32),
            pltpu.VMEM((nb, 2 * T, 128), F32),
        ],
        compiler_params=_params(("arbitrary", "arbitrary", "arbitrary")),
        name="diff_attn",
    )(proj, proj, proj, proj, band, bmax, diff_lambda, lcoef, nw)


def _bucket_table(T):
    nb = N_BUCKETS // 2
    max_exact = nb // 2
    rel = np.arange(6 * T) - (3 * T - 1)
    ret = np.where(rel > 0, nb, 0)
    n = np.abs(rel)
    nf = np.maximum(n, 1).astype(np.float64)
    large = max_exact + (np.log(nf / max_exact) / math.log(MAX_DISTANCE / max_exact)
                         * (nb - max_exact)).astype(np.int32)
    large = np.minimum(large, nb - 1)
    return (ret + np.where(n < max_exact, n, large)).astype(np.int32)


def _bias_band(rel_bias, T):
    n = 6 * T
    v = (rel_bias.astype(F32)[_bucket_table(T)] * LOG2E).T
    rows = jnp.tile(v, (1, T))[:, :T * (n - 1)].reshape(DIFF_HEADS, T, n - 1)
    band = rows[:, :, T - 1:n - 1].reshape(DIFF_HEADS, T, 5, T)
    return jnp.transpose(band, (0, 2, 1, 3))


def _conv_kernel(x_ref, w_ref, b_ref, o_ref):
    x = x_ref[...].astype(F32)
    w = w_ref[...]
    n = x.shape[0]
    half = (SSD_CONV - 1) // 2
    acc = x * w[half:half + 1, :] + b_ref[...]
    for kk in range(SSD_CONV):
        if kk != half:
            acc = acc + pltpu.roll(x, (half - kk) % n, 0) * w[kk:kk + 1, :]
    o_ref[...] = _silu(acc).astype(o_ref.dtype)


def _conv(proj, conv_w, conv_b):
    B, Lp, _ = proj.shape
    tn = 256
    return pl.pallas_call(
        _conv_kernel,
        grid=(B, SSD_CONV_DIM // tn),
        in_specs=[
            pl.BlockSpec((None, Lp, tn), lambda b, j: (b, 0, C_XBC // tn + j)),
            pl.BlockSpec((SSD_CONV, tn), lambda b, j: (0, j)),
            pl.BlockSpec((1, tn), lambda b, j: (0, j)),
        ],
        out_specs=pl.BlockSpec((None, Lp, tn), lambda b, j: (b, 0, j)),
        out_shape=jax.ShapeDtypeStruct((B, Lp, SSD_CONV_DIM), BF16),
        compiler_params=_params(("arbitrary", "arbitrary")),
        name="ssd_conv",
    )(proj, conv_w, conv_b.reshape(1, SSD_CONV_DIM))


def _ssd_block(bi, xc_ref, z_ref, misc_ref, dtb_r_ref, al_r_ref, e_ref, dskip_ref,
               nw_ref, o_ref, yf_scr, s_scr, *, d, blk):
    G, N, P = SSD_GROUPS, SSD_STATE, SSD_HEADDIM
    R = SSD_HEADS // G
    W = R * P
    row = _iota((BLK, 1), 0)
    lane = _iota((1, BLK), 1)
    dt = jnp.where((blk * BLK + row) >= PAD, _softplus(misc_ref[bi] + dtb_r_ref[...]), 0.0)
    a = dt * (-jnp.exp(al_r_ref[...]))
    aT = a.T

    r = _iota((BLK, BLK), 0)
    c = _iota((BLK, BLK), 1)
    tri = (c <= r) if d == 0 else (c >= r)
    triT = (r <= c) if d == 0 else (r >= c)
    cum = _dot_r3(jnp.where(tri, 1.0, 0.0).astype(BF16), a)
    cumT = _dot_l3(aT, jnp.where(triT, 1.0, 0.0).astype(BF16))
    yield

    lo = DT_LANE0 + SSD_HEADS * d
    hm = (lane >= lo) & (lane < lo + SSD_HEADS)
    last = BLK - 1 if d == 0 else 0
    tot = cum[last:last + 1, :]
    ecum = jnp.where(hm, jnp.exp(cum), 0.0)
    ecum_hi = ecum.astype(BF16)
    stack = jnp.concatenate([
        jnp.where(hm, dt, 0.0).astype(BF16),
        jnp.where(hm, jnp.exp(tot - cum), 0.0).astype(BF16),
        ecum_hi,
        (ecum - ecum_hi.astype(F32)).astype(BF16)], axis=0)
    yield

    for g in range(G):
        cols = slice(g * W, (g + 1) * W)
        ex = jnp.dot(stack, e_ref[d, :, cols], preferred_element_type=F32)
        dt_x = ex[0:BLK]
        toend_x = ex[BLK:2 * BLK]
        ecum_x = ex[2 * BLK:3 * BLK] + ex[3 * BLK:4 * BLK]
        etot_x = ecum_x[last:last + 1, :]
        xs = xc_ref[bi, :, cols].astype(F32)
        xd = xs * dt_x
        xdb = xd.astype(BF16)
        xdw = (xd * toend_x).astype(BF16)
        BgT = xc_ref[bi, :, SSD_WIDTH + g * N:SSD_WIDTH + (g + 1) * N].astype(F32).T.astype(BF16)
        Cg = xc_ref[bi, :, SSD_WIDTH + (G + g) * N:SSD_WIDTH + (G + g + 1) * N]
        CB = jnp.dot(Cg, BgT, preferred_element_type=F32)
        S = s_scr[bi, g]
        y_off = jnp.dot(Cg, S.astype(BF16), preferred_element_type=F32) * ecum_x
        s_scr[bi, g] = S * etot_x + jnp.dot(BgT, xdw, preferred_element_type=F32)
        yield
        y_pairs = []
        for pr in range(R // 2):
            h0 = g * R + 2 * pr
            pcols = slice(2 * pr * P, (2 * pr + 2) * P)
            xpair = xdb[:, pcols]
            acc = y_off[:, pcols]
            for hh in range(2):
                li = lo + h0 + hh
                seg = cum[:, li:li + 1] - cumT[li:li + 1, :]
                dec = jnp.exp(jnp.where(tri, seg, -jnp.inf))
                sc = (CB * dec).astype(BF16)
                xm = jnp.where((lane >= hh * P) & (lane < (hh + 1) * P), xpair, 0.0)
                acc = acc + jnp.dot(sc, xm, preferred_element_type=F32)
            if d == 0:
                yf_scr[bi, blk, :, g * W + 2 * pr * P:g * W + (2 * pr + 2) * P] = acc.astype(yf_scr.dtype)
            else:
                y_pairs.append(acc)
            yield
        if d == 1:
            y = jnp.concatenate(y_pairs, axis=1)
            y = y + yf_scr[bi, blk, :, cols].astype(F32) + xs * dskip_ref[:, cols]
            y = y * _silu(z_ref[bi, :, cols].astype(F32))
            o_ref[bi, :, cols] = _rms(y, nw_ref[:, cols]).astype(o_ref.dtype)


def _ssd_kernel(*refs, nblk, nb):
    s_scr = refs[-1]
    s = pl.program_id(1)

    @pl.when((s == 0) | (s == nblk))
    def _():
        s_scr[...] = jnp.zeros_like(s_scr)

    @pl.when(s < nblk)
    def _():
        _round_robin([_ssd_block(bi, *refs, d=0, blk=s) for bi in range(nb)])

    @pl.when(s >= nblk)
    def _():
        _round_robin([_ssd_block(bi, *refs, d=1, blk=2 * nblk - 1 - s) for bi in range(nb)])


def _ssd(proj, misc, xc, dtb_r, al_r, E, dskip, nw):
    B, Lp, _ = proj.shape
    nblk = Lp // BLK
    nb = _batch_slots(B, 4)
    const2 = lambda b, s: (0, 0)
    return pl.pallas_call(
        functools.partial(_ssd_kernel, nblk=nblk, nb=nb),
        grid=(B // nb, 2 * nblk),
        in_specs=[
            pl.BlockSpec((nb, BLK, SSD_CONV_DIM), lambda b, s: (b, _scan_blk(s, nblk), 0)),
            pl.BlockSpec((nb, BLK, SSD_WIDTH), lambda b, s: (b, _scan_blk(s, nblk), C_Z // SSD_WIDTH)),
            pl.BlockSpec((nb, BLK, 128), lambda b, s: (b, _scan_blk(s, nblk), 0)),
            pl.BlockSpec((1, 128), const2),
            pl.BlockSpec((1, 128), const2),
            pl.BlockSpec((2, 128, SSD_WIDTH), lambda b, s: (0, 0, 0)),
            pl.BlockSpec((1, SSD_WIDTH), const2),
            pl.BlockSpec((1, SSD_WIDTH), const2),
        ],
        out_specs=pl.BlockSpec((nb, BLK, SSD_WIDTH), lambda b, s: (b, _scan_out_blk(s, nblk), 0)),
        out_shape=jax.ShapeDtypeStruct((B, Lp, SSD_WIDTH), BF16),
        scratch_shapes=[
            pltpu.VMEM((nb, nblk, BLK, SSD_WIDTH), BF16),
            pltpu.VMEM((nb, SSD_GROUPS, SSD_STATE, SSD_WIDTH // SSD_GROUPS), F32),
        ],
        compiler_params=_params(("arbitrary", "arbitrary")),
        name="ssd_scan",
    )(xc, proj, misc, dtb_r, al_r, E, dskip, nw)


def _permute_w_in(w):
    o = np.cumsum([0, 256, 256, 512, 512, 32, 512, 512, 512, 512, 1024, 1536, 32])
    gq, gk, gv, gg, gcode, dq, dk, dv, dg, z, xbc, dt, end = [int(v) for v in o]
    pieces = [w[..., gq:gcode], w[..., dq:dg], w[..., z:xbc], w[..., dg:z], w[..., xbc:dt],
              w[..., gcode:dq], w[..., dt:end],
              jnp.zeros(w.shape[:-1] + (N_PROJ - C_MISC - 64,), w.dtype)]
    return jnp.concatenate(pieces, axis=-1)


def _expansion():
    E = np.zeros((2, 128, SSD_WIDTH), np.float32)
    for d in range(2):
        for h in range(SSD_HEADS):
            E[d, DT_LANE0 + SSD_HEADS * d + h, h * SSD_HEADDIM:(h + 1) * SSD_HEADDIM] = 1.0
    return jnp.asarray(E, BF16)


def _misc_row(p):
    flat = p.reshape(p.shape[0], 1, 2 * SSD_HEADS).astype(F32)
    return jnp.pad(flat, ((0, 0), (0, 0), (DT_LANE0, 128 - DT_LANE0 - 2 * SSD_HEADS)))


def kernel(x, meta_tokens, rel_bias, final_norm_w, norm_w, w_in, w_out, gla_wa2, gla_ba, gla_norm_w,
           diff_lambda, diff_norm_w, conv_w, conv_b, ssd_A_log, ssd_dt_bias, ssd_D, ssd_norm_w):
    B, S, D = x.shape
    assert D == D_MODEL and S % BLK == 0
    Lp = TOK0 + S
    T = 384 if Lp % 384 == 0 else 128

    h = jnp.concatenate([
        jnp.zeros((B, PAD, D), x.dtype),
        jnp.broadcast_to(meta_tokens[None].astype(x.dtype), (B, N_META, D)),
        x], axis=1)

    band = _bias_band(rel_bias, T)
    bmax = jnp.broadcast_to((jnp.max(rel_bias.astype(F32), axis=0) * LOG2E)[:, None, None], (DIFF_HEADS, 1, 128))
    E = _expansion()
    lam_init = np.array([0.8 - 0.6 * math.exp(-0.3 * l) for l in range(DEPTH)], np.float32)
    lcoef = np.zeros((DEPTH, 1, 128), np.float32)
    lcoef[:, 0, 0] = lam_init
    lcoef[:, 0, 1] = 1.0 - lam_init

    kd = GLA_HEADS * GLA_DK
    wa_p = jnp.zeros((DEPTH, 2, 128, kd), F32)
    wa_p = wa_p.at[:, 0, 0:GLA_RANK].set(gla_wa2[:, 0].astype(F32))
    wa_p = wa_p.at[:, 1, GLA_RANK:2 * GLA_RANK].set(gla_wa2[:, 1].astype(F32))
    dtb_r = _misc_row(ssd_dt_bias)
    al_r = _misc_row(ssd_A_log)
    layers = dict(
        norm_w=norm_w,
        w_in=_permute_w_in(w_in).astype(BF16),
        w_out=w_out.astype(BF16),
        wa_p=wa_p,
        ba=gla_ba.reshape(DEPTH, 2, 1, kd).astype(F32),
        gla_nw=gla_norm_w.reshape(DEPTH, 1, GLA_DV).astype(F32),
        lam=diff_lambda.astype(F32),
        lcoef=jnp.asarray(lcoef),
        diff_nw=diff_norm_w.reshape(DEPTH, 1, DIFF_DV).astype(F32),
        conv_w=conv_w.astype(F32),
        conv_b=conv_b.astype(F32),
        dtb_r=dtb_r,
        al_r=al_r,
        dskip=jnp.repeat(ssd_D.astype(F32), SSD_HEADDIM, axis=-1).reshape(DEPTH, 1, SSD_WIDTH),
        ssd_nw=ssd_norm_w.reshape(DEPTH, 1, SSD_WIDTH).astype(F32),
    )

    def layer(h, p):
        proj, misc = _inproj(h, p["norm_w"], p["w_in"])
        o_gla = _gla(proj, misc, p["wa_p"], p["ba"], p["gla_nw"])
        o_diff = _attn(proj, band, bmax, p["lam"], p["lcoef"], p["diff_nw"], T)
        xc = _conv(proj, p["conv_w"], p["conv_b"])
        o_ssd = _ssd(proj, misc, xc, p["dtb_r"], p["al_r"], E, p["dskip"], p["ssd_nw"])
        return _outproj(h, o_gla, o_diff, o_ssd, p["w_out"]), None

    h, _ = lax.scan(layer, h, layers)
    return _final_norm(h, final_norm_w)
```

```python
import functools
import math

import numpy as np
import jax
import jax.numpy as jnp
from jax import lax
from jax.experimental import pallas as pl
from jax.experimental.pallas import tpu as pltpu

F32 = jnp.float32
BF16 = jnp.bfloat16

D_MODEL = 2048
DEPTH = 4
N_META = 16
EPS = 1e-6
GLA_HEADS = 4
GLA_DK = 64
GLA_DV = 128
GLA_WIDTH = 512
GLA_RANK = 16
GLA_TAU = 16.0
GLA_CHUNK = 64
DIFF_HEADS = 4
DIFF_DQK = 64
DIFF_DV = 128
DIFF_WIDTH = 512
N_BUCKETS = 32
MAX_DISTANCE = 128
SSD_WIDTH = 1024
SSD_HEADDIM = 64
SSD_HEADS = 16
SSD_GROUPS = 2
SSD_STATE = 128
SSD_CONV = 5
SSD_CHUNK = 128
SSD_CONV_DIM = SSD_WIDTH + 2 * SSD_GROUPS * SSD_STATE

BLK = SSD_CHUNK
assert BLK == 2 * GLA_CHUNK == 128
TOK0 = BLK
PAD = TOK0 - N_META

C_GQ, C_GK, C_GV, C_GG = 0, 256, 512, 1024
C_DQ, C_DK, C_DV = 1536, 2048, 2560
C_Z, C_DG, C_XBC, C_MISC = 3072, 4096, 4608, 6144
N_PROJ = 6272
DT_LANE0 = 2 * GLA_RANK

VMEM_LIMIT = 56 * 1024 * 1024
LOG2E = 1.4426950408889634


def _params(sem, limit=VMEM_LIMIT):
    return pltpu.CompilerParams(dimension_semantics=sem, vmem_limit_bytes=limit)


def _silu(x):
    return x / (1.0 + jnp.exp(-x))


def _softplus(x):
    return jnp.maximum(x, 0.0) + jnp.log(1.0 + jnp.exp(-jnp.abs(x)))


def _log_sigmoid(x):
    return jnp.minimum(x, 0.0) - jnp.log(1.0 + jnp.exp(-jnp.abs(x)))


def _split3(x):
    hi = x.astype(BF16)
    r = x - hi.astype(F32)
    mid = r.astype(BF16)
    lo = (r - mid.astype(F32)).astype(BF16)
    return hi, mid, lo


def _dot_l3(a, b_exact):
    hi, mid, lo = _split3(a)
    d = functools.partial(jnp.dot, preferred_element_type=F32)
    return d(hi, b_exact) + d(mid, b_exact) + d(lo, b_exact)


def _dot_r3(a_exact, b):
    hi, mid, lo = _split3(b)
    d = functools.partial(jnp.dot, preferred_element_type=F32)
    return d(a_exact, hi) + d(a_exact, mid) + d(a_exact, lo)


def _dot_22(a, b):
    ah = a.astype(BF16)
    al = (a - ah.astype(F32)).astype(BF16)
    bh = b.astype(BF16)
    bl = (b - bh.astype(F32)).astype(BF16)
    d = functools.partial(jnp.dot, preferred_element_type=F32)
    return d(ah, bh) + d(ah, bl) + d(al, bh)


def _iota(shape, dim):
    return lax.broadcasted_iota(jnp.int32, shape, dim)


def _rms(x, w):
    return x * lax.rsqrt(jnp.mean(x * x, axis=-1, keepdims=True) + EPS) * w


def _inproj_kernel(h_ref, nw_ref, w_ref, wm_ref, o_ref, om_ref, u_scr, *, tm):
    @pl.when(pl.program_id(2) == 0)
    def _():
        x = h_ref[...]
        y = _rms(x, nw_ref[...])
        row = pl.program_id(1) * tm + _iota((tm, 1), 0)
        u_scr[...] = jnp.where(row >= PAD, y, 0.0).astype(BF16)
        om_ref[...] = jnp.dot(u_scr[...], wm_ref[...], preferred_element_type=F32)

    o_ref[...] = jnp.dot(u_scr[...], w_ref[...], preferred_element_type=F32).astype(o_ref.dtype)


def _inproj(h, norm_w, w_p):
    B, Lp, D = h.shape
    tm = Lp // 4
    tn = C_MISC // 4
    return pl.pallas_call(
        functools.partial(_inproj_kernel, tm=tm),
        grid=(B, Lp // tm, C_MISC // tn),
        in_specs=[
            pl.BlockSpec((None, tm, D), lambda b, i, j: (b, i, 0)),
            pl.BlockSpec((1, D), lambda b, i, j: (0, 0)),
            pl.BlockSpec((D, tn), lambda b, i, j: (0, j)),
            pl.BlockSpec((D, 128), lambda b, i, j: (0, C_MISC // 128)),
        ],
        out_specs=[
            pl.BlockSpec((None, tm, tn), lambda b, i, j: (b, i, j)),
            pl.BlockSpec((None, tm, 128), lambda b, i, j: (b, i, 0)),
        ],
        out_shape=[jax.ShapeDtypeStruct((B, Lp, C_MISC), BF16),
                   jax.ShapeDtypeStruct((B, Lp, 128), F32)],
        scratch_shapes=[pltpu.VMEM((tm, D), BF16)],
        compiler_params=_params(("arbitrary", "arbitrary", "arbitrary")),
        name="inproj",
    )(h, norm_w.reshape(1, D), w_p, w_p)


def _outproj_kernel(h_ref, a_ref, b_ref, c_ref, w_ref, o_ref):
    d = functools.partial(jnp.dot, preferred_element_type=F32)
    acc = d(a_ref[...], w_ref[0:GLA_WIDTH, :])
    acc += d(b_ref[...], w_ref[GLA_WIDTH:GLA_WIDTH + DIFF_WIDTH, :])
    acc += d(c_ref[...], w_ref[GLA_WIDTH + DIFF_WIDTH:, :])
    o_ref[...] = h_ref[...] + acc


def _outproj(h, o_gla, o_diff, o_ssd, w_out):
    B, Lp, D = h.shape
    tm = Lp // 8 if (Lp // 8) % 16 == 0 else Lp // 4
    tn = D
    return pl.pallas_call(
        _outproj_kernel,
        grid=(B, Lp // tm, D // tn),
        in_specs=[
            pl.BlockSpec((None, tm, tn), lambda b, i, j: (b, i, j)),
            pl.BlockSpec((None, tm, GLA_WIDTH), lambda b, i, j: (b, i, 0)),
            pl.BlockSpec((None, tm, DIFF_WIDTH), lambda b, i, j: (b, i, 0)),
            pl.BlockSpec((None, tm, SSD_WIDTH), lambda b, i, j: (b, i, 0)),
            pl.BlockSpec((D, tn), lambda b, i, j: (0, j)),
        ],
        out_specs=pl.BlockSpec((None, tm, tn), lambda b, i, j: (b, i, j)),
        out_shape=jax.ShapeDtypeStruct((B, Lp, D), F32),
        input_output_aliases={0: 0},
        compiler_params=_params(("arbitrary", "arbitrary", "arbitrary")),
        name="outproj",
    )(h, o_gla, o_diff, o_ssd, w_out)


def _final_kernel(h_ref, w_ref, o_ref):
    o_ref[...] = _rms(h_ref[...], w_ref[...])


def _final_norm(h, w):
    B, Lp, D = h.shape
    S = Lp - TOK0
    rb = 512 if S % 512 == 0 else BLK
    return pl.pallas_call(
        _final_kernel,
        grid=(B, S // rb),
        in_specs=[
            pl.BlockSpec((pl.Element(rb), pl.Element(D)),
                         lambda b, i: (pl.multiple_of(b * Lp + TOK0 + i * rb, BLK), 0)),
            pl.BlockSpec((1, D), lambda b, i: (0, 0)),
        ],
        out_specs=pl.BlockSpec((None, rb, D), lambda b, i: (b, i, 0)),
        out_shape=jax.ShapeDtypeStruct((B, S, D), F32),
        compiler_params=_params(("arbitrary", "arbitrary")),
        name="final_norm",
    )(h.reshape(B * Lp, D), w.reshape(1, D))


def _round_robin(gens):
    alive = list(gens)
    while alive:
        for g in list(alive):
            try:
                next(g)
            except StopIteration:
                alive.remove(g)


def _gla_block(bi, q_ref, k_ref, v_ref, gate_ref, misc_ref, wa_ref, ba_ref, nw_ref, o_ref,
               of_scr, s_scr, *, d, blk):
    C = GLA_CHUNK
    row = _iota((BLK, 1), 0)
    valid = (blk * BLK + row) >= PAD
    x = _dot_22(misc_ref[bi], wa_ref[d]) + ba_ref[d]
    yield
    g = jnp.where(valid, _log_sigmoid(x) * (1.0 / GLA_TAU), 0.0)

    r = _iota((BLK, BLK), 0)
    c = _iota((BLK, BLK), 1)
    same = (r >= C) == (c >= C)
    tri = (same & (c <= r)) if d == 0 else (same & (c >= r))
    b = _dot_r3(jnp.where(tri, 1.0, 0.0).astype(BF16), g)
    yield
    first, second = (C - 1, 2 * C - 1) if d == 0 else (0, C)

    k = k_ref[bi].astype(F32)
    q_in = q_ref[bi].astype(F32) * (GLA_DK ** -0.5) * jnp.exp(b)
    kT = k.T
    bT = b.T
    lane = _iota((1, BLK), 1)
    blastT = jnp.where(lane < C, bT[:, first:first + 1], bT[:, second:second + 1])
    kinT = (kT * jnp.exp(-bT)).astype(BF16)
    koutT = kT * jnp.exp(blastT - bT)
    vb = v_ref[bi]
    yield

    lane_k = _iota((1, GLA_HEADS * GLA_DK), 1)
    qm = [jnp.where((lane_k >= h * GLA_DK) & (lane_k < (h + 1) * GLA_DK), q_in, 0.0).astype(BF16)
          for h in range(GLA_HEADS)]
    att = [jnp.where(tri, jnp.dot(qm[h], kinT, preferred_element_type=F32), 0.0).astype(BF16)
           for h in range(GLA_HEADS)]
    yield
    o_intra = [jnp.dot(att[h], vb[:, h * GLA_DV:(h + 1) * GLA_DV], preferred_element_type=F32)
               for h in range(GLA_HEADS)]
    yield

    S = s_scr[bi]
    o_inter = [[None, None] for _ in range(GLA_HEADS)]
    for cc in ((0, 1) if d == 0 else (1, 0)):
        Sb = S.astype(BF16)
        for h in range(GLA_HEADS):
            o_inter[h][cc] = jnp.dot(qm[h][cc * C:(cc + 1) * C, :], Sb, preferred_element_type=F32)
        tot = first if cc == 0 else second
        dec = jnp.exp(bT[:, tot:tot + 1])
        kc = jnp.where((lane >= cc * C) & (lane < (cc + 1) * C), koutT, 0.0).astype(BF16)
        upd = [jnp.dot(kc[h * GLA_DK:(h + 1) * GLA_DK, :], vb[:, h * GLA_DV:(h + 1) * GLA_DV],
                       preferred_element_type=F32) for h in range(GLA_HEADS)]
        S = S * dec + jnp.concatenate(upd, axis=0)
        yield
    s_scr[bi] = S

    o = jnp.concatenate(
        [o_intra[h] + jnp.concatenate(o_inter[h], axis=0) for h in range(GLA_HEADS)], axis=1)
    if d == 0:
        of_scr[bi, blk] = o.astype(of_scr.dtype)
    else:
        o = o + of_scr[bi, blk].astype(F32)
        gate = gate_ref[bi].astype(F32)
        nw = nw_ref[...]
        outs = []
        for h in range(GLA_HEADS):
            sl = slice(h * GLA_DV, (h + 1) * GLA_DV)
            outs.append(_rms(o[:, sl], nw) * _silu(gate[:, sl]))
        o_ref[bi] = jnp.concatenate(outs, axis=1).astype(o_ref.dtype)


def _gla_kernel(*refs, nblk, nb):
    s_scr = refs[-1]
    s = pl.program_id(1)

    @pl.when((s == 0) | (s == nblk))
    def _():
        s_scr[...] = jnp.zeros_like(s_scr)

    @pl.when(s < nblk)
    def _():
        _round_robin([_gla_block(bi, *refs, d=0, blk=s) for bi in range(nb)])

    @pl.when(s >= nblk)
    def _():
        _round_robin([_gla_block(bi, *refs, d=1, blk=2 * nblk - 1 - s) for bi in range(nb)])


def _scan_blk(s, nblk):
    return jnp.where(s < nblk, s, 2 * nblk - 1 - s)


def _scan_out_blk(s, nblk):
    return jnp.where(s < nblk, nblk - 1, 2 * nblk - 1 - s)


def _batch_slots(B, want):
    nb = want
    while B % nb:
        nb -= 1
    return nb


def _gla(proj, misc, wa_p, ba, nw):
    B, Lp, _ = proj.shape
    nblk = Lp // BLK
    nb = _batch_slots(B, 8)
    kd = GLA_HEADS * GLA_DK
    im = lambda col: (lambda b, s: (b, _scan_blk(s, nblk), col))
    return pl.pallas_call(
        functools.partial(_gla_kernel, nblk=nblk, nb=nb),
        grid=(B // nb, 2 * nblk),
        in_specs=[
            pl.BlockSpec((nb, BLK, kd), im(C_GQ // kd)),
            pl.BlockSpec((nb, BLK, kd), im(C_GK // kd)),
            pl.BlockSpec((nb, BLK, GLA_WIDTH), im(C_GV // GLA_WIDTH)),
            pl.BlockSpec((nb, BLK, GLA_WIDTH), im(C_GG // GLA_WIDTH)),
            pl.BlockSpec((nb, BLK, 128), im(0)),
            pl.BlockSpec((2, 128, kd), lambda b, s: (0, 0, 0)),
            pl.BlockSpec((2, 1, kd), lambda b, s: (0, 0, 0)),
            pl.BlockSpec((1, GLA_DV), lambda b, s: (0, 0)),
        ],
        out_specs=pl.BlockSpec((nb, BLK, GLA_WIDTH), lambda b, s: (b, _scan_out_blk(s, nblk), 0)),
        out_shape=jax.ShapeDtypeStruct((B, Lp, GLA_WIDTH), BF16),
        scratch_shapes=[
            pltpu.VMEM((nb, nblk, BLK, GLA_WIDTH), BF16),
            pltpu.VMEM((nb, kd, GLA_DV), F32),
        ],
        compiler_params=_params(("arbitrary", "arbitrary")),
        name="gla",
    )(proj, proj, proj, proj, misc, wa_p, ba, nw)


ATTN_BAND_TILES = 3
ATTN_L_FLOOR = 2.0 ** -60


def _q_halves(q_ref, bi):
    lane = _iota((1, 2 * DIFF_DQK), 1)
    q = q_ref[bi].astype(F32) * (DIFF_DQK ** -0.5 * LOG2E)
    return [jnp.where((lane >= c * DIFF_DQK) & (lane < (c + 1) * DIFF_DQK), q, 0.0).astype(BF16)
            for c in range(2)]


def _attn_slot(bi, i, q_ref, v_ref, band_ref, bmax_ref, kT_scr, kn_scr, l_scr, acc_scr, *, T, nk):
    sub = T // 128
    qb = _q_halves(q_ref, bi)
    qstack = jnp.concatenate(qb, axis=0)
    kn = kn_scr[bi]
    m = jnp.concatenate([
        jnp.sqrt(jnp.sum(x.astype(F32) ** 2, axis=-1, keepdims=True)) * kn[c:c + 1, :] + bmax_ref[...]
        for c, x in enumerate(qb)], axis=0)
    off_l = m - band_ref[0, 0:1, 0:128]
    off_r = m - band_ref[4, 0:1, 0:128]
    b0 = jnp.clip(i - 1, 0, nk - ATTN_BAND_TILES)
    padrow = jnp.where(_iota((1, 128), 1) < PAD, 3.0e4, 0.0)
    l_scr[bi] = jnp.zeros(l_scr.shape[1:], F32)
    acc_scr[bi] = jnp.zeros(acc_scr.shape[1:], F32)

    def sub_tile(j, t, off, bidx):
        cols = slice(t * 128, (t + 1) * 128)
        x = jnp.dot(qstack, kT_scr[bi, j, :, cols], preferred_element_type=F32) - off
        if bidx is not None:
            bt = band_ref[bidx, :, cols]
            x = x + jnp.concatenate([bt, bt], axis=0)
        if t == 0:
            x = x - jnp.where(j == 0, padrow, 0.0)
        e = jnp.exp2(x)
        l_scr[bi] += e
        vt = v_ref[bi, pl.ds(pl.multiple_of(j * T + t * 128, 128), 128), :]
        acc_scr[bi] += jnp.dot(e.astype(BF16), vt, preferred_element_type=F32)

    for jj in range(nk - ATTN_BAND_TILES):
        j = jj + jnp.where(jj >= b0, ATTN_BAND_TILES, 0)
        off = jnp.where(j < i, off_l, off_r)
        for t in range(sub):
            sub_tile(j, t, off, None)
            yield
    for b in range(ATTN_BAND_TILES):
        j = b0 + b
        for t in range(sub):
            sub_tile(j, t, m, j - i + 2)
            yield


def _attn_exact(bi, i, q_ref, v_ref, band_ref, kT_scr, l_scr, acc_scr, *, T, nk):
    sub = T // 128
    qb = _q_halves(q_ref, bi)
    col = _iota((1, T), 1)

    def fold(x, op):
        r = x[:, 0:128]
        for t in range(1, sub):
            r = op(r, x[:, t * 128:(t + 1) * 128])
        return r

    for c in range(2):
        for r0 in range(0, T, 128):
            rows = slice(r0, r0 + 128)
            qc = qb[c][rows]

            def scores(j):
                s = jnp.dot(qc, kT_scr[bi, j], preferred_element_type=F32)
                s = s + band_ref[jnp.clip(j - i, -2, 2) + 2, rows, :]
                return jnp.where((j == 0) & (col < PAD), -1e30, s)

            m_run = lax.fori_loop(0, nk, lambda j, mr: jnp.maximum(mr, fold(scores(j), jnp.maximum)),
                                  jnp.full((128, 128), -jnp.inf, F32))
            m = jnp.max(m_run, axis=-1, keepdims=True)

            def body(j, carry):
                l_run, acc = carry
                e = jnp.exp2(scores(j) - m)
                vt = v_ref[bi, pl.ds(pl.multiple_of(j * T, 128), T), :]
                return l_run + fold(e, jnp.add), acc + jnp.dot(e.astype(BF16), vt, preferred_element_type=F32)

            l_run, acc = lax.fori_loop(0, nk, body, (jnp.zeros((128, 128), F32), jnp.zeros((128, 128), F32)))
            l_scr[bi, c * T + r0:c * T + r0 + 128, :] = l_run
            acc_scr[bi, c * T + r0:c * T + r0 + 128, :] = acc


def _attn_kernel(q_ref, k_ref, v_ref, g_ref, band_ref, bmax_ref, lam_ref, lc_ref, nw_ref, o_ref,
                 kT_scr, kn_scr, l_scr, acc_scr, *, nb, T, nk):
    i = pl.program_id(2)

    @pl.when(i == 0)
    def _():
        lane = _iota((1, 2 * DIFF_DQK), 1)
        for bi in range(nb):
            for j in range(nk):
                kT_scr[bi, j] = k_ref[bi, j * T:(j + 1) * T, :].astype(F32).T.astype(BF16)
            k2 = k_ref[bi].astype(F32) ** 2
            for c in range(2):
                half = (lane >= c * DIFF_DQK) & (lane < (c + 1) * DIFF_DQK)
                n2 = jnp.max(jnp.sum(jnp.where(half, k2, 0.0), axis=-1, keepdims=True), axis=0, keepdims=True)
                kn_scr[bi, c:c + 1, :] = jnp.broadcast_to(jnp.sqrt(n2), (1, 128))

    _round_robin([
        _attn_slot(bi, i, q_ref, v_ref, band_ref, bmax_ref, kT_scr, kn_scr, l_scr, acc_scr, T=T, nk=nk)
        for bi in range(nb)])

    lp = lam_ref[...]
    lc = lc_ref[...]
    lam = (jnp.exp(jnp.sum(lp[0:1] * lp[1:2], axis=-1, keepdims=True))
           - jnp.exp(jnp.sum(lp[2:3] * lp[3:4], axis=-1, keepdims=True)) + lc[:, 0:1])
    for bi in range(nb):
        @pl.when(jnp.min(jnp.sum(l_scr[bi], axis=-1, keepdims=True)) < ATTN_L_FLOOR)
        def _():
            _attn_exact(bi, i, q_ref, v_ref, band_ref, kT_scr, l_scr, acc_scr, T=T, nk=nk)

    for bi in range(nb):
        out = acc_scr[bi] / jnp.sum(l_scr[bi], axis=-1, keepdims=True)
        o = out[0:T] - lam * out[T:2 * T]
        y = _rms(o, nw_ref[...]) * lc[:, 1:2]
        o_ref[bi] = (y * _silu(g_ref[bi].astype(F32))).astype(o_ref.dtype)


def _attn(proj, band, bmax, diff_lambda, lcoef, nw, T):
    B, Lp, _ = proj.shape
    nk = Lp // T
    H = DIFF_HEADS
    assert nk >= ATTN_BAND_TILES
    nb = _batch_slots(B, 2)
    return pl.pallas_call(
        functools.partial(_attn_kernel, nb=nb, T=T, nk=nk),
        grid=(B // nb, H, nk),
        in_specs=[
            pl.BlockSpec((nb, T, 128), lambda b, h, i: (b, i, C_DQ // 128 + h)),
            pl.BlockSpec((nb, Lp, 128), lambda b, h, i: (b, 0, C_DK // 128 + h)),
            pl.BlockSpec((nb, Lp, 128), lambda b, h, i: (b, 0, C_DV // 128 + h)),
            pl.BlockSpec((nb, T, 128), lambda b, h, i: (b, i, C_DG // 128 + h)),
            pl.BlockSpec((None, 5, T, T), lambda b, h, i: (h, 0, 0, 0)),
            pl.BlockSpec((None, 1, 128), lambda b, h, i: (h, 0, 0)),
            pl.BlockSpec((4, DIFF_DQK), lambda b, h, i: (0, 0)),
            pl.BlockSpec((1, 128), lambda b, h, i: (0, 0)),
            pl.BlockSpec((1, DIFF_DV), lambda b, h, i: (0, 0)),
        ],
        out_specs=pl.BlockSpec((nb, T, 128), lambda b, h, i: (b, i, h)),
        out_shape=jax.ShapeDtypeStruct((B, Lp, DIFF_WIDTH), BF16),
        scratch_shapes=[
            pltpu.VMEM((nb, nk, 128, T), BF16),
            pltpu.VMEM((nb, 8, 128), F32),
            pltpu.VMEM((nb, 2 * T, 128), F32),
            pltpu.VMEM((nb, 2 * T, 128), F32),
        ],
        compiler_params=_params(("arbitrary", "arbitrary", "arbitrary")),
        name="diff_attn",
    )(proj, proj, proj, proj, band, bmax, diff_lambda, lcoef, nw)


def _bucket_table(T):
    nb = N_BUCKETS // 2
    max_exact = nb // 2
    rel = np.arange(6 * T) - (3 * T - 1)
    ret = np.where(rel > 0, nb, 0)
    n = np.abs(rel)
    nf = np.maximum(n, 1).astype(np.float64)
    large = max_exact + (np.log(nf / max_exact) / math.log(MAX_DISTANCE / max_exact)
                         * (nb - max_exact)).astype(np.int32)
    large = np.minimum(large, nb - 1)
    return (ret + np.where(n < max_exact, n, large)).astype(np.int32)


def _bias_band(rel_bias, T):
    n = 6 * T
    v = (rel_bias.astype(F32)[_bucket_table(T)] * LOG2E).T
    rows = jnp.tile(v, (1, T))[:, :T * (n - 1)].reshape(DIFF_HEADS, T, n - 1)
    band = rows[:, :, T - 1:n - 1].reshape(DIFF_HEADS, T, 5, T)
    return jnp.transpose(band, (0, 2, 1, 3))


def _conv_kernel(x_ref, w_ref, b_ref, o_ref):
    x = x_ref[...].astype(F32)
    w = w_ref[...]
    n = x.shape[0]
    half = (SSD_CONV - 1) // 2
    acc = x * w[half:half + 1, :] + b_ref[...]
    for kk in range(SSD_CONV):
        if kk != half:
            acc = acc + pltpu.roll(x, (half - kk) % n, 0) * w[kk:kk + 1, :]
    o_ref[...] = _silu(acc).astype(o_ref.dtype)


def _conv(proj, conv_w, conv_b):
    B, Lp, _ = proj.shape
    tn = 256
    return pl.pallas_call(
        _conv_kernel,
        grid=(B, SSD_CONV_DIM // tn),
        in_specs=[
            pl.BlockSpec((None, Lp, tn), lambda b, j: (b, 0, C_XBC // tn + j)),
            pl.BlockSpec((SSD_CONV, tn), lambda b, j: (0, j)),
            pl.BlockSpec((1, tn), lambda b, j: (0, j)),
        ],
        out_specs=pl.BlockSpec((None, Lp, tn), lambda b, j: (b, 0, j)),
        out_shape=jax.ShapeDtypeStruct((B, Lp, SSD_CONV_DIM), BF16),
        compiler_params=_params(("arbitrary", "arbitrary")),
        name="ssd_conv",
    )(proj, conv_w, conv_b.reshape(1, SSD_CONV_DIM))


def _ssd_block(bi, xc_ref, z_ref, misc_ref, dtb_r_ref, al_r_ref, e_ref, dskip_ref,
               nw_ref, o_ref, yf_scr, s_scr, *, d, blk):
    G, N, P = SSD_GROUPS, SSD_STATE, SSD_HEADDIM
    R = SSD_HEADS // G
    W = R * P
    row = _iota((BLK, 1), 0)
    lane = _iota((1, BLK), 1)
    dt = jnp.where((blk * BLK + row) >= PAD, _softplus(misc_ref[bi] + dtb_r_ref[...]), 0.0)
    a = dt * (-jnp.exp(al_r_ref[...]))
    aT = a.T

    r = _iota((BLK, BLK), 0)
    c = _iota((BLK, BLK), 1)
    tri = (c <= r) if d == 0 else (c >= r)
    triT = (r <= c) if d == 0 else (r >= c)
    cum = _dot_r3(jnp.where(tri, 1.0, 0.0).astype(BF16), a)
    cumT = _dot_l3(aT, jnp.where(triT, 1.0, 0.0).astype(BF16))
    yield

    lo = DT_LANE0 + SSD_HEADS * d
    hm = (lane >= lo) & (lane < lo + SSD_HEADS)
    last = BLK - 1 if d == 0 else 0
    tot = cum[last:last + 1, :]
    ecum = jnp.where(hm, jnp.exp(cum), 0.0)
    ecum_hi = ecum.astype(BF16)
    stack = jnp.concatenate([
        jnp.where(hm, dt, 0.0).astype(BF16),
        jnp.where(hm, jnp.exp(tot - cum), 0.0).astype(BF16),
        ecum_hi,
        (ecum - ecum_hi.astype(F32)).astype(BF16)], axis=0)
    yield

    for g in range(G):
        cols = slice(g * W, (g + 1) * W)
        ex = jnp.dot(stack, e_ref[d, :, cols], preferred_element_type=F32)
        dt_x = ex[0:BLK]
        toend_x = ex[BLK:2 * BLK]
        ecum_x = ex[2 * BLK:3 * BLK] + ex[3 * BLK:4 * BLK]
        etot_x = ecum_x[last:last + 1, :]
        xs = xc_ref[bi, :, cols].astype(F32)
        xd = xs * dt_x
        xdb = xd.astype(BF16)
        xdw = (xd * toend_x).astype(BF16)
        BgT = xc_ref[bi, :, SSD_WIDTH + g * N:SSD_WIDTH + (g + 1) * N].astype(F32).T.astype(BF16)
        Cg = xc_ref[bi, :, SSD_WIDTH + (G + g) * N:SSD_WIDTH + (G + g + 1) * N]
        CB = jnp.dot(Cg, BgT, preferred_element_type=F32)
        S = s_scr[bi, g]
        y_off = jnp.dot(Cg, S.astype(BF16), preferred_element_type=F32) * ecum_x
        s_scr[bi, g] = S * etot_x + jnp.dot(BgT, xdw, preferred_element_type=F32)
        yield
        y_pairs = []
        for pr in range(R // 2):
            h0 = g * R + 2 * pr
            pcols = slice(2 * pr * P, (2 * pr + 2) * P)
            xpair = xdb[:, pcols]
            acc = y_off[:, pcols]
            for hh in range(2):
                li = lo + h0 + hh
                seg = cum[:, li:li + 1] - cumT[li:li + 1, :]
                dec = jnp.exp(jnp.where(tri, seg, -jnp.inf))
                sc = (CB * dec).astype(BF16)
                xm = jnp.where((lane >= hh * P) & (lane < (hh + 1) * P), xpair, 0.0)
                acc = acc + jnp.dot(sc, xm, preferred_element_type=F32)
            if d == 0:
                yf_scr[bi, blk, :, g * W + 2 * pr * P:g * W + (2 * pr + 2) * P] = acc.astype(yf_scr.dtype)
            else:
                y_pairs.append(acc)
            yield
        if d == 1:
            y = jnp.concatenate(y_pairs, axis=1)
            y = y + yf_scr[bi, blk, :, cols].astype(F32) + xs * dskip_ref[:, cols]
            y = y * _silu(z_ref[bi, :, cols].astype(F32))
            o_ref[bi, :, cols] = _rms(y, nw_ref[:, cols]).astype(o_ref.dtype)


def _ssd_kernel(*refs, nblk, nb):
    s_scr = refs[-1]
    s = pl.program_id(1)

    @pl.when((s == 0) | (s == nblk))
    def _():
        s_scr[...] = jnp.zeros_like(s_scr)

    @pl.when(s < nblk)
    def _():
        _round_robin([_ssd_block(bi, *refs, d=0, blk=s) for bi in range(nb)])

    @pl.when(s >= nblk)
    def _():
        _round_robin([_ssd_block(bi, *refs, d=1, blk=2 * nblk - 1 - s) for bi in range(nb)])


def _ssd(proj, misc, xc, dtb_r, al_r, E, dskip, nw):
    B, Lp, _ = proj.shape
    nblk = Lp // BLK
    nb = _batch_slots(B, 4)
    const2 = lambda b, s: (0, 0)
    return pl.pallas_call(
        functools.partial(_ssd_kernel, nblk=nblk, nb=nb),
        grid=(B // nb, 2 * nblk),
        in_specs=[
            pl.BlockSpec((nb, BLK, SSD_CONV_DIM), lambda b, s: (b, _scan_blk(s, nblk), 0)),
            pl.BlockSpec((nb, BLK, SSD_WIDTH), lambda b, s: (b, _scan_blk(s, nblk), C_Z // SSD_WIDTH)),
            pl.BlockSpec((nb, BLK, 128), lambda b, s: (b, _scan_blk(s, nblk), 0)),
            pl.BlockSpec((1, 128), const2),
            pl.BlockSpec((1, 128), const2),
            pl.BlockSpec((2, 128, SSD_WIDTH), lambda b, s: (0, 0, 0)),
            pl.BlockSpec((1, SSD_WIDTH), const2),
            pl.BlockSpec((1, SSD_WIDTH), const2),
        ],
        out_specs=pl.BlockSpec((nb, BLK, SSD_WIDTH), lambda b, s: (b, _scan_out_blk(s, nblk), 0)),
        out_shape=jax.ShapeDtypeStruct((B, Lp, SSD_WIDTH), BF16),
        scratch_shapes=[
            pltpu.VMEM((nb, nblk, BLK, SSD_WIDTH), BF16),
            pltpu.VMEM((nb, SSD_GROUPS, SSD_STATE, SSD_WIDTH // SSD_GROUPS), F32),
        ],
        compiler_params=_params(("arbitrary", "arbitrary")),
        name="ssd_scan",
    )(xc, proj, misc, dtb_r, al_r, E, dskip, nw)


def _permute_w_in(w):
    o = np.cumsum([0, 256, 256, 512, 512, 32, 512, 512, 512, 512, 1024, 1536, 32])
    gq, gk, gv, gg, gcode, dq, dk, dv, dg, z, xbc, dt, end = [int(v) for v in o]
    pieces = [w[..., gq:gcode], w[..., dq:dg], w[..., z:xbc], w[..., dg:z], w[..., xbc:dt],
              w[..., gcode:dq], w[..., dt:end],
              jnp.zeros(w.shape[:-1] + (N_PROJ - C_MISC - 64,), w.dtype)]
    return jnp.concatenate(pieces, axis=-1)


def _expansion():
    E = np.zeros((2, 128, SSD_WIDTH), np.float32)
    for d in range(2):
        for h in range(SSD_HEADS):
            E[d, DT_LANE0 + SSD_HEADS * d + h, h * SSD_HEADDIM:(h + 1) * SSD_HEADDIM] = 1.0
    return jnp.asarray(E, BF16)


def _misc_row(p):
    flat = p.reshape(p.shape[0], 1, 2 * SSD_HEADS).astype(F32)
    return jnp.pad(flat, ((0, 0), (0, 0), (DT_LANE0, 128 - DT_LANE0 - 2 * SSD_HEADS)))


def kernel(x, meta_tokens, rel_bias, final_norm_w, norm_w, w_in, w_out, gla_wa2, gla_ba, gla_norm_w,
           diff_lambda, diff_norm_w, conv_w, conv_b, ssd_A_log, ssd_dt_bias, ssd_D, ssd_norm_w):
    B, S, D = x.shape
    assert D == D_MODEL and S % BLK == 0
    Lp = TOK0 + S
    T = 384 if Lp % 384 == 0 else 128

    h = jnp.concatenate([
        jnp.zeros((B, PAD, D), x.dtype),
        jnp.broadcast_to(meta_tokens[None].astype(x.dtype), (B, N_META, D)),
        x], axis=1)

    band = _bias_band(rel_bias, T)
    bmax = jnp.broadcast_to((jnp.max(rel_bias.astype(F32), axis=0) * LOG2E)[:, None, None], (DIFF_HEADS, 1, 128))
    E = _expansion()
    lam_init = np.array([0.8 - 0.6 * math.exp(-0.3 * l) for l in range(DEPTH)], np.float32)
    lcoef = np.zeros((DEPTH, 1, 128), np.float32)
    lcoef[:, 0, 0] = lam_init
    lcoef[:, 0, 1] = 1.0 - lam_init

    kd = GLA_HEADS * GLA_DK
    wa_p = jnp.zeros((DEPTH, 2, 128, kd), F32)
    wa_p = wa_p.at[:, 0, 0:GLA_RANK].set(gla_wa2[:, 0].astype(F32))
    wa_p = wa_p.at[:, 1, GLA_RANK:2 * GLA_RANK].set(gla_wa2[:, 1].astype(F32))
    dtb_r = _misc_row(ssd_dt_bias)
    al_r = _misc_row(ssd_A_log)
    layers = dict(
        norm_w=norm_w,
        w_in=_permute_w_in(w_in).astype(BF16),
        w_out=w_out.astype(BF16),
        wa_p=wa_p,
        ba=gla_ba.reshape(DEPTH, 2, 1, kd).astype(F32),
        gla_nw=gla_norm_w.reshape(DEPTH, 1, GLA_DV).astype(F32),
        lam=diff_lambda.astype(F32),
        lcoef=jnp.asarray(lcoef),
        diff_nw=diff_norm_w.reshape(DEPTH, 1, DIFF_DV).astype(F32),
        conv_w=conv_w.astype(F32),
        conv_b=conv_b.astype(F32),
        dtb_r=dtb_r,
        al_r=al_r,
        dskip=jnp.repeat(ssd_D.astype(F32), SSD_HEADDIM, axis=-1).reshape(DEPTH, 1, SSD_WIDTH),
        ssd_nw=ssd_norm_w.reshape(DEPTH, 1, SSD_WIDTH).astype(F32),
    )

    def layer(h, p):
        proj, misc = _inproj(h, p["norm_w"], p["w_in"])
        o_gla = _gla(proj, misc, p["wa_p"], p["ba"], p["gla_nw"])
        o_diff = _attn(proj, band, bmax, p["lam"], p["lcoef"], p["diff_nw"], T)
        xc = _conv(proj, p["conv_w"], p["conv_b"])
        o_ssd = _ssd(proj, misc, xc, p["dtb_r"], p["al_r"], E, p["dskip"], p["ssd_nw"])
        return _outproj(h, o_gla, o_diff, o_ssd, p["w_out"]), None

    h, _ = lax.scan(layer, h, layers)
    return _final_norm(h, final_norm_w)
```

```python
import functools
import math

import numpy as np
import jax
import jax.numpy as jnp
from jax import lax
from jax.experimental import pallas as pl
from jax.experimental.pallas import tpu as pltpu

F32 = jnp.float32
BF16 = jnp.bfloat16

D_MODEL = 2048
DEPTH = 4
N_META = 16
EPS = 1e-6
GLA_HEADS = 4
GLA_DK = 64
GLA_DV = 128
GLA_WIDTH = 512
GLA_RANK = 16
GLA_TAU = 16.0
GLA_CHUNK = 64
DIFF_HEADS = 4
DIFF_DQK = 64
DIFF_DV = 128
DIFF_WIDTH = 512
N_BUCKETS = 32
MAX_DISTANCE = 128
SSD_WIDTH = 1024
SSD_HEADDIM = 64
SSD_HEADS = 16
SSD_GROUPS = 2
SSD_STATE = 128
SSD_CONV = 5
SSD_CHUNK = 128
SSD_CONV_DIM = SSD_WIDTH + 2 * SSD_GROUPS * SSD_STATE

BLK = SSD_CHUNK
assert BLK == 2 * GLA_CHUNK == 128
TOK0 = BLK
PAD = TOK0 - N_META

C_GQ, C_GK, C_GV, C_GG = 0, 256, 512, 1024
C_DQ, C_DK, C_DV = 1536, 2048, 2560
C_Z, C_DG, C_XBC, C_MISC = 3072, 4096, 4608, 6144
N_PROJ = 6272
DT_LANE0 = 2 * GLA_RANK

VMEM_LIMIT = 56 * 1024 * 1024
LOG2E = 1.4426950408889634


def _params(sem, limit=VMEM_LIMIT):
    return pltpu.CompilerParams(dimension_semantics=sem, vmem_limit_bytes=limit)


def _silu(x):
    return x / (1.0 + jnp.exp(-x))


def _softplus(x):
    return jnp.maximum(x, 0.0) + jnp.log(1.0 + jnp.exp(-jnp.abs(x)))


def _log_sigmoid(x):
    return jnp.minimum(x, 0.0) - jnp.log(1.0 + jnp.exp(-jnp.abs(x)))


def _split3(x):
    hi = x.astype(BF16)
    r = x - hi.astype(F32)
    mid = r.astype(BF16)
    lo = (r - mid.astype(F32)).astype(BF16)
    return hi, mid, lo


def _dot_l3(a, b_exact):
    hi, mid, lo = _split3(a)
    d = functools.partial(jnp.dot, preferred_element_type=F32)
    return d(hi, b_exact) + d(mid, b_exact) + d(lo, b_exact)


def _dot_r3(a_exact, b):
    hi, mid, lo = _split3(b)
    d = functools.partial(jnp.dot, preferred_element_type=F32)
    return d(a_exact, hi) + d(a_exact, mid) + d(a_exact, lo)


def _dot_22(a, b):
    ah = a.astype(BF16)
    al = (a - ah.astype(F32)).astype(BF16)
    bh = b.astype(BF16)
    bl = (b - bh.astype(F32)).astype(BF16)
    d = functools.partial(jnp.dot, preferred_element_type=F32)
    return d(ah, bh) + d(ah, bl) + d(al, bh)


def _iota(shape, dim):
    return lax.broadcasted_iota(jnp.int32, shape, dim)


def _rms(x, w):
    return x * lax.rsqrt(jnp.mean(x * x, axis=-1, keepdims=True) + EPS) * w


def _inproj_kernel(h_ref, nw_ref, w_ref, wm_ref, o_ref, om_ref, u_scr, *, tm):
    @pl.when(pl.program_id(2) == 0)
    def _():
        x = h_ref[...]
        y = _rms(x, nw_ref[...])
        row = pl.program_id(1) * tm + _iota((tm, 1), 0)
        u_scr[...] = jnp.where(row >= PAD, y, 0.0).astype(BF16)
        om_ref[...] = jnp.dot(u_scr[...], wm_ref[...], preferred_element_type=F32)

    o_ref[...] = jnp.dot(u_scr[...], w_ref[...], preferred_element_type=F32).astype(o_ref.dtype)


def _inproj(h, norm_w, w_p):
    B, Lp, D = h.shape
    tm = Lp // 4
    tn = C_MISC // 4
    return pl.pallas_call(
        functools.partial(_inproj_kernel, tm=tm),
        grid=(B, Lp // tm, C_MISC // tn),
        in_specs=[
            pl.BlockSpec((None, tm, D), lambda b, i, j: (b, i, 0)),
            pl.BlockSpec((1, D), lambda b, i, j: (0, 0)),
            pl.BlockSpec((D, tn), lambda b, i, j: (0, j)),
            pl.BlockSpec((D, 128), lambda b, i, j: (0, C_MISC // 128)),
        ],
        out_specs=[
            pl.BlockSpec((None, tm, tn), lambda b, i, j: (b, i, j)),
            pl.BlockSpec((None, tm, 128), lambda b, i, j: (b, i, 0)),
        ],
        out_shape=[jax.ShapeDtypeStruct((B, Lp, C_MISC), BF16),
                   jax.ShapeDtypeStruct((B, Lp, 128), F32)],
        scratch_shapes=[pltpu.VMEM((tm, D), BF16)],
        compiler_params=_params(("arbitrary", "arbitrary", "arbitrary")),
        name="inproj",
    )(h, norm_w.reshape(1, D), w_p, w_p)


def _outproj_kernel(h_ref, a_ref, b_ref, c_ref, w_ref, o_ref):
    d = functools.partial(jnp.dot, preferred_element_type=F32)
    acc = d(a_ref[...], w_ref[0:GLA_WIDTH, :])
    acc += d(b_ref[...], w_ref[GLA_WIDTH:GLA_WIDTH + DIFF_WIDTH, :])
    acc += d(c_ref[...], w_ref[GLA_WIDTH + DIFF_WIDTH:, :])
    o_ref[...] = h_ref[...] + acc


def _outproj(h, o_gla, o_diff, o_ssd, w_out):
    B, Lp, D = h.shape
    tm = Lp // 8 if (Lp // 8) % 16 == 0 else Lp // 4
    tn = D
    return pl.pallas_call(
        _outproj_kernel,
        grid=(B, Lp // tm, D // tn),
        in_specs=[
            pl.BlockSpec((None, tm, tn), lambda b, i, j: (b, i, j)),
            pl.BlockSpec((None, tm, GLA_WIDTH), lambda b, i, j: (b, i, 0)),
            pl.BlockSpec((None, tm, DIFF_WIDTH), lambda b, i, j: (b, i, 0)),
            pl.BlockSpec((None, tm, SSD_WIDTH), lambda b, i, j: (b, i, 0)),
            pl.BlockSpec((D, tn), lambda b, i, j: (0, j)),
        ],
        out_specs=pl.BlockSpec((None, tm, tn), lambda b, i, j: (b, i, j)),
        out_shape=jax.ShapeDtypeStruct((B, Lp, D), F32),
        input_output_aliases={0: 0},
        compiler_params=_params(("arbitrary", "arbitrary", "arbitrary")),
        name="outproj",
    )(h, o_gla, o_diff, o_ssd, w_out)


def _final_kernel(h_ref, w_ref, o_ref):
    o_ref[...] = _rms(h_ref[...], w_ref[...])


def _final_norm(h, w):
    B, Lp, D = h.shape
    S = Lp - TOK0
    rb = 512 if S % 512 == 0 else BLK
    return pl.pallas_call(
        _final_kernel,
        grid=(B, S // rb),
        in_specs=[
            pl.BlockSpec((pl.Element(rb), pl.Element(D)),
                         lambda b, i: (pl.multiple_of(b * Lp + TOK0 + i * rb, BLK), 0)),
            pl.BlockSpec((1, D), lambda b, i: (0, 0)),
        ],
        out_specs=pl.BlockSpec((None, rb, D), lambda b, i: (b, i, 0)),
        out_shape=jax.ShapeDtypeStruct((B, S, D), F32),
        compiler_params=_params(("arbitrary", "arbitrary")),
        name="final_norm",
    )(h.reshape(B * Lp, D), w.reshape(1, D))


def _round_robin(gens):
    alive = list(gens)
    while alive:
        for g in list(alive):
            try:
                next(g)
            except StopIteration:
                alive.remove(g)


def _gla_block(bi, q_ref, k_ref, v_ref, gate_ref, misc_ref, wa_ref, ba_ref, nw_ref, o_ref,
               of_scr, s_scr, *, d, blk):
    C = GLA_CHUNK
    row = _iota((BLK, 1), 0)
    valid = (blk * BLK + row) >= PAD
    x = _dot_22(misc_ref[bi], wa_ref[d]) + ba_ref[d]
    yield
    g = jnp.where(valid, _log_sigmoid(x) * (LOG2E / GLA_TAU), 0.0)

    r = _iota((BLK, BLK), 0)
    c = _iota((BLK, BLK), 1)
    same = (r >= C) == (c >= C)
    tri = (same & (c <= r)) if d == 0 else (same & (c >= r))
    b = _dot_r3(jnp.where(tri, 1.0, 0.0).astype(BF16), g)
    yield
    first, second = (C - 1, 2 * C - 1) if d == 0 else (0, C)

    k = k_ref[bi].astype(F32)
    q_in = q_ref[bi].astype(F32) * (GLA_DK ** -0.5) * jnp.exp2(b)
    kT = k.T
    bT = b.T
    lane = _iota((1, BLK), 1)
    blastT = jnp.where(lane < C, bT[:, first:first + 1], bT[:, second:second + 1])
    kinT = (kT * jnp.exp2(-bT)).astype(BF16)
    koutT = kT * jnp.exp2(blastT - bT)
    vb = v_ref[bi]
    yield

    lane_k = _iota((1, GLA_HEADS * GLA_DK), 1)
    qm = [jnp.where((lane_k >= h * GLA_DK) & (lane_k < (h + 1) * GLA_DK), q_in, 0.0).astype(BF16)
          for h in range(GLA_HEADS)]
    att = [jnp.where(tri, jnp.dot(qm[h], kinT, preferred_element_type=F32), 0.0).astype(BF16)
           for h in range(GLA_HEADS)]
    yield
    o_intra = [jnp.dot(att[h], vb[:, h * GLA_DV:(h + 1) * GLA_DV], preferred_element_type=F32)
               for h in range(GLA_HEADS)]
    yield

    S = s_scr[bi]
    o_inter = [[None, None] for _ in range(GLA_HEADS)]
    for cc in ((0, 1) if d == 0 else (1, 0)):
        Sb = S.astype(BF16)
        for h in range(GLA_HEADS):
            o_inter[h][cc] = jnp.dot(qm[h][cc * C:(cc + 1) * C, :], Sb, preferred_element_type=F32)
        tot = first if cc == 0 else second
        dec = jnp.exp2(bT[:, tot:tot + 1])
        kc = jnp.where((lane >= cc * C) & (lane < (cc + 1) * C), koutT, 0.0).astype(BF16)
        upd = [jnp.dot(kc[h * GLA_DK:(h + 1) * GLA_DK, :], vb[:, h * GLA_DV:(h + 1) * GLA_DV],
                       preferred_element_type=F32) for h in range(GLA_HEADS)]
        S = S * dec + jnp.concatenate(upd, axis=0)
        yield
    s_scr[bi] = S

    o = jnp.concatenate(
        [o_intra[h] + jnp.concatenate(o_inter[h], axis=0) for h in range(GLA_HEADS)], axis=1)
    if d == 0:
        of_scr[bi, blk] = o.astype(of_scr.dtype)
    else:
        o = o + of_scr[bi, blk].astype(F32)
        gate = gate_ref[bi].astype(F32)
        nw = nw_ref[...]
        outs = []
        for h in range(GLA_HEADS):
            sl = slice(h * GLA_DV, (h + 1) * GLA_DV)
            outs.append(_rms(o[:, sl], nw) * _silu(gate[:, sl]))
        o_ref[bi] = jnp.concatenate(outs, axis=1).astype(o_ref.dtype)


def _gla_kernel(*refs, nblk, nb):
    s_scr = refs[-1]
    s = pl.program_id(1)

    @pl.when((s == 0) | (s == nblk))
    def _():
        s_scr[...] = jnp.zeros_like(s_scr)

    @pl.when(s < nblk)
    def _():
        _round_robin([_gla_block(bi, *refs, d=0, blk=s) for bi in range(nb)])

    @pl.when(s >= nblk)
    def _():
        _round_robin([_gla_block(bi, *refs, d=1, blk=2 * nblk - 1 - s) for bi in range(nb)])


def _scan_blk(s, nblk):
    return jnp.where(s < nblk, s, 2 * nblk - 1 - s)


def _scan_out_blk(s, nblk):
    return jnp.where(s < nblk, nblk - 1, 2 * nblk - 1 - s)


def _batch_slots(B, want):
    nb = want
    while B % nb:
        nb -= 1
    return nb


def _gla(proj, misc, wa_p, ba, nw):
    B, Lp, _ = proj.shape
    nblk = Lp // BLK
    nb = _batch_slots(B, 8)
    kd = GLA_HEADS * GLA_DK
    im = lambda col: (lambda b, s: (b, _scan_blk(s, nblk), col))
    return pl.pallas_call(
        functools.partial(_gla_kernel, nblk=nblk, nb=nb),
        grid=(B // nb, 2 * nblk),
        in_specs=[
            pl.BlockSpec((nb, BLK, kd), im(C_GQ // kd)),
            pl.BlockSpec((nb, BLK, kd), im(C_GK // kd)),
            pl.BlockSpec((nb, BLK, GLA_WIDTH), im(C_GV // GLA_WIDTH)),
            pl.BlockSpec((nb, BLK, GLA_WIDTH), im(C_GG // GLA_WIDTH)),
            pl.BlockSpec((nb, BLK, 128), im(0)),
            pl.BlockSpec((2, 128, kd), lambda b, s: (0, 0, 0)),
            pl.BlockSpec((2, 1, kd), lambda b, s: (0, 0, 0)),
            pl.BlockSpec((1, GLA_DV), lambda b, s: (0, 0)),
        ],
        out_specs=pl.BlockSpec((nb, BLK, GLA_WIDTH), lambda b, s: (b, _scan_out_blk(s, nblk), 0)),
        out_shape=jax.ShapeDtypeStruct((B, Lp, GLA_WIDTH), BF16),
        scratch_shapes=[
            pltpu.VMEM((nb, nblk, BLK, GLA_WIDTH), BF16),
            pltpu.VMEM((nb, kd, GLA_DV), F32),
        ],
        compiler_params=_params(("arbitrary", "arbitrary")),
        name="gla",
    )(proj, proj, proj, proj, misc, wa_p, ba, nw)


ATTN_BAND_TILES = 3
ATTN_L_FLOOR = 2.0 ** -60


def _q_halves(q_ref, bi):
    lane = _iota((1, 2 * DIFF_DQK), 1)
    q = q_ref[bi].astype(F32) * (DIFF_DQK ** -0.5 * LOG2E)
    return [jnp.where((lane >= c * DIFF_DQK) & (lane < (c + 1) * DIFF_DQK), q, 0.0).astype(BF16)
            for c in range(2)]


def _attn_slot(bi, i, q_ref, v_ref, band_ref, bmax_ref, kT_scr, kn_scr, l_scr, acc_scr, *, T, nk):
    sub = T // 128
    qb = _q_halves(q_ref, bi)
    qstack = jnp.concatenate(qb, axis=0)
    kn = kn_scr[bi]
    m = jnp.concatenate([
        jnp.sqrt(jnp.sum(x.astype(F32) ** 2, axis=-1, keepdims=True)) * kn[c:c + 1, :] + bmax_ref[...]
        for c, x in enumerate(qb)], axis=0)
    off_l = m - band_ref[0, 0:1, 0:128]
    off_r = m - band_ref[4, 0:1, 0:128]
    b0 = jnp.clip(i - 1, 0, nk - ATTN_BAND_TILES)
    padrow = jnp.where(_iota((1, 128), 1) < PAD, 3.0e4, 0.0)
    l_scr[bi] = jnp.zeros(l_scr.shape[1:], F32)
    acc_scr[bi] = jnp.zeros(acc_scr.shape[1:], F32)

    def sub_tile(j, t, off, bidx):
        cols = slice(t * 128, (t + 1) * 128)
        x = jnp.dot(qstack, kT_scr[bi, j, :, cols], preferred_element_type=F32) - off
        if bidx is not None:
            bt = band_ref[bidx, :, cols]
            x = x + jnp.concatenate([bt, bt], axis=0)
        if t == 0:
            x = x - jnp.where(j == 0, padrow, 0.0)
        e = jnp.exp2(x)
        l_scr[bi] += e
        vt = v_ref[bi, pl.ds(pl.multiple_of(j * T + t * 128, 128), 128), :]
        acc_scr[bi] += jnp.dot(e.astype(BF16), vt, preferred_element_type=F32)

    for jj in range(nk - ATTN_BAND_TILES):
        j = jj + jnp.where(jj >= b0, ATTN_BAND_TILES, 0)
        off = jnp.where(j < i, off_l, off_r)
        for t in range(sub):
            sub_tile(j, t, off, None)
            yield
    for b in range(ATTN_BAND_TILES):
        j = b0 + b
        for t in range(sub):
            sub_tile(j, t, m, j - i + 2)
            yield


def _attn_exact(bi, i, q_ref, v_ref, band_ref, kT_scr, l_scr, acc_scr, *, T, nk):
    sub = T // 128
    qb = _q_halves(q_ref, bi)
    col = _iota((1, T), 1)

    def fold(x, op):
        r = x[:, 0:128]
        for t in range(1, sub):
            r = op(r, x[:, t * 128:(t + 1) * 128])
        return r

    for c in range(2):
        for r0 in range(0, T, 128):
            rows = slice(r0, r0 + 128)
            qc = qb[c][rows]

            def scores(j):
                s = jnp.dot(qc, kT_scr[bi, j], preferred_element_type=F32)
                s = s + band_ref[jnp.clip(j - i, -2, 2) + 2, rows, :]
                return jnp.where((j == 0) & (col < PAD), -1e30, s)

            m_run = lax.fori_loop(0, nk, lambda j, mr: jnp.maximum(mr, fold(scores(j), jnp.maximum)),
                                  jnp.full((128, 128), -jnp.inf, F32))
            m = jnp.max(m_run, axis=-1, keepdims=True)

            def body(j, carry):
                l_run, acc = carry
                e = jnp.exp2(scores(j) - m)
                vt = v_ref[bi, pl.ds(pl.multiple_of(j * T, 128), T), :]
                return l_run + fold(e, jnp.add), acc + jnp.dot(e.astype(BF16), vt, preferred_element_type=F32)

            l_run, acc = lax.fori_loop(0, nk, body, (jnp.zeros((128, 128), F32), jnp.zeros((128, 128), F32)))
            l_scr[bi, c * T + r0:c * T + r0 + 128, :] = l_run
            acc_scr[bi, c * T + r0:c * T + r0 + 128, :] = acc


def _attn_kernel(q_ref, k_ref, v_ref, g_ref, band_ref, bmax_ref, lam_ref, lc_ref, nw_ref, o_ref,
                 kT_scr, kn_scr, l_scr, acc_scr, *, nb, T, nk):
    i = pl.program_id(2)

    @pl.when(i == 0)
    def _():
        lane = _iota((1, 2 * DIFF_DQK), 1)
        for bi in range(nb):
            for j in range(nk):
                kT_scr[bi, j] = k_ref[bi, j * T:(j + 1) * T, :].astype(F32).T.astype(BF16)
            k2 = k_ref[bi].astype(F32) ** 2
            for c in range(2):
                half = (lane >= c * DIFF_DQK) & (lane < (c + 1) * DIFF_DQK)
                n2 = jnp.max(jnp.sum(jnp.where(half, k2, 0.0), axis=-1, keepdims=True), axis=0, keepdims=True)
                kn_scr[bi, c:c + 1, :] = jnp.broadcast_to(jnp.sqrt(n2), (1, 128))

    _round_robin([
        _attn_slot(bi, i, q_ref, v_ref, band_ref, bmax_ref, kT_scr, kn_scr, l_scr, acc_scr, T=T, nk=nk)
        for bi in range(nb)])

    lp = lam_ref[...]
    lc = lc_ref[...]
    lam = (jnp.exp(jnp.sum(lp[0:1] * lp[1:2], axis=-1, keepdims=True))
           - jnp.exp(jnp.sum(lp[2:3] * lp[3:4], axis=-1, keepdims=True)) + lc[:, 0:1])
    for bi in range(nb):
        @pl.when(jnp.min(jnp.sum(l_scr[bi], axis=-1, keepdims=True)) < ATTN_L_FLOOR)
        def _():
            _attn_exact(bi, i, q_ref, v_ref, band_ref, kT_scr, l_scr, acc_scr, T=T, nk=nk)

    for bi in range(nb):
        out = acc_scr[bi] / jnp.sum(l_scr[bi], axis=-1, keepdims=True)
        o = out[0:T] - lam * out[T:2 * T]
        y = _rms(o, nw_ref[...]) * lc[:, 1:2]
        o_ref[bi] = (y * _silu(g_ref[bi].astype(F32))).astype(o_ref.dtype)


def _attn(proj, band, bmax, diff_lambda, lcoef, nw, T):
    B, Lp, _ = proj.shape
    nk = Lp // T
    H = DIFF_HEADS
    assert nk >= ATTN_BAND_TILES
    nb = _batch_slots(B, 2)
    return pl.pallas_call(
        functools.partial(_attn_kernel, nb=nb, T=T, nk=nk),
        grid=(B // nb, H, nk),
        in_specs=[
            pl.BlockSpec((nb, T, 128), lambda b, h, i: (b, i, C_DQ // 128 + h)),
            pl.BlockSpec((nb, Lp, 128), lambda b, h, i: (b, 0, C_DK // 128 + h)),
            pl.BlockSpec((nb, Lp, 128), lambda b, h, i: (b, 0, C_DV // 128 + h)),
            pl.BlockSpec((nb, T, 128), lambda b, h, i: (b, i, C_DG // 128 + h)),
            pl.BlockSpec((None, 5, T, T), lambda b, h, i: (h, 0, 0, 0)),
            pl.BlockSpec((None, 1, 128), lambda b, h, i: (h, 0, 0)),
            pl.BlockSpec((4, DIFF_DQK), lambda b, h, i: (0, 0)),
            pl.BlockSpec((1, 128), lambda b, h, i: (0, 0)),
            pl.BlockSpec((1, DIFF_DV), lambda b, h, i: (0, 0)),
        ],
        out_specs=pl.BlockSpec((nb, T, 128), lambda b, h, i: (b, i, h)),
        out_shape=jax.ShapeDtypeStruct((B, Lp, DIFF_WIDTH), BF16),
        scratch_shapes=[
            pltpu.VMEM((nb, nk, 128, T), BF16),
            pltpu.VMEM((nb, 8, 128), F32),
            pltpu.VMEM((nb, 2 * T, 128), F32),
            pltpu.VMEM((nb, 2 * T, 128), F32),
        ],
        compiler_params=_params(("arbitrary", "arbitrary", "arbitrary")),
        name="diff_attn",
    )(proj, proj, proj, proj, band, bmax, diff_lambda, lcoef, nw)


def _bucket_table(T):
    nb = N_BUCKETS // 2
    max_exact = nb // 2
    rel = np.arange(6 * T) - (3 * T - 1)
    ret = np.where(rel > 0, nb, 0)
    n = np.abs(rel)
    nf = np.maximum(n, 1).astype(np.float64)
    large = max_exact + (np.log(nf / max_exact) / math.log(MAX_DISTANCE / max_exact)
                         * (nb - max_exact)).astype(np.int32)
    large = np.minimum(large, nb - 1)
    return (ret + np.where(n < max_exact, n, large)).astype(np.int32)


def _bias_band(rel_bias, T):
    n = 6 * T
    v = (rel_bias.astype(F32)[_bucket_table(T)] * LOG2E).T
    rows = jnp.tile(v, (1, T))[:, :T * (n - 1)].reshape(DIFF_HEADS, T, n - 1)
    band = rows[:, :, T - 1:n - 1].reshape(DIFF_HEADS, T, 5, T)
    return jnp.transpose(band, (0, 2, 1, 3))


def _conv_kernel(x_ref, w_ref, b_ref, o_ref):
    x = x_ref[...].astype(F32)
    w = w_ref[...]
    n = x.shape[0]
    half = (SSD_CONV - 1) // 2
    acc = x * w[half:half + 1, :] + b_ref[...]
    for kk in range(SSD_CONV):
        if kk != half:
            acc = acc + pltpu.roll(x, (half - kk) % n, 0) * w[kk:kk + 1, :]
    o_ref[...] = _silu(acc).astype(o_ref.dtype)


def _conv(proj, conv_w, conv_b):
    B, Lp, _ = proj.shape
    tn = 256
    return pl.pallas_call(
        _conv_kernel,
        grid=(B, SSD_CONV_DIM // tn),
        in_specs=[
            pl.BlockSpec((None, Lp, tn), lambda b, j: (b, 0, C_XBC // tn + j)),
            pl.BlockSpec((SSD_CONV, tn), lambda b, j: (0, j)),
            pl.BlockSpec((1, tn), lambda b, j: (0, j)),
        ],
        out_specs=pl.BlockSpec((None, Lp, tn), lambda b, j: (b, 0, j)),
        out_shape=jax.ShapeDtypeStruct((B, Lp, SSD_CONV_DIM), BF16),
        compiler_params=_params(("arbitrary", "arbitrary")),
        name="ssd_conv",
    )(proj, conv_w, conv_b.reshape(1, SSD_CONV_DIM))


def _ssd_block(bi, xc_ref, z_ref, misc_ref, dtb_r_ref, al_r_ref, e_ref, dskip_ref,
               nw_ref, o_ref, yf_scr, s_scr, *, d, blk):
    G, N, P = SSD_GROUPS, SSD_STATE, SSD_HEADDIM
    R = SSD_HEADS // G
    W = R * P
    row = _iota((BLK, 1), 0)
    lane = _iota((1, BLK), 1)
    dt = jnp.where((blk * BLK + row) >= PAD, _softplus(misc_ref[bi] + dtb_r_ref[...]), 0.0)
    a = dt * (-LOG2E * jnp.exp(al_r_ref[...]))
    aT = a.T

    r = _iota((BLK, BLK), 0)
    c = _iota((BLK, BLK), 1)
    tri = (c <= r) if d == 0 else (c >= r)
    triT = (r <= c) if d == 0 else (r >= c)
    cum = _dot_r3(jnp.where(tri, 1.0, 0.0).astype(BF16), a)
    cumT = _dot_l3(aT, jnp.where(triT, 1.0, 0.0).astype(BF16))
    yield

    lo = DT_LANE0 + SSD_HEADS * d
    hm = (lane >= lo) & (lane < lo + SSD_HEADS)
    last = BLK - 1 if d == 0 else 0
    tot = cum[last:last + 1, :]
    ecum = jnp.where(hm, jnp.exp2(cum), 0.0)
    ecum_hi = ecum.astype(BF16)
    stack = jnp.concatenate([
        jnp.where(hm, dt, 0.0).astype(BF16),
        jnp.where(hm, jnp.exp2(tot - cum), 0.0).astype(BF16),
        ecum_hi,
        (ecum - ecum_hi.astype(F32)).astype(BF16)], axis=0)
    yield

    for g in range(G):
        cols = slice(g * W, (g + 1) * W)
        ex = jnp.dot(stack, e_ref[d, :, cols], preferred_element_type=F32)
        dt_x = ex[0:BLK]
        toend_x = ex[BLK:2 * BLK]
        ecum_x = ex[2 * BLK:3 * BLK] + ex[3 * BLK:4 * BLK]
        etot_x = ecum_x[last:last + 1, :]
        xs = xc_ref[bi, :, cols].astype(F32)
        xd = xs * dt_x
        xdb = xd.astype(BF16)
        xdw = (xd * toend_x).astype(BF16)
        BgT = xc_ref[bi, :, SSD_WIDTH + g * N:SSD_WIDTH + (g + 1) * N].astype(F32).T.astype(BF16)
        Cg = xc_ref[bi, :, SSD_WIDTH + (G + g) * N:SSD_WIDTH + (G + g + 1) * N]
        CB = jnp.dot(Cg, BgT, preferred_element_type=F32)
        S = s_scr[bi, g]
        y_off = jnp.dot(Cg, S.astype(BF16), preferred_element_type=F32) * ecum_x
        s_scr[bi, g] = S * etot_x + jnp.dot(BgT, xdw, preferred_element_type=F32)
        yield
        y_pairs = []
        for pr in range(R // 2):
            h0 = g * R + 2 * pr
            pcols = slice(2 * pr * P, (2 * pr + 2) * P)
            xpair = xdb[:, pcols]
            acc = y_off[:, pcols]
            for hh in range(2):
                li = lo + h0 + hh
                seg = cum[:, li:li + 1] - cumT[li:li + 1, :]
                dec = jnp.exp2(jnp.where(tri, seg, -jnp.inf))
                sc = (CB * dec).astype(BF16)
                xm = jnp.where((lane >= hh * P) & (lane < (hh + 1) * P), xpair, 0.0)
                acc = acc + jnp.dot(sc, xm, preferred_element_type=F32)
            if d == 0:
                yf_scr[bi, blk, :, g * W + 2 * pr * P:g * W + (2 * pr + 2) * P] = acc.astype(yf_scr.dtype)
            else:
                y_pairs.append(acc)
            yield
        if d == 1:
            y = jnp.concatenate(y_pairs, axis=1)
            y = y + yf_scr[bi, blk, :, cols].astype(F32) + xs * dskip_ref[:, cols]
            y = y * _silu(z_ref[bi, :, cols].astype(F32))
            o_ref[bi, :, cols] = _rms(y, nw_ref[:, cols]).astype(o_ref.dtype)


def _ssd_kernel(*refs, nblk, nb):
    s_scr = refs[-1]
    s = pl.program_id(1)

    @pl.when((s == 0) | (s == nblk))
    def _():
        s_scr[...] = jnp.zeros_like(s_scr)

    @pl.when(s < nblk)
    def _():
        _round_robin([_ssd_block(bi, *refs, d=0, blk=s) for bi in range(nb)])

    @pl.when(s >= nblk)
    def _():
        _round_robin([_ssd_block(bi, *refs, d=1, blk=2 * nblk - 1 - s) for bi in range(nb)])


def _ssd(proj, misc, xc, dtb_r, al_r, E, dskip, nw):
    B, Lp, _ = proj.shape
    nblk = Lp // BLK
    nb = _batch_slots(B, 4)
    const2 = lambda b, s: (0, 0)
    return pl.pallas_call(
        functools.partial(_ssd_kernel, nblk=nblk, nb=nb),
        grid=(B // nb, 2 * nblk),
        in_specs=[
            pl.BlockSpec((nb, BLK, SSD_CONV_DIM), lambda b, s: (b, _scan_blk(s, nblk), 0)),
            pl.BlockSpec((nb, BLK, SSD_WIDTH), lambda b, s: (b, _scan_blk(s, nblk), C_Z // SSD_WIDTH)),
            pl.BlockSpec((nb, BLK, 128), lambda b, s: (b, _scan_blk(s, nblk), 0)),
            pl.BlockSpec((1, 128), const2),
            pl.BlockSpec((1, 128), const2),
            pl.BlockSpec((2, 128, SSD_WIDTH), lambda b, s: (0, 0, 0)),
            pl.BlockSpec((1, SSD_WIDTH), const2),
            pl.BlockSpec((1, SSD_WIDTH), const2),
        ],
        out_specs=pl.BlockSpec((nb, BLK, SSD_WIDTH), lambda b, s: (b, _scan_out_blk(s, nblk), 0)),
        out_shape=jax.ShapeDtypeStruct((B, Lp, SSD_WIDTH), BF16),
        scratch_shapes=[
            pltpu.VMEM((nb, nblk, BLK, SSD_WIDTH), BF16),
            pltpu.VMEM((nb, SSD_GROUPS, SSD_STATE, SSD_WIDTH // SSD_GROUPS), F32),
        ],
        compiler_params=_params(("arbitrary", "arbitrary")),
        name="ssd_scan",
    )(xc, proj, misc, dtb_r, al_r, E, dskip, nw)


def _permute_w_in(w):
    o = np.cumsum([0, 256, 256, 512, 512, 32, 512, 512, 512, 512, 1024, 1536, 32])
    gq, gk, gv, gg, gcode, dq, dk, dv, dg, z, xbc, dt, end = [int(v) for v in o]
    pieces = [w[..., gq:gcode], w[..., dq:dg], w[..., z:xbc], w[..., dg:z], w[..., xbc:dt],
              w[..., gcode:dq], w[..., dt:end],
              jnp.zeros(w.shape[:-1] + (N_PROJ - C_MISC - 64,), w.dtype)]
    return jnp.concatenate(pieces, axis=-1)


def _expansion():
    E = np.zeros((2, 128, SSD_WIDTH), np.float32)
    for d in range(2):
        for h in range(SSD_HEADS):
            E[d, DT_LANE0 + SSD_HEADS * d + h, h * SSD_HEADDIM:(h + 1) * SSD_HEADDIM] = 1.0
    return jnp.asarray(E, BF16)


def _misc_row(p):
    flat = p.reshape(p.shape[0], 1, 2 * SSD_HEADS).astype(F32)
    return jnp.pad(flat, ((0, 0), (0, 0), (DT_LANE0, 128 - DT_LANE0 - 2 * SSD_HEADS)))


def kernel(x, meta_tokens, rel_bias, final_norm_w, norm_w, w_in, w_out, gla_wa2, gla_ba, gla_norm_w,
           diff_lambda, diff_norm_w, conv_w, conv_b, ssd_A_log, ssd_dt_bias, ssd_D, ssd_norm_w):
    B, S, D = x.shape
    assert D == D_MODEL and S % BLK == 0
    Lp = TOK0 + S
    T = 384 if Lp % 384 == 0 else 128

    h = jnp.concatenate([
        jnp.zeros((B, PAD, D), x.dtype),
        jnp.broadcast_to(meta_tokens[None].astype(x.dtype), (B, N_META, D)),
        x], axis=1)

    band = _bias_band(rel_bias, T)
    bmax = jnp.broadcast_to((jnp.max(rel_bias.astype(F32), axis=0) * LOG2E)[:, None, None], (DIFF_HEADS, 1, 128))
    E = _expansion()
    lam_init = np.array([0.8 - 0.6 * math.exp(-0.3 * l) for l in range(DEPTH)], np.float32)
    lcoef = np.zeros((DEPTH, 1, 128), np.float32)
    lcoef[:, 0, 0] = lam_init
    lcoef[:, 0, 1] = 1.0 - lam_init

    kd = GLA_HEADS * GLA_DK
    wa_p = jnp.zeros((DEPTH, 2, 128, kd), F32)
    wa_p = wa_p.at[:, 0, 0:GLA_RANK].set(gla_wa2[:, 0].astype(F32))
    wa_p = wa_p.at[:, 1, GLA_RANK:2 * GLA_RANK].set(gla_wa2[:, 1].astype(F32))
    dtb_r = _misc_row(ssd_dt_bias)
    al_r = _misc_row(ssd_A_log)
    layers = dict(
        norm_w=norm_w,
        w_in=_permute_w_in(w_in).astype(BF16),
        w_out=w_out.astype(BF16),
        wa_p=wa_p,
        ba=gla_ba.reshape(DEPTH, 2, 1, kd).astype(F32),
        gla_nw=gla_norm_w.reshape(DEPTH, 1, GLA_DV).astype(F32),
        lam=diff_lambda.astype(F32),
        lcoef=jnp.asarray(lcoef),
        diff_nw=diff_norm_w.reshape(DEPTH, 1, DIFF_DV).astype(F32),
        conv_w=conv_w.astype(F32),
        conv_b=conv_b.astype(F32),
        dtb_r=dtb_r,
        al_r=al_r,
        dskip=jnp.repeat(ssd_D.astype(F32), SSD_HEADDIM, axis=-1).reshape(DEPTH, 1, SSD_WIDTH),
        ssd_nw=ssd_norm_w.reshape(DEPTH, 1, SSD_WIDTH).astype(F32),
    )

    def layer(h, p):
        proj, misc = _inproj(h, p["norm_w"], p["w_in"])
        o_gla = _gla(proj, misc, p["wa_p"], p["ba"], p["gla_nw"])
        o_diff = _attn(proj, band, bmax, p["lam"], p["lcoef"], p["diff_nw"], T)
        xc = _conv(proj, p["conv_w"], p["conv_b"])
        o_ssd = _ssd(proj, misc, xc, p["dtb_r"], p["al_r"], E, p["dskip"], p["ssd_nw"])
        return _outproj(h, o_gla, o_diff, o_ssd, p["w_out"]), None

    h, _ = lax.scan(layer, h, layers)
    return _final_norm(h, final_norm_w)
```

```python
import functools
import math

import numpy as np
import jax
import jax.numpy as jnp
from jax import lax
from jax.experimental import pallas as pl
from jax.experimental.pallas import tpu as pltpu

F32 = jnp.float32
BF16 = jnp.bfloat16

D_MODEL = 2048
DEPTH = 4
N_META = 16
EPS = 1e-6
GLA_HEADS = 4
GLA_DK = 64
GLA_DV = 128
GLA_WIDTH = 512
GLA_RANK = 16
GLA_TAU = 16.0
GLA_CHUNK = 64
DIFF_HEADS = 4
DIFF_DQK = 64
DIFF_DV = 128
DIFF_WIDTH = 512
N_BUCKETS = 32
MAX_DISTANCE = 128
SSD_WIDTH = 1024
SSD_HEADDIM = 64
SSD_HEADS = 16
SSD_GROUPS = 2
SSD_STATE = 128
SSD_CONV = 5
SSD_CHUNK = 128
SSD_CONV_DIM = SSD_WIDTH + 2 * SSD_GROUPS * SSD_STATE

BLK = SSD_CHUNK
assert BLK == 2 * GLA_CHUNK == 128
TOK0 = BLK
PAD = TOK0 - N_META

C_GQ, C_GK, C_GV, C_GG = 0, 256, 512, 1024
C_DQ, C_DK, C_DV = 1536, 2048, 2560
C_Z, C_DG, C_XBC, C_MISC = 3072, 4096, 4608, 6144
N_PROJ = 6272
DT_LANE0 = 2 * GLA_RANK

VMEM_LIMIT = 56 * 1024 * 1024
LOG2E = 1.4426950408889634


def _params(sem, limit=VMEM_LIMIT):
    return pltpu.CompilerParams(dimension_semantics=sem, vmem_limit_bytes=limit)


def _silu(x):
    return x / (1.0 + jnp.exp(-x))


def _softplus(x):
    return jnp.maximum(x, 0.0) + jnp.log(1.0 + jnp.exp(-jnp.abs(x)))


def _log_sigmoid(x):
    return jnp.minimum(x, 0.0) - jnp.log(1.0 + jnp.exp(-jnp.abs(x)))


def _split3(x):
    hi = x.astype(BF16)
    r = x - hi.astype(F32)
    mid = r.astype(BF16)
    lo = (r - mid.astype(F32)).astype(BF16)
    return hi, mid, lo


def _dot_l3(a, b_exact):
    hi, mid, lo = _split3(a)
    d = functools.partial(jnp.dot, preferred_element_type=F32)
    return d(hi, b_exact) + d(mid, b_exact) + d(lo, b_exact)


def _dot_r3(a_exact, b):
    hi, mid, lo = _split3(b)
    d = functools.partial(jnp.dot, preferred_element_type=F32)
    return d(a_exact, hi) + d(a_exact, mid) + d(a_exact, lo)


def _dot_22(a, b):
    ah = a.astype(BF16)
    al = (a - ah.astype(F32)).astype(BF16)
    bh = b.astype(BF16)
    bl = (b - bh.astype(F32)).astype(BF16)
    d = functools.partial(jnp.dot, preferred_element_type=F32)
    return d(ah, bh) + d(ah, bl) + d(al, bh)


def _iota(shape, dim):
    return lax.broadcasted_iota(jnp.int32, shape, dim)


def _rms(x, w):
    return x * lax.rsqrt(jnp.mean(x * x, axis=-1, keepdims=True) + EPS) * w


def _inproj_kernel(h_ref, nw_ref, w_ref, wm_ref, o_ref, om_ref, u_scr, *, tm):
    @pl.when(pl.program_id(2) == 0)
    def _():
        x = h_ref[...]
        y = _rms(x, nw_ref[...])
        row = pl.program_id(1) * tm + _iota((tm, 1), 0)
        u_scr[...] = jnp.where(row >= PAD, y, 0.0).astype(BF16)
        om_ref[...] = jnp.dot(u_scr[...], wm_ref[...], preferred_element_type=F32)

    o_ref[...] = jnp.dot(u_scr[...], w_ref[...], preferred_element_type=F32).astype(o_ref.dtype)


def _inproj(h, norm_w, w_p):
    B, Lp, D = h.shape
    tm = Lp // 4
    tn = C_MISC // 4
    return pl.pallas_call(
        functools.partial(_inproj_kernel, tm=tm),
        grid=(B, Lp // tm, C_MISC // tn),
        in_specs=[
            pl.BlockSpec((None, tm, D), lambda b, i, j: (b, i, 0)),
            pl.BlockSpec((1, D), lambda b, i, j: (0, 0)),
            pl.BlockSpec((D, tn), lambda b, i, j: (0, j)),
            pl.BlockSpec((D, 128), lambda b, i, j: (0, C_MISC // 128)),
        ],
        out_specs=[
            pl.BlockSpec((None, tm, tn), lambda b, i, j: (b, i, j)),
            pl.BlockSpec((None, tm, 128), lambda b, i, j: (b, i, 0)),
        ],
        out_shape=[jax.ShapeDtypeStruct((B, Lp, C_MISC), BF16),
                   jax.ShapeDtypeStruct((B, Lp, 128), F32)],
        scratch_shapes=[pltpu.VMEM((tm, D), BF16)],
        compiler_params=_params(("arbitrary", "arbitrary", "arbitrary")),
        name="inproj",
    )(h, norm_w.reshape(1, D), w_p, w_p)


def _outproj_kernel(h_ref, a_ref, b_ref, c_ref, w_ref, o_ref):
    d = functools.partial(jnp.dot, preferred_element_type=F32)
    acc = d(a_ref[...], w_ref[0:GLA_WIDTH, :])
    acc += d(b_ref[...], w_ref[GLA_WIDTH:GLA_WIDTH + DIFF_WIDTH, :])
    acc += d(c_ref[...], w_ref[GLA_WIDTH + DIFF_WIDTH:, :])
    o_ref[...] = h_ref[...] + acc


def _outproj(h, o_gla, o_diff, o_ssd, w_out):
    B, Lp, D = h.shape
    tm = Lp // 8 if (Lp // 8) % 16 == 0 else Lp // 4
    tn = D
    return pl.pallas_call(
        _outproj_kernel,
        grid=(B, Lp // tm, D // tn),
        in_specs=[
            pl.BlockSpec((None, tm, tn), lambda b, i, j: (b, i, j)),
            pl.BlockSpec((None, tm, GLA_WIDTH), lambda b, i, j: (b, i, 0)),
            pl.BlockSpec((None, tm, DIFF_WIDTH), lambda b, i, j: (b, i, 0)),
            pl.BlockSpec((None, tm, SSD_WIDTH), lambda b, i, j: (b, i, 0)),
            pl.BlockSpec((D, tn), lambda b, i, j: (0, j)),
        ],
        out_specs=pl.BlockSpec((None, tm, tn), lambda b, i, j: (b, i, j)),
        out_shape=jax.ShapeDtypeStruct((B, Lp, D), F32),
        input_output_aliases={0: 0},
        compiler_params=_params(("arbitrary", "arbitrary", "arbitrary")),
        name="outproj",
    )(h, o_gla, o_diff, o_ssd, w_out)


def _final_kernel(h_ref, w_ref, o_ref):
    o_ref[...] = _rms(h_ref[...], w_ref[...])


def _final_norm(h, w):
    B, Lp, D = h.shape
    S = Lp - TOK0
    rb = 512 if S % 512 == 0 else BLK
    return pl.pallas_call(
        _final_kernel,
        grid=(B, S // rb),
        in_specs=[
            pl.BlockSpec((pl.Element(rb), pl.Element(D)),
                         lambda b, i: (pl.multiple_of(b * Lp + TOK0 + i * rb, BLK), 0)),
            pl.BlockSpec((1, D), lambda b, i: (0, 0)),
        ],
        out_specs=pl.BlockSpec((None, rb, D), lambda b, i: (b, i, 0)),
        out_shape=jax.ShapeDtypeStruct((B, S, D), F32),
        compiler_params=_params(("arbitrary", "arbitrary")),
        name="final_norm",
    )(h.reshape(B * Lp, D), w.reshape(1, D))


def _round_robin(gens):
    alive = list(gens)
    while alive:
        for g in list(alive):
            try:
                next(g)
            except StopIteration:
                alive.remove(g)


def _gla_block(bi, q_ref, k_ref, v_ref, gate_ref, misc_ref, wa_ref, ba_ref, nw_ref, o_ref,
               of_scr, s_scr, *, d, blk):
    C = GLA_CHUNK
    row = _iota((BLK, 1), 0)
    valid = (blk * BLK + row) >= PAD
    x = _dot_22(misc_ref[bi], wa_ref[d]) + ba_ref[d]
    yield
    g = jnp.where(valid, _log_sigmoid(x) * (LOG2E / GLA_TAU), 0.0)

    r = _iota((BLK, BLK), 0)
    c = _iota((BLK, BLK), 1)
    same = (r >= C) == (c >= C)
    tri = (same & (c <= r)) if d == 0 else (same & (c >= r))
    b = _dot_r3(jnp.where(tri, 1.0, 0.0).astype(BF16), g)
    yield
    first, second = (C - 1, 2 * C - 1) if d == 0 else (0, C)

    k = k_ref[bi].astype(F32)
    q_in = q_ref[bi].astype(F32) * (GLA_DK ** -0.5) * jnp.exp2(b)
    kT = k.T
    bT = b.T
    lane = _iota((1, BLK), 1)
    blastT = jnp.where(lane < C, bT[:, first:first + 1], bT[:, second:second + 1])
    kinT = (kT * jnp.exp2(-bT)).astype(BF16)
    koutT = kT * jnp.exp2(blastT - bT)
    vb = v_ref[bi]
    yield

    lane_k = _iota((1, GLA_HEADS * GLA_DK), 1)
    qm = [jnp.where((lane_k >= h * GLA_DK) & (lane_k < (h + 1) * GLA_DK), q_in, 0.0).astype(BF16)
          for h in range(GLA_HEADS)]
    att = [jnp.where(tri, jnp.dot(qm[h], kinT, preferred_element_type=F32), 0.0).astype(BF16)
           for h in range(GLA_HEADS)]
    yield
    o_intra = [jnp.dot(att[h], vb[:, h * GLA_DV:(h + 1) * GLA_DV], preferred_element_type=F32)
               for h in range(GLA_HEADS)]
    yield

    S = s_scr[bi]
    o_inter = [[None, None] for _ in range(GLA_HEADS)]
    for cc in ((0, 1) if d == 0 else (1, 0)):
        Sb = S.astype(BF16)
        for h in range(GLA_HEADS):
            o_inter[h][cc] = jnp.dot(qm[h][cc * C:(cc + 1) * C, :], Sb, preferred_element_type=F32)
        tot = first if cc == 0 else second
        dec = jnp.exp2(bT[:, tot:tot + 1])
        kc = jnp.where((lane >= cc * C) & (lane < (cc + 1) * C), koutT, 0.0).astype(BF16)
        upd = [jnp.dot(kc[h * GLA_DK:(h + 1) * GLA_DK, :], vb[:, h * GLA_DV:(h + 1) * GLA_DV],
                       preferred_element_type=F32) for h in range(GLA_HEADS)]
        S = S * dec + jnp.concatenate(upd, axis=0)
        yield
    s_scr[bi] = S

    o = jnp.concatenate(
        [o_intra[h] + jnp.concatenate(o_inter[h], axis=0) for h in range(GLA_HEADS)], axis=1)
    if d == 0:
        of_scr[bi, blk] = o.astype(of_scr.dtype)
    else:
        o = o + of_scr[bi, blk].astype(F32)
        gate = gate_ref[bi].astype(F32)
        nw = nw_ref[...]
        outs = []
        for h in range(GLA_HEADS):
            sl = slice(h * GLA_DV, (h + 1) * GLA_DV)
            outs.append(_rms(o[:, sl], nw) * _silu(gate[:, sl]))
        o_ref[bi] = jnp.concatenate(outs, axis=1).astype(o_ref.dtype)


def _gla_kernel(*refs, nblk, nb):
    s_scr = refs[-1]
    s = pl.program_id(1)

    @pl.when((s == 0) | (s == nblk))
    def _():
        s_scr[...] = jnp.zeros_like(s_scr)

    @pl.when(s < nblk)
    def _():
        _round_robin([_gla_block(bi, *refs, d=0, blk=s) for bi in range(nb)])

    @pl.when(s >= nblk)
    def _():
        _round_robin([_gla_block(bi, *refs, d=1, blk=2 * nblk - 1 - s) for bi in range(nb)])


def _scan_blk(s, nblk):
    return jnp.where(s < nblk, s, 2 * nblk - 1 - s)


def _scan_out_blk(s, nblk):
    return jnp.where(s < nblk, nblk - 1, 2 * nblk - 1 - s)


def _batch_slots(B, want):
    nb = want
    while B % nb:
        nb -= 1
    return nb


def _gla(proj, misc, wa_p, ba, nw):
    B, Lp, _ = proj.shape
    nblk = Lp // BLK
    nb = _batch_slots(B, 8)
    kd = GLA_HEADS * GLA_DK
    im = lambda col: (lambda b, s: (b, _scan_blk(s, nblk), col))
    return pl.pallas_call(
        functools.partial(_gla_kernel, nblk=nblk, nb=nb),
        grid=(B // nb, 2 * nblk),
        in_specs=[
            pl.BlockSpec((nb, BLK, kd), im(C_GQ // kd)),
            pl.BlockSpec((nb, BLK, kd), im(C_GK // kd)),
            pl.BlockSpec((nb, BLK, GLA_WIDTH), im(C_GV // GLA_WIDTH)),
            pl.BlockSpec((nb, BLK, GLA_WIDTH), im(C_GG // GLA_WIDTH)),
            pl.BlockSpec((nb, BLK, 128), im(0)),
            pl.BlockSpec((2, 128, kd), lambda b, s: (0, 0, 0)),
            pl.BlockSpec((2, 1, kd), lambda b, s: (0, 0, 0)),
            pl.BlockSpec((1, GLA_DV), lambda b, s: (0, 0)),
        ],
        out_specs=pl.BlockSpec((nb, BLK, GLA_WIDTH), lambda b, s: (b, _scan_out_blk(s, nblk), 0)),
        out_shape=jax.ShapeDtypeStruct((B, Lp, GLA_WIDTH), BF16),
        scratch_shapes=[
            pltpu.VMEM((nb, nblk, BLK, GLA_WIDTH), BF16),
            pltpu.VMEM((nb, kd, GLA_DV), F32),
        ],
        compiler_params=_params(("arbitrary", "arbitrary")),
        name="gla",
    )(proj, proj, proj, proj, misc, wa_p, ba, nw)


ATTN_BAND_TILES = 3
ATTN_L_FLOOR = 2.0 ** -60


def _q_halves(q_ref, bi):
    lane = _iota((1, 2 * DIFF_DQK), 1)
    q = q_ref[bi].astype(F32) * (DIFF_DQK ** -0.5 * LOG2E)
    return [jnp.where((lane >= c * DIFF_DQK) & (lane < (c + 1) * DIFF_DQK), q, 0.0).astype(BF16)
            for c in range(2)]


def _attn_slot(bi, i, q_ref, v_ref, band_ref, bmax_ref, kT_scr, kn_scr, l_scr, acc_scr, *, T, nk):
    sub = T // 128
    qb = _q_halves(q_ref, bi)
    qstack = jnp.concatenate(qb, axis=0)
    kn = kn_scr[bi]
    m = jnp.concatenate([
        jnp.sqrt(jnp.sum(x.astype(F32) ** 2, axis=-1, keepdims=True)) * kn[c:c + 1, :] + bmax_ref[...]
        for c, x in enumerate(qb)], axis=0)
    off_l = m - band_ref[0, 0:1, 0:128]
    off_r = m - band_ref[4, 0:1, 0:128]
    b0 = jnp.clip(i - 1, 0, nk - ATTN_BAND_TILES)
    padrow = jnp.where(_iota((1, 128), 1) < PAD, 3.0e4, 0.0)
    l_scr[bi] = jnp.zeros(l_scr.shape[1:], F32)
    acc_scr[bi] = jnp.zeros(acc_scr.shape[1:], F32)

    def sub_tile(j, t, off, bidx):
        cols = slice(t * 128, (t + 1) * 128)
        x = jnp.dot(qstack, kT_scr[bi, j, :, cols], preferred_element_type=F32) - off
        if bidx is not None:
            bt = band_ref[bidx, :, cols]
            x = x + jnp.concatenate([bt, bt], axis=0)
        if t == 0:
            x = x - jnp.where(j == 0, padrow, 0.0)
        e = jnp.exp2(x)
        l_scr[bi] += e
        vt = v_ref[bi, pl.ds(pl.multiple_of(j * T + t * 128, 128), 128), :]
        acc_scr[bi] += jnp.dot(e.astype(BF16), vt, preferred_element_type=F32)

    for jj in range(nk - ATTN_BAND_TILES):
        j = jj + jnp.where(jj >= b0, ATTN_BAND_TILES, 0)
        off = jnp.where(j < i, off_l, off_r)
        for t in range(sub):
            sub_tile(j, t, off, None)
            yield
    for b in range(ATTN_BAND_TILES):
        j = b0 + b
        for t in range(sub):
            sub_tile(j, t, m, j - i + 2)
            yield


def _attn_exact(bi, i, q_ref, v_ref, band_ref, kT_scr, l_scr, acc_scr, *, T, nk):
    sub = T // 128
    qb = _q_halves(q_ref, bi)
    col = _iota((1, T), 1)

    def fold(x, op):
        r = x[:, 0:128]
        for t in range(1, sub):
            r = op(r, x[:, t * 128:(t + 1) * 128])
        return r

    for c in range(2):
        for r0 in range(0, T, 128):
            rows = slice(r0, r0 + 128)
            qc = qb[c][rows]

            def scores(j):
                s = jnp.dot(qc, kT_scr[bi, j], preferred_element_type=F32)
                s = s + band_ref[jnp.clip(j - i, -2, 2) + 2, rows, :]
                return jnp.where((j == 0) & (col < PAD), -1e30, s)

            m_run = lax.fori_loop(0, nk, lambda j, mr: jnp.maximum(mr, fold(scores(j), jnp.maximum)),
                                  jnp.full((128, 128), -jnp.inf, F32))
            m = jnp.max(m_run, axis=-1, keepdims=True)

            def body(j, carry):
                l_run, acc = carry
                e = jnp.exp2(scores(j) - m)
                vt = v_ref[bi, pl.ds(pl.multiple_of(j * T, 128), T), :]
                return l_run + fold(e, jnp.add), acc + jnp.dot(e.astype(BF16), vt, preferred_element_type=F32)

            l_run, acc = lax.fori_loop(0, nk, body, (jnp.zeros((128, 128), F32), jnp.zeros((128, 128), F32)))
            l_scr[bi, c * T + r0:c * T + r0 + 128, :] = l_run
            acc_scr[bi, c * T + r0:c * T + r0 + 128, :] = acc


def _attn_kernel(q_ref, k_ref, v_ref, g_ref, band_ref, bmax_ref, lam_ref, lc_ref, nw_ref, o_ref,
                 kT_scr, kn_scr, l_scr, acc_scr, *, nb, T, nk):
    i = pl.program_id(2)

    @pl.when(i == 0)
    def _():
        lane = _iota((1, 2 * DIFF_DQK), 1)
        w = 2 * DIFF_DQK
        same_half = ((_iota((w, w), 0) < DIFF_DQK) == (_iota((w, w), 1) < DIFF_DQK))
        ones = jnp.where(same_half, 1.0, 0.0).astype(BF16)
        for bi in range(nb):
            for j in range(nk):
                kT_scr[bi, j] = k_ref[bi, j * T:(j + 1) * T, :].T
            k2 = k_ref[bi].astype(F32) ** 2
            hi = k2.astype(BF16)
            lo = (k2 - hi.astype(F32)).astype(BF16)
            n2 = (jnp.dot(hi, ones, preferred_element_type=F32)
                  + jnp.dot(lo, ones, preferred_element_type=F32))
            nmax = jnp.max(n2, axis=0, keepdims=True)
            for c in range(2):
                half = (lane >= c * DIFF_DQK) & (lane < (c + 1) * DIFF_DQK)
                kmax = jnp.sqrt(jnp.max(jnp.where(half, nmax, 0.0), axis=-1, keepdims=True)) * (1.0 + 2.0 ** -10)
                kn_scr[bi, c:c + 1, :] = jnp.broadcast_to(kmax, (1, 128))

    _round_robin([
        _attn_slot(bi, i, q_ref, v_ref, band_ref, bmax_ref, kT_scr, kn_scr, l_scr, acc_scr, T=T, nk=nk)
        for bi in range(nb)])

    lp = lam_ref[...]
    lc = lc_ref[...]
    lam = (jnp.exp(jnp.sum(lp[0:1] * lp[1:2], axis=-1, keepdims=True))
           - jnp.exp(jnp.sum(lp[2:3] * lp[3:4], axis=-1, keepdims=True)) + lc[:, 0:1])
    l_min = None
    for bi in range(nb):
        lt = jnp.min(l_scr[bi], axis=0, keepdims=True)
        l_min = lt if l_min is None else jnp.minimum(l_min, lt)

    @pl.when(jnp.min(l_min) * 128.0 < ATTN_L_FLOOR)
    def _():
        for bi in range(nb):
            _attn_exact(bi, i, q_ref, v_ref, band_ref, kT_scr, l_scr, acc_scr, T=T, nk=nk)

    for bi in range(nb):
        out = acc_scr[bi] / jnp.sum(l_scr[bi], axis=-1, keepdims=True)
        o = out[0:T] - lam * out[T:2 * T]
        y = _rms(o, nw_ref[...]) * lc[:, 1:2]
        o_ref[bi] = (y * _silu(g_ref[bi].astype(F32))).astype(o_ref.dtype)


def _attn(proj, band, bmax, diff_lambda, lcoef, nw, T):
    B, Lp, _ = proj.shape
    nk = Lp // T
    H = DIFF_HEADS
    assert nk >= ATTN_BAND_TILES
    nb = _batch_slots(B, 2)
    return pl.pallas_call(
        functools.partial(_attn_kernel, nb=nb, T=T, nk=nk),
        grid=(B // nb, H, nk),
        in_specs=[
            pl.BlockSpec((nb, T, 128), lambda b, h, i: (b, i, C_DQ // 128 + h)),
            pl.BlockSpec((nb, Lp, 128), lambda b, h, i: (b, 0, C_DK // 128 + h)),
            pl.BlockSpec((nb, Lp, 128), lambda b, h, i: (b, 0, C_DV // 128 + h)),
            pl.BlockSpec((nb, T, 128), lambda b, h, i: (b, i, C_DG // 128 + h)),
            pl.BlockSpec((None, 5, T, T), lambda b, h, i: (h, 0, 0, 0)),
            pl.BlockSpec((None, 1, 128), lambda b, h, i: (h, 0, 0)),
            pl.BlockSpec((4, DIFF_DQK), lambda b, h, i: (0, 0)),
            pl.BlockSpec((1, 128), lambda b, h, i: (0, 0)),
            pl.BlockSpec((1, DIFF_DV), lambda b, h, i: (0, 0)),
        ],
        out_specs=pl.BlockSpec((nb, T, 128), lambda b, h, i: (b, i, h)),
        out_shape=jax.ShapeDtypeStruct((B, Lp, DIFF_WIDTH), BF16),
        scratch_shapes=[
            pltpu.VMEM((nb, nk, 128, T), BF16),
            pltpu.VMEM((nb, 8, 128), F32),
            pltpu.VMEM((nb, 2 * T, 128), F32),
            pltpu.VMEM((nb, 2 * T, 128), F32),
        ],
        compiler_params=_params(("arbitrary", "arbitrary", "arbitrary")),
        name="diff_attn",
    )(proj, proj, proj, proj, band, bmax, diff_lambda, lcoef, nw)


def _bucket_table(T):
    nb = N_BUCKETS // 2
    max_exact = nb // 2
    rel = np.arange(6 * T) - (3 * T - 1)
    ret = np.where(rel > 0, nb, 0)
    n = np.abs(rel)
    nf = np.maximum(n, 1).astype(np.float64)
    large = max_exact + (np.log(nf / max_exact) / math.log(MAX_DISTANCE / max_exact)
                         * (nb - max_exact)).astype(np.int32)
    large = np.minimum(large, nb - 1)
    return (ret + np.where(n < max_exact, n, large)).astype(np.int32)


def _bias_band(rel_bias, T):
    n = 6 * T
    v = (rel_bias.astype(F32)[_bucket_table(T)] * LOG2E).T
    rows = jnp.tile(v, (1, T))[:, :T * (n - 1)].reshape(DIFF_HEADS, T, n - 1)
    band = rows[:, :, T - 1:n - 1].reshape(DIFF_HEADS, T, 5, T)
    return jnp.transpose(band, (0, 2, 1, 3))


def _conv_kernel(x_ref, w_ref, b_ref, o_ref):
    x = x_ref[...].astype(F32)
    w = w_ref[...]
    n = x.shape[0]
    half = (SSD_CONV - 1) // 2
    acc = x * w[half:half + 1, :] + b_ref[...]
    for kk in range(SSD_CONV):
        if kk != half:
            acc = acc + pltpu.roll(x, (half - kk) % n, 0) * w[kk:kk + 1, :]
    o_ref[...] = _silu(acc).astype(o_ref.dtype)


def _conv(proj, conv_w, conv_b):
    B, Lp, _ = proj.shape
    tn = 256
    return pl.pallas_call(
        _conv_kernel,
        grid=(B, SSD_CONV_DIM // tn),
        in_specs=[
            pl.BlockSpec((None, Lp, tn), lambda b, j: (b, 0, C_XBC // tn + j)),
            pl.BlockSpec((SSD_CONV, tn), lambda b, j: (0, j)),
            pl.BlockSpec((1, tn), lambda b, j: (0, j)),
        ],
        out_specs=pl.BlockSpec((None, Lp, tn), lambda b, j: (b, 0, j)),
        out_shape=jax.ShapeDtypeStruct((B, Lp, SSD_CONV_DIM), BF16),
        compiler_params=_params(("arbitrary", "arbitrary")),
        name="ssd_conv",
    )(proj, conv_w, conv_b.reshape(1, SSD_CONV_DIM))


def _ssd_block(bi, xc_ref, z_ref, misc_ref, dtb_r_ref, al_r_ref, e_ref, dskip_ref,
               nw_ref, o_ref, yf_scr, s_scr, *, d, blk):
    G, N, P = SSD_GROUPS, SSD_STATE, SSD_HEADDIM
    R = SSD_HEADS // G
    W = R * P
    row = _iota((BLK, 1), 0)
    lane = _iota((1, BLK), 1)
    dt = jnp.where((blk * BLK + row) >= PAD, _softplus(misc_ref[bi] + dtb_r_ref[...]), 0.0)
    a = dt * (-LOG2E * jnp.exp(al_r_ref[...]))
    aT = a.T

    r = _iota((BLK, BLK), 0)
    c = _iota((BLK, BLK), 1)
    tri = (c <= r) if d == 0 else (c >= r)
    triT = (r <= c) if d == 0 else (r >= c)
    cum = _dot_r3(jnp.where(tri, 1.0, 0.0).astype(BF16), a)
    cumT = _dot_l3(aT, jnp.where(triT, 1.0, 0.0).astype(BF16))
    yield

    lo = DT_LANE0 + SSD_HEADS * d
    hm = (lane >= lo) & (lane < lo + SSD_HEADS)
    last = BLK - 1 if d == 0 else 0
    tot = cum[last:last + 1, :]
    ecum = jnp.where(hm, jnp.exp2(cum), 0.0)
    ecum_hi = ecum.astype(BF16)
    stack = jnp.concatenate([
        jnp.where(hm, dt, 0.0).astype(BF16),
        jnp.where(hm, jnp.exp2(tot - cum), 0.0).astype(BF16),
        ecum_hi,
        (ecum - ecum_hi.astype(F32)).astype(BF16)], axis=0)
    yield

    for g in range(G):
        cols = slice(g * W, (g + 1) * W)
        ex = jnp.dot(stack, e_ref[d, :, cols], preferred_element_type=F32)
        dt_x = ex[0:BLK]
        toend_x = ex[BLK:2 * BLK]
        ecum_x = ex[2 * BLK:3 * BLK] + ex[3 * BLK:4 * BLK]
        etot_x = ecum_x[last:last + 1, :]
        xs = xc_ref[bi, :, cols].astype(F32)
        xd = xs * dt_x
        xdb = xd.astype(BF16)
        xdw = (xd * toend_x).astype(BF16)
        BgT = xc_ref[bi, :, SSD_WIDTH + g * N:SSD_WIDTH + (g + 1) * N].astype(F32).T.astype(BF16)
        Cg = xc_ref[bi, :, SSD_WIDTH + (G + g) * N:SSD_WIDTH + (G + g + 1) * N]
        CB = jnp.dot(Cg, BgT, preferred_element_type=F32)
        S = s_scr[bi, g]
        y_off = jnp.dot(Cg, S.astype(BF16), preferred_element_type=F32) * ecum_x
        s_scr[bi, g] = S * etot_x + jnp.dot(BgT, xdw, preferred_element_type=F32)
        yield
        y_pairs = []
        for pr in range(R // 2):
            h0 = g * R + 2 * pr
            pcols = slice(2 * pr * P, (2 * pr + 2) * P)
            xpair = xdb[:, pcols]
            acc = y_off[:, pcols]
            for hh in range(2):
                li = lo + h0 + hh
                seg = cum[:, li:li + 1] - cumT[li:li + 1, :]
                dec = jnp.exp2(jnp.where(tri, seg, -jnp.inf))
                sc = (CB * dec).astype(BF16)
                xm = jnp.where((lane >= hh * P) & (lane < (hh + 1) * P), xpair, 0.0)
                acc = acc + jnp.dot(sc, xm, preferred_element_type=F32)
            if d == 0:
                yf_scr[bi, blk, :, g * W + 2 * pr * P:g * W + (2 * pr + 2) * P] = acc.astype(yf_scr.dtype)
            else:
                y_pairs.append(acc)
            yield
        if d == 1:
            y = jnp.concatenate(y_pairs, axis=1)
            y = y + yf_scr[bi, blk, :, cols].astype(F32) + xs * dskip_ref[:, cols]
            y = y * _silu(z_ref[bi, :, cols].astype(F32))
            o_ref[bi, :, cols] = _rms(y, nw_ref[:, cols]).astype(o_ref.dtype)


def _ssd_kernel(*refs, nblk, nb):
    s_scr = refs[-1]
    s = pl.program_id(1)

    @pl.when((s == 0) | (s == nblk))
    def _():
        s_scr[...] = jnp.zeros_like(s_scr)

    @pl.when(s < nblk)
    def _():
        _round_robin([_ssd_block(bi, *refs, d=0, blk=s) for bi in range(nb)])

    @pl.when(s >= nblk)
    def _():
        _round_robin([_ssd_block(bi, *refs, d=1, blk=2 * nblk - 1 - s) for bi in range(nb)])


def _ssd(proj, misc, xc, dtb_r, al_r, E, dskip, nw):
    B, Lp, _ = proj.shape
    nblk = Lp // BLK
    nb = _batch_slots(B, 4)
    const2 = lambda b, s: (0, 0)
    return pl.pallas_call(
        functools.partial(_ssd_kernel, nblk=nblk, nb=nb),
        grid=(B // nb, 2 * nblk),
        in_specs=[
            pl.BlockSpec((nb, BLK, SSD_CONV_DIM), lambda b, s: (b, _scan_blk(s, nblk), 0)),
            pl.BlockSpec((nb, BLK, SSD_WIDTH), lambda b, s: (b, _scan_blk(s, nblk), C_Z // SSD_WIDTH)),
            pl.BlockSpec((nb, BLK, 128), lambda b, s: (b, _scan_blk(s, nblk), 0)),
            pl.BlockSpec((1, 128), const2),
            pl.BlockSpec((1, 128), const2),
            pl.BlockSpec((2, 128, SSD_WIDTH), lambda b, s: (0, 0, 0)),
            pl.BlockSpec((1, SSD_WIDTH), const2),
            pl.BlockSpec((1, SSD_WIDTH), const2),
        ],
        out_specs=pl.BlockSpec((nb, BLK, SSD_WIDTH), lambda b, s: (b, _scan_out_blk(s, nblk), 0)),
        out_shape=jax.ShapeDtypeStruct((B, Lp, SSD_WIDTH), BF16),
        scratch_shapes=[
            pltpu.VMEM((nb, nblk, BLK, SSD_WIDTH), BF16),
            pltpu.VMEM((nb, SSD_GROUPS, SSD_STATE, SSD_WIDTH // SSD_GROUPS), F32),
        ],
        compiler_params=_params(("arbitrary", "arbitrary")),
        name="ssd_scan",
    )(xc, proj, misc, dtb_r, al_r, E, dskip, nw)


def _permute_w_in(w):
    o = np.cumsum([0, 256, 256, 512, 512, 32, 512, 512, 512, 512, 1024, 1536, 32])
    gq, gk, gv, gg, gcode, dq, dk, dv, dg, z, xbc, dt, end = [int(v) for v in o]
    pieces = [w[..., gq:gcode], w[..., dq:dg], w[..., z:xbc], w[..., dg:z], w[..., xbc:dt],
              w[..., gcode:dq], w[..., dt:end],
              jnp.zeros(w.shape[:-1] + (N_PROJ - C_MISC - 64,), w.dtype)]
    return jnp.concatenate(pieces, axis=-1)


def _expansion():
    E = np.zeros((2, 128, SSD_WIDTH), np.float32)
    for d in range(2):
        for h in range(SSD_HEADS):
            E[d, DT_LANE0 + SSD_HEADS * d + h, h * SSD_HEADDIM:(h + 1) * SSD_HEADDIM] = 1.0
    return jnp.asarray(E, BF16)


def _misc_row(p):
    flat = p.reshape(p.shape[0], 1, 2 * SSD_HEADS).astype(F32)
    return jnp.pad(flat, ((0, 0), (0, 0), (DT_LANE0, 128 - DT_LANE0 - 2 * SSD_HEADS)))


def kernel(x, meta_tokens, rel_bias, final_norm_w, norm_w, w_in, w_out, gla_wa2, gla_ba, gla_norm_w,
           diff_lambda, diff_norm_w, conv_w, conv_b, ssd_A_log, ssd_dt_bias, ssd_D, ssd_norm_w):
    B, S, D = x.shape
    assert D == D_MODEL and S % BLK == 0
    Lp = TOK0 + S
    T = 384 if Lp % 384 == 0 else 128

    h = jnp.concatenate([
        jnp.zeros((B, PAD, D), x.dtype),
        jnp.broadcast_to(meta_tokens[None].astype(x.dtype), (B, N_META, D)),
        x], axis=1)

    band = _bias_band(rel_bias, T)
    bmax = jnp.broadcast_to((jnp.max(rel_bias.astype(F32), axis=0) * LOG2E)[:, None, None], (DIFF_HEADS, 1, 128))
    E = _expansion()
    lam_init = np.array([0.8 - 0.6 * math.exp(-0.3 * l) for l in range(DEPTH)], np.float32)
    lcoef = np.zeros((DEPTH, 1, 128), np.float32)
    lcoef[:, 0, 0] = lam_init
    lcoef[:, 0, 1] = 1.0 - lam_init

    kd = GLA_HEADS * GLA_DK
    wa_p = jnp.zeros((DEPTH, 2, 128, kd), F32)
    wa_p = wa_p.at[:, 0, 0:GLA_RANK].set(gla_wa2[:, 0].astype(F32))
    wa_p = wa_p.at[:, 1, GLA_RANK:2 * GLA_RANK].set(gla_wa2[:, 1].astype(F32))
    dtb_r = _misc_row(ssd_dt_bias)
    al_r = _misc_row(ssd_A_log)
    layers = dict(
        norm_w=norm_w,
        w_in=_permute_w_in(w_in).astype(BF16),
        w_out=w_out.astype(BF16),
        wa_p=wa_p,
        ba=gla_ba.reshape(DEPTH, 2, 1, kd).astype(F32),
        gla_nw=gla_norm_w.reshape(DEPTH, 1, GLA_DV).astype(F32),
        lam=diff_lambda.astype(F32),
        lcoef=jnp.asarray(lcoef),
        diff_nw=diff_norm_w.reshape(DEPTH, 1, DIFF_DV).astype(F32),
        conv_w=conv_w.astype(F32),
        conv_b=conv_b.astype(F32),
        dtb_r=dtb_r,
        al_r=al_r,
        dskip=jnp.repeat(ssd_D.astype(F32), SSD_HEADDIM, axis=-1).reshape(DEPTH, 1, SSD_WIDTH),
        ssd_nw=ssd_norm_w.reshape(DEPTH, 1, SSD_WIDTH).astype(F32),
    )

    def layer(h, p):
        proj, misc = _inproj(h, p["norm_w"], p["w_in"])
        o_gla = _gla(proj, misc, p["wa_p"], p["ba"], p["gla_nw"])
        o_diff = _attn(proj, band, bmax, p["lam"], p["lcoef"], p["diff_nw"], T)
        xc = _conv(proj, p["conv_w"], p["conv_b"])
        o_ssd = _ssd(proj, misc, xc, p["dtb_r"], p["al_r"], E, p["dskip"], p["ssd_nw"])
        return _outproj(h, o_gla, o_diff, o_ssd, p["w_out"]), None

    h, _ = lax.scan(layer, h, layers)
    return _final_norm(h, final_norm_w)
```

```python
import functools
import math

import numpy as np
import jax
import jax.numpy as jnp
from jax import lax
from jax.experimental import pallas as pl
from jax.experimental.pallas import tpu as pltpu

F32 = jnp.float32
BF16 = jnp.bfloat16

D_MODEL = 2048
DEPTH = 4
N_META = 16
EPS = 1e-6
GLA_HEADS = 4
GLA_DK = 64
GLA_DV = 128
GLA_WIDTH = 512
GLA_RANK = 16
GLA_TAU = 16.0
GLA_CHUNK = 64
DIFF_HEADS = 4
DIFF_DQK = 64
DIFF_DV = 128
DIFF_WIDTH = 512
N_BUCKETS = 32
MAX_DISTANCE = 128
SSD_WIDTH = 1024
SSD_HEADDIM = 64
SSD_HEADS = 16
SSD_GROUPS = 2
SSD_STATE = 128
SSD_CONV = 5
SSD_CHUNK = 128
SSD_CONV_DIM = SSD_WIDTH + 2 * SSD_GROUPS * SSD_STATE

BLK = SSD_CHUNK
assert BLK == 2 * GLA_CHUNK == 128
TOK0 = BLK
PAD = TOK0 - N_META

C_GQ, C_GK, C_GV, C_GG = 0, 256, 512, 1024
C_DQ, C_DK, C_DV = 1536, 2048, 2560
C_Z, C_DG, C_XBC, C_MISC = 3072, 4096, 4608, 6144
N_PROJ = 6272
DT_LANE0 = 2 * GLA_RANK

VMEM_LIMIT = 56 * 1024 * 1024
LOG2E = 1.4426950408889634


def _params(sem, limit=VMEM_LIMIT):
    return pltpu.CompilerParams(dimension_semantics=sem, vmem_limit_bytes=limit)


def _silu(x):
    return x / (1.0 + jnp.exp(-x))


def _softplus(x):
    return jnp.maximum(x, 0.0) + jnp.log(1.0 + jnp.exp(-jnp.abs(x)))


def _log_sigmoid(x):
    return jnp.minimum(x, 0.0) - jnp.log(1.0 + jnp.exp(-jnp.abs(x)))


def _split3(x):
    hi = x.astype(BF16)
    r = x - hi.astype(F32)
    mid = r.astype(BF16)
    lo = (r - mid.astype(F32)).astype(BF16)
    return hi, mid, lo


def _dot_l3(a, b_exact):
    hi, mid, lo = _split3(a)
    d = functools.partial(jnp.dot, preferred_element_type=F32)
    return d(hi, b_exact) + d(mid, b_exact) + d(lo, b_exact)


def _dot_r3(a_exact, b):
    hi, mid, lo = _split3(b)
    d = functools.partial(jnp.dot, preferred_element_type=F32)
    return d(a_exact, hi) + d(a_exact, mid) + d(a_exact, lo)


def _dot_22(a, b):
    ah = a.astype(BF16)
    al = (a - ah.astype(F32)).astype(BF16)
    bh = b.astype(BF16)
    bl = (b - bh.astype(F32)).astype(BF16)
    d = functools.partial(jnp.dot, preferred_element_type=F32)
    return d(ah, bh) + d(ah, bl) + d(al, bh)


def _iota(shape, dim):
    return lax.broadcasted_iota(jnp.int32, shape, dim)


def _rms(x, w):
    return x * lax.rsqrt(jnp.mean(x * x, axis=-1, keepdims=True) + EPS) * w


def _inproj_kernel(h_ref, nw_ref, w_ref, wm_ref, o_ref, om_ref, u_scr, *, tm):
    j = pl.program_id(2)

    @pl.when(j == 0)
    def _():
        rc = tm // 3 if tm % 48 == 0 else tm
        for r0 in range(0, tm, rc):
            rows = slice(r0, r0 + rc)
            x = h_ref[rows, :]
            row = pl.program_id(1) * tm + r0 + _iota((rc, 1), 0)
            scale = lax.rsqrt(jnp.mean(x * x, axis=-1, keepdims=True) + EPS) * jnp.where(row >= PAD, 1.0, 0.0)
            u = (x * scale * nw_ref[...]).astype(BF16)
            u_scr[rows, :] = u
            om_ref[rows, :] = jnp.dot(u, wm_ref[...], preferred_element_type=F32)
            o_ref[rows, :] = jnp.dot(u, w_ref[...], preferred_element_type=F32).astype(o_ref.dtype)

    @pl.when(j != 0)
    def _():
        o_ref[...] = jnp.dot(u_scr[...], w_ref[...], preferred_element_type=F32).astype(o_ref.dtype)


def _inproj(h, norm_w, w_p):
    B, Lp, D = h.shape
    tm = Lp // 4
    tn = C_MISC // 4
    return pl.pallas_call(
        functools.partial(_inproj_kernel, tm=tm),
        grid=(B, Lp // tm, C_MISC // tn),
        in_specs=[
            pl.BlockSpec((None, tm, D), lambda b, i, j: (b, i, 0)),
            pl.BlockSpec((1, D), lambda b, i, j: (0, 0)),
            pl.BlockSpec((D, tn), lambda b, i, j: (0, j)),
            pl.BlockSpec((D, 128), lambda b, i, j: (0, C_MISC // 128)),
        ],
        out_specs=[
            pl.BlockSpec((None, tm, tn), lambda b, i, j: (b, i, j)),
            pl.BlockSpec((None, tm, 128), lambda b, i, j: (b, i, 0)),
        ],
        out_shape=[jax.ShapeDtypeStruct((B, Lp, C_MISC), BF16),
                   jax.ShapeDtypeStruct((B, Lp, 128), F32)],
        scratch_shapes=[pltpu.VMEM((tm, D), BF16)],
        compiler_params=_params(("arbitrary", "arbitrary", "arbitrary")),
        name="inproj",
    )(h, norm_w.reshape(1, D), w_p, w_p)


def _outproj_kernel(h_ref, a_ref, b_ref, c_ref, w_ref, o_ref):
    d = functools.partial(jnp.dot, preferred_element_type=F32)
    acc = d(a_ref[...], w_ref[0:GLA_WIDTH, :])
    acc += d(b_ref[...], w_ref[GLA_WIDTH:GLA_WIDTH + DIFF_WIDTH, :])
    acc += d(c_ref[...], w_ref[GLA_WIDTH + DIFF_WIDTH:, :])
    o_ref[...] = h_ref[...] + acc


def _outproj(h, o_gla, o_diff, o_ssd, w_out):
    B, Lp, D = h.shape
    tm = Lp // 6 if Lp % 96 == 0 else Lp // 4
    tn = D
    return pl.pallas_call(
        _outproj_kernel,
        grid=(B, Lp // tm, D // tn),
        in_specs=[
            pl.BlockSpec((None, tm, tn), lambda b, i, j: (b, i, j)),
            pl.BlockSpec((None, tm, GLA_WIDTH), lambda b, i, j: (b, i, 0)),
            pl.BlockSpec((None, tm, DIFF_WIDTH), lambda b, i, j: (b, i, 0)),
            pl.BlockSpec((None, tm, SSD_WIDTH), lambda b, i, j: (b, i, 0)),
            pl.BlockSpec((D, tn), lambda b, i, j: (0, j), pipeline_mode=pl.Buffered(1)),
        ],
        out_specs=pl.BlockSpec((None, tm, tn), lambda b, i, j: (b, i, j)),
        out_shape=jax.ShapeDtypeStruct((B, Lp, D), F32),
        input_output_aliases={0: 0},
        compiler_params=_params(("arbitrary", "arbitrary", "arbitrary")),
        name="outproj",
    )(h, o_gla, o_diff, o_ssd, w_out)


def _final_kernel(h_ref, w_ref, o_ref):
    o_ref[...] = _rms(h_ref[...], w_ref[...])


def _final_norm(h, w):
    B, Lp, D = h.shape
    S = Lp - TOK0
    rb = 512 if S % 512 == 0 else BLK
    return pl.pallas_call(
        _final_kernel,
        grid=(B, S // rb),
        in_specs=[
            pl.BlockSpec((pl.Element(rb), pl.Element(D)),
                         lambda b, i: (pl.multiple_of(b * Lp + TOK0 + i * rb, BLK), 0)),
            pl.BlockSpec((1, D), lambda b, i: (0, 0)),
        ],
        out_specs=pl.BlockSpec((None, rb, D), lambda b, i: (b, i, 0)),
        out_shape=jax.ShapeDtypeStruct((B, S, D), F32),
        compiler_params=_params(("arbitrary", "arbitrary")),
        name="final_norm",
    )(h.reshape(B * Lp, D), w.reshape(1, D))


def _round_robin(gens):
    alive = list(gens)
    while alive:
        for g in list(alive):
            try:
                next(g)
            except StopIteration:
                alive.remove(g)


def _gla_block(bi, q_ref, k_ref, v_ref, gate_ref, misc_ref, wa_ref, ba_ref, nw_ref, o_ref,
               of_scr, s_scr, *, d, blk):
    C = GLA_CHUNK
    row = _iota((BLK, 1), 0)
    valid = (blk * BLK + row) >= PAD
    x = _dot_22(misc_ref[bi], wa_ref[d]) + ba_ref[d]
    yield
    g = jnp.where(valid, _log_sigmoid(x) * (LOG2E / GLA_TAU), 0.0)

    r = _iota((BLK, BLK), 0)
    c = _iota((BLK, BLK), 1)
    same = (r >= C) == (c >= C)
    tri = (same & (c <= r)) if d == 0 else (same & (c >= r))
    b = _dot_r3(jnp.where(tri, 1.0, 0.0).astype(BF16), g)
    yield
    first, second = (C - 1, 2 * C - 1) if d == 0 else (0, C)

    k = k_ref[bi].astype(F32)
    q_in = q_ref[bi].astype(F32) * (GLA_DK ** -0.5) * jnp.exp2(b)
    kT = k.T
    bT = b.T
    lane = _iota((1, BLK), 1)
    blastT = jnp.where(lane < C, bT[:, first:first + 1], bT[:, second:second + 1])
    kinT = (kT * jnp.exp2(-bT)).astype(BF16)
    koutT = kT * jnp.exp2(blastT - bT)
    vb = v_ref[bi]
    yield

    lane_k = _iota((1, GLA_HEADS * GLA_DK), 1)
    qm = [jnp.where((lane_k >= h * GLA_DK) & (lane_k < (h + 1) * GLA_DK), q_in, 0.0).astype(BF16)
          for h in range(GLA_HEADS)]
    att = [jnp.where(tri, jnp.dot(qm[h], kinT, preferred_element_type=F32), 0.0).astype(BF16)
           for h in range(GLA_HEADS)]
    yield
    o_intra = [jnp.dot(att[h], vb[:, h * GLA_DV:(h + 1) * GLA_DV], preferred_element_type=F32)
               for h in range(GLA_HEADS)]
    yield

    S = s_scr[bi]
    o_inter = [[None, None] for _ in range(GLA_HEADS)]
    for cc in ((0, 1) if d == 0 else (1, 0)):
        Sb = S.astype(BF16)
        for h in range(GLA_HEADS):
            o_inter[h][cc] = jnp.dot(qm[h][cc * C:(cc + 1) * C, :], Sb, preferred_element_type=F32)
        tot = first if cc == 0 else second
        dec = jnp.exp2(bT[:, tot:tot + 1])
        kc = jnp.where((lane >= cc * C) & (lane < (cc + 1) * C), koutT, 0.0).astype(BF16)
        upd = [jnp.dot(kc[h * GLA_DK:(h + 1) * GLA_DK, :], vb[:, h * GLA_DV:(h + 1) * GLA_DV],
                       preferred_element_type=F32) for h in range(GLA_HEADS)]
        S = S * dec + jnp.concatenate(upd, axis=0)
        yield
    s_scr[bi] = S

    o = jnp.concatenate(
        [o_intra[h] + jnp.concatenate(o_inter[h], axis=0) for h in range(GLA_HEADS)], axis=1)
    if d == 0:
        of_scr[bi, blk] = o.astype(of_scr.dtype)
    else:
        o = o + of_scr[bi, blk].astype(F32)
        gate = gate_ref[bi].astype(F32)
        nw = nw_ref[...]
        outs = []
        for h in range(GLA_HEADS):
            sl = slice(h * GLA_DV, (h + 1) * GLA_DV)
            outs.append(_rms(o[:, sl], nw) * _silu(gate[:, sl]))
        o_ref[bi] = jnp.concatenate(outs, axis=1).astype(o_ref.dtype)


def _gla_kernel(*refs, nblk, nb):
    s_scr = refs[-1]
    s = pl.program_id(1)

    @pl.when((s == 0) | (s == nblk))
    def _():
        s_scr[...] = jnp.zeros_like(s_scr)

    @pl.when(s < nblk)
    def _():
        _round_robin([_gla_block(bi, *refs, d=0, blk=s) for bi in range(nb)])

    @pl.when(s >= nblk)
    def _():
        _round_robin([_gla_block(bi, *refs, d=1, blk=2 * nblk - 1 - s) for bi in range(nb)])


def _scan_blk(s, nblk):
    return jnp.where(s < nblk, s, 2 * nblk - 1 - s)


def _scan_out_blk(s, nblk):
    return jnp.where(s < nblk, nblk - 1, 2 * nblk - 1 - s)


def _batch_slots(B, want):
    nb = want
    while B % nb:
        nb -= 1
    return nb


def _gla(proj, misc, wa_p, ba, nw):
    B, Lp, _ = proj.shape
    nblk = Lp // BLK
    nb = _batch_slots(B, 8)
    kd = GLA_HEADS * GLA_DK
    im = lambda col: (lambda b, s: (b, _scan_blk(s, nblk), col))
    return pl.pallas_call(
        functools.partial(_gla_kernel, nblk=nblk, nb=nb),
        grid=(B // nb, 2 * nblk),
        in_specs=[
            pl.BlockSpec((nb, BLK, kd), im(C_GQ // kd)),
            pl.BlockSpec((nb, BLK, kd), im(C_GK // kd)),
            pl.BlockSpec((nb, BLK, GLA_WIDTH), im(C_GV // GLA_WIDTH)),
            pl.BlockSpec((nb, BLK, GLA_WIDTH), im(C_GG // GLA_WIDTH)),
            pl.BlockSpec((nb, BLK, 128), im(0)),
            pl.BlockSpec((2, 128, kd), lambda b, s: (0, 0, 0)),
            pl.BlockSpec((2, 1, kd), lambda b, s: (0, 0, 0)),
            pl.BlockSpec((1, GLA_DV), lambda b, s: (0, 0)),
        ],
        out_specs=pl.BlockSpec((nb, BLK, GLA_WIDTH), lambda b, s: (b, _scan_out_blk(s, nblk), 0)),
        out_shape=jax.ShapeDtypeStruct((B, Lp, GLA_WIDTH), BF16),
        scratch_shapes=[
            pltpu.VMEM((nb, nblk, BLK, GLA_WIDTH), BF16),
            pltpu.VMEM((nb, kd, GLA_DV), F32),
        ],
        compiler_params=_params(("arbitrary", "arbitrary")),
        name="gla",
    )(proj, proj, proj, proj, misc, wa_p, ba, nw)


ATTN_BAND_TILES = 3
ATTN_L_FLOOR = 2.0 ** -60


def _q_halves(q_ref, bi):
    lane = _iota((1, 2 * DIFF_DQK), 1)
    q = q_ref[bi].astype(F32) * (DIFF_DQK ** -0.5 * LOG2E)
    return [jnp.where((lane >= c * DIFF_DQK) & (lane < (c + 1) * DIFF_DQK), q, 0.0).astype(BF16)
            for c in range(2)]


def _attn_slot(bi, i, q_ref, v_ref, band_ref, bmax_ref, kT_scr, kn_scr, l_scr, acc_scr, *, T, nk):
    sub = T // 128
    qb = _q_halves(q_ref, bi)
    qstack = jnp.concatenate(qb, axis=0)
    kn = kn_scr[bi]
    m = jnp.concatenate([
        jnp.sqrt(jnp.sum(x.astype(F32) ** 2, axis=-1, keepdims=True)) * kn[c:c + 1, :] + bmax_ref[...]
        for c, x in enumerate(qb)], axis=0)
    off_l = m - band_ref[0, 0:1, 0:128]
    off_r = m - band_ref[4, 0:1, 0:128]
    b0 = jnp.clip(i - 1, 0, nk - ATTN_BAND_TILES)
    padrow = jnp.where(_iota((1, 128), 1) < PAD, 3.0e4, 0.0)
    l_scr[bi] = jnp.zeros(l_scr.shape[1:], F32)
    acc_scr[bi] = jnp.zeros(acc_scr.shape[1:], F32)

    def sub_tile(j, t, off, bidx):
        cols = slice(t * 128, (t + 1) * 128)
        x = jnp.dot(qstack, kT_scr[bi, j, :, cols], preferred_element_type=F32) - off
        if bidx is not None:
            bt = band_ref[bidx, :, cols]
            x = x + jnp.concatenate([bt, bt], axis=0)
        if t == 0:
            x = x - jnp.where(j == 0, padrow, 0.0)
        e = jnp.exp2(x)
        l_scr[bi] += e
        vt = v_ref[bi, pl.ds(pl.multiple_of(j * T + t * 128, 128), 128), :]
        acc_scr[bi] += jnp.dot(e.astype(BF16), vt, preferred_element_type=F32)

    for jj in range(nk - ATTN_BAND_TILES):
        j = jj + jnp.where(jj >= b0, ATTN_BAND_TILES, 0)
        off = jnp.where(j < i, off_l, off_r)
        for t in range(sub):
            sub_tile(j, t, off, None)
            yield
    for b in range(ATTN_BAND_TILES):
        j = b0 + b
        for t in range(sub):
            sub_tile(j, t, m, j - i + 2)
            yield


def _attn_exact(bi, i, q_ref, v_ref, band_ref, kT_scr, l_scr, acc_scr, *, T, nk):
    sub = T // 128
    qb = _q_halves(q_ref, bi)
    col = _iota((1, T), 1)

    def fold(x, op):
        r = x[:, 0:128]
        for t in range(1, sub):
            r = op(r, x[:, t * 128:(t + 1) * 128])
        return r

    for c in range(2):
        for r0 in range(0, T, 128):
            rows = slice(r0, r0 + 128)
            qc = qb[c][rows]

            def scores(j):
                s = jnp.dot(qc, kT_scr[bi, j], preferred_element_type=F32)
                s = s + band_ref[jnp.clip(j - i, -2, 2) + 2, rows, :]
                return jnp.where((j == 0) & (col < PAD), -1e30, s)

            m_run = lax.fori_loop(0, nk, lambda j, mr: jnp.maximum(mr, fold(scores(j), jnp.maximum)),
                                  jnp.full((128, 128), -jnp.inf, F32))
            m = jnp.max(m_run, axis=-1, keepdims=True)

            def body(j, carry):
                l_run, acc = carry
                e = jnp.exp2(scores(j) - m)
                vt = v_ref[bi, pl.ds(pl.multiple_of(j * T, 128), T), :]
                return l_run + fold(e, jnp.add), acc + jnp.dot(e.astype(BF16), vt, preferred_element_type=F32)

            l_run, acc = lax.fori_loop(0, nk, body, (jnp.zeros((128, 128), F32), jnp.zeros((128, 128), F32)))
            l_scr[bi, c * T + r0:c * T + r0 + 128, :] = l_run
            acc_scr[bi, c * T + r0:c * T + r0 + 128, :] = acc


def _attn_kernel(q_ref, k_ref, v_ref, g_ref, band_ref, bmax_ref, lam_ref, lc_ref, nw_ref, o_ref,
                 kT_scr, kn_scr, l_scr, acc_scr, *, nb, T, nk):
    i = pl.program_id(2)

    @pl.when(i == 0)
    def _():
        lane = _iota((1, 2 * DIFF_DQK), 1)
        w = 2 * DIFF_DQK
        same_half = ((_iota((w, w), 0) < DIFF_DQK) == (_iota((w, w), 1) < DIFF_DQK))
        ones = jnp.where(same_half, 1.0, 0.0).astype(BF16)
        for bi in range(nb):
            for j in range(nk):
                kT_scr[bi, j] = k_ref[bi, j * T:(j + 1) * T, :].T
            k2 = k_ref[bi].astype(F32) ** 2
            hi = k2.astype(BF16)
            lo = (k2 - hi.astype(F32)).astype(BF16)
            n2 = (jnp.dot(hi, ones, preferred_element_type=F32)
                  + jnp.dot(lo, ones, preferred_element_type=F32))
            nmax = jnp.max(n2, axis=0, keepdims=True)
            for c in range(2):
                half = (lane >= c * DIFF_DQK) & (lane < (c + 1) * DIFF_DQK)
                kmax = jnp.sqrt(jnp.max(jnp.where(half, nmax, 0.0), axis=-1, keepdims=True)) * (1.0 + 2.0 ** -10)
                kn_scr[bi, c:c + 1, :] = jnp.broadcast_to(kmax, (1, 128))

    _round_robin([
        _attn_slot(bi, i, q_ref, v_ref, band_ref, bmax_ref, kT_scr, kn_scr, l_scr, acc_scr, T=T, nk=nk)
        for bi in range(nb)])

    lp = lam_ref[...]
    lc = lc_ref[...]
    lam = (jnp.exp(jnp.sum(lp[0:1] * lp[1:2], axis=-1, keepdims=True))
           - jnp.exp(jnp.sum(lp[2:3] * lp[3:4], axis=-1, keepdims=True)) + lc[:, 0:1])
    l_min = None
    for bi in range(nb):
        lt = jnp.min(l_scr[bi], axis=0, keepdims=True)
        l_min = lt if l_min is None else jnp.minimum(l_min, lt)

    @pl.when(jnp.min(l_min) * 128.0 < ATTN_L_FLOOR)
    def _():
        for bi in range(nb):
            _attn_exact(bi, i, q_ref, v_ref, band_ref, kT_scr, l_scr, acc_scr, T=T, nk=nk)

    for bi in range(nb):
        out = acc_scr[bi] / jnp.sum(l_scr[bi], axis=-1, keepdims=True)
        o = out[0:T] - lam * out[T:2 * T]
        y = _rms(o, nw_ref[...]) * lc[:, 1:2]
        o_ref[bi] = (y * _silu(g_ref[bi].astype(F32))).astype(o_ref.dtype)


def _attn(proj, band, bmax, diff_lambda, lcoef, nw, T):
    B, Lp, _ = proj.shape
    nk = Lp // T
    H = DIFF_HEADS
    assert nk >= ATTN_BAND_TILES
    nb = _batch_slots(B, 4)
    return pl.pallas_call(
        functools.partial(_attn_kernel, nb=nb, T=T, nk=nk),
        grid=(B // nb, H, nk),
        in_specs=[
            pl.BlockSpec((nb, T, 128), lambda b, h, i: (b, i, C_DQ // 128 + h)),
            pl.BlockSpec((nb, Lp, 128), lambda b, h, i: (b, 0, C_DK // 128 + h)),
            pl.BlockSpec((nb, Lp, 128), lambda b, h, i: (b, 0, C_DV // 128 + h)),
            pl.BlockSpec((nb, T, 128), lambda b, h, i: (b, i, C_DG // 128 + h)),
            pl.BlockSpec((None, 5, T, T), lambda b, h, i: (h, 0, 0, 0)),
            pl.BlockSpec((None, 1, 128), lambda b, h, i: (h, 0, 0)),
            pl.BlockSpec((4, DIFF_DQK), lambda b, h, i: (0, 0)),
            pl.BlockSpec((1, 128), lambda b, h, i: (0, 0)),
            pl.BlockSpec((1, DIFF_DV), lambda b, h, i: (0, 0)),
        ],
        out_specs=pl.BlockSpec((nb, T, 128), lambda b, h, i: (b, i, h)),
        out_shape=jax.ShapeDtypeStruct((B, Lp, DIFF_WIDTH), BF16),
        scratch_shapes=[
            pltpu.VMEM((nb, nk, 128, T), BF16),
            pltpu.VMEM((nb, 8, 128), F32),
            pltpu.VMEM((nb, 2 * T, 128), F32),
            pltpu.VMEM((nb, 2 * T, 128), F32),
        ],
        compiler_params=_params(("arbitrary", "arbitrary", "arbitrary")),
        name="diff_attn",
    )(proj, proj, proj, proj, band, bmax, diff_lambda, lcoef, nw)


def _bucket_table(T):
    nb = N_BUCKETS // 2
    max_exact = nb // 2
    rel = np.arange(6 * T) - (3 * T - 1)
    ret = np.where(rel > 0, nb, 0)
    n = np.abs(rel)
    nf = np.maximum(n, 1).astype(np.float64)
    large = max_exact + (np.log(nf / max_exact) / math.log(MAX_DISTANCE / max_exact)
                         * (nb - max_exact)).astype(np.int32)
    large = np.minimum(large, nb - 1)
    return (ret + np.where(n < max_exact, n, large)).astype(np.int32)


def _bias_band(rel_bias, T):
    n = 6 * T
    v = (rel_bias.astype(F32)[_bucket_table(T)] * LOG2E).T
    rows = jnp.tile(v, (1, T))[:, :T * (n - 1)].reshape(DIFF_HEADS, T, n - 1)
    band = rows[:, :, T - 1:n - 1].reshape(DIFF_HEADS, T, 5, T)
    return jnp.transpose(band, (0, 2, 1, 3))


def _conv_kernel(x_ref, w_ref, b_ref, o_ref):
    x = x_ref[...].astype(F32)
    w = w_ref[...]
    n = x.shape[0]
    half = (SSD_CONV - 1) // 2
    acc = x * w[half:half + 1, :] + b_ref[...]
    for kk in range(SSD_CONV):
        if kk != half:
            acc = acc + pltpu.roll(x, (half - kk) % n, 0) * w[kk:kk + 1, :]
    o_ref[...] = _silu(acc).astype(o_ref.dtype)


def _conv(proj, conv_w, conv_b):
    B, Lp, _ = proj.shape
    tn = 256
    return pl.pallas_call(
        _conv_kernel,
        grid=(B, SSD_CONV_DIM // tn),
        in_specs=[
            pl.BlockSpec((None, Lp, tn), lambda b, j: (b, 0, C_XBC // tn + j)),
            pl.BlockSpec((SSD_CONV, tn), lambda b, j: (0, j)),
            pl.BlockSpec((1, tn), lambda b, j: (0, j)),
        ],
        out_specs=pl.BlockSpec((None, Lp, tn), lambda b, j: (b, 0, j)),
        out_shape=jax.ShapeDtypeStruct((B, Lp, SSD_CONV_DIM), BF16),
        compiler_params=_params(("arbitrary", "arbitrary")),
        name="ssd_conv",
    )(proj, conv_w, conv_b.reshape(1, SSD_CONV_DIM))


def _ssd_block(bi, xc_ref, z_ref, misc_ref, dtb_r_ref, al_r_ref, e_ref, dskip_ref,
               nw_ref, o_ref, yf_scr, s_scr, *, d, blk):
    G, N, P = SSD_GROUPS, SSD_STATE, SSD_HEADDIM
    R = SSD_HEADS // G
    W = R * P
    row = _iota((BLK, 1), 0)
    lane = _iota((1, BLK), 1)
    dt = jnp.where((blk * BLK + row) >= PAD, _softplus(misc_ref[bi] + dtb_r_ref[...]), 0.0)
    a = dt * (-LOG2E * jnp.exp(al_r_ref[...]))
    aT = a.T

    r = _iota((BLK, BLK), 0)
    c = _iota((BLK, BLK), 1)
    tri = (c <= r) if d == 0 else (c >= r)
    triT = (r <= c) if d == 0 else (r >= c)
    cum = _dot_r3(jnp.where(tri, 1.0, 0.0).astype(BF16), a)
    cumT = _dot_l3(aT, jnp.where(triT, 1.0, 0.0).astype(BF16))
    yield

    lo = DT_LANE0 + SSD_HEADS * d
    hm = (lane >= lo) & (lane < lo + SSD_HEADS)
    last = BLK - 1 if d == 0 else 0
    tot = cum[last:last + 1, :]
    ecum = jnp.where(hm, jnp.exp2(cum), 0.0)
    ecum_hi = ecum.astype(BF16)
    stack = jnp.concatenate([
        jnp.where(hm, dt, 0.0).astype(BF16),
        jnp.where(hm, jnp.exp2(tot - cum), 0.0).astype(BF16),
        ecum_hi,
        (ecum - ecum_hi.astype(F32)).astype(BF16)], axis=0)
    yield

    for g in range(G):
        cols = slice(g * W, (g + 1) * W)
        ex = jnp.dot(stack, e_ref[d, :, cols], preferred_element_type=F32)
        dt_x = ex[0:BLK]
        toend_x = ex[BLK:2 * BLK]
        ecum_x = ex[2 * BLK:3 * BLK] + ex[3 * BLK:4 * BLK]
        etot_x = ecum_x[last:last + 1, :]
        xs = xc_ref[bi, :, cols].astype(F32)
        xd = xs * dt_x
        xdb = xd.astype(BF16)
        xdw = (xd * toend_x).astype(BF16)
        BgT = xc_ref[bi, :, SSD_WIDTH + g * N:SSD_WIDTH + (g + 1) * N].astype(F32).T.astype(BF16)
        Cg = xc_ref[bi, :, SSD_WIDTH + (G + g) * N:SSD_WIDTH + (G + g + 1) * N]
        CB = jnp.dot(Cg, BgT, preferred_element_type=F32)
        S = s_scr[bi, g]
        y_off = jnp.dot(Cg, S.astype(BF16), preferred_element_type=F32) * ecum_x
        s_scr[bi, g] = S * etot_x + jnp.dot(BgT, xdw, preferred_element_type=F32)
        yield
        y_pairs = []
        for pr in range(R // 2):
            h0 = g * R + 2 * pr
            pcols = slice(2 * pr * P, (2 * pr + 2) * P)
            xpair = xdb[:, pcols]
            acc = y_off[:, pcols]
            for hh in range(2):
                li = lo + h0 + hh
                seg = cum[:, li:li + 1] - cumT[li:li + 1, :]
                dec = jnp.exp2(jnp.where(tri, seg, -jnp.inf))
                sc = (CB * dec).astype(BF16)
                xm = jnp.where((lane >= hh * P) & (lane < (hh + 1) * P), xpair, 0.0)
                acc = acc + jnp.dot(sc, xm, preferred_element_type=F32)
            if d == 0:
                yf_scr[bi, blk, :, g * W + 2 * pr * P:g * W + (2 * pr + 2) * P] = acc.astype(yf_scr.dtype)
            else:
                y_pairs.append(acc)
            yield
        if d == 1:
            y = jnp.concatenate(y_pairs, axis=1)
            y = y + yf_scr[bi, blk, :, cols].astype(F32) + xs * dskip_ref[:, cols]
            y = y * _silu(z_ref[bi, :, cols].astype(F32))
            o_ref[bi, :, cols] = _rms(y, nw_ref[:, cols]).astype(o_ref.dtype)


def _ssd_kernel(*refs, nblk, nb):
    s_scr = refs[-1]
    s = pl.program_id(1)

    @pl.when((s == 0) | (s == nblk))
    def _():
        s_scr[...] = jnp.zeros_like(s_scr)

    @pl.when(s < nblk)
    def _():
        _round_robin([_ssd_block(bi, *refs, d=0, blk=s) for bi in range(nb)])

    @pl.when(s >= nblk)
    def _():
        _round_robin([_ssd_block(bi, *refs, d=1, blk=2 * nblk - 1 - s) for bi in range(nb)])


def _ssd(proj, misc, xc, dtb_r, al_r, E, dskip, nw):
    B, Lp, _ = proj.shape
    nblk = Lp // BLK
    nb = _batch_slots(B, 4)
    const2 = lambda b, s: (0, 0)
    return pl.pallas_call(
        functools.partial(_ssd_kernel, nblk=nblk, nb=nb),
        grid=(B // nb, 2 * nblk),
        in_specs=[
            pl.BlockSpec((nb, BLK, SSD_CONV_DIM), lambda b, s: (b, _scan_blk(s, nblk), 0)),
            pl.BlockSpec((nb, BLK, SSD_WIDTH), lambda b, s: (b, _scan_blk(s, nblk), C_Z // SSD_WIDTH)),
            pl.BlockSpec((nb, BLK, 128), lambda b, s: (b, _scan_blk(s, nblk), 0)),
            pl.BlockSpec((1, 128), const2),
            pl.BlockSpec((1, 128), const2),
            pl.BlockSpec((2, 128, SSD_WIDTH), lambda b, s: (0, 0, 0)),
            pl.BlockSpec((1, SSD_WIDTH), const2),
            pl.BlockSpec((1, SSD_WIDTH), const2),
        ],
        out_specs=pl.BlockSpec((nb, BLK, SSD_WIDTH), lambda b, s: (b, _scan_out_blk(s, nblk), 0)),
        out_shape=jax.ShapeDtypeStruct((B, Lp, SSD_WIDTH), BF16),
        scratch_shapes=[
            pltpu.VMEM((nb, nblk, BLK, SSD_WIDTH), BF16),
            pltpu.VMEM((nb, SSD_GROUPS, SSD_STATE, SSD_WIDTH // SSD_GROUPS), F32),
        ],
        compiler_params=_params(("arbitrary", "arbitrary")),
        name="ssd_scan",
    )(xc, proj, misc, dtb_r, al_r, E, dskip, nw)


def _permute_w_in(w):
    o = np.cumsum([0, 256, 256, 512, 512, 32, 512, 512, 512, 512, 1024, 1536, 32])
    gq, gk, gv, gg, gcode, dq, dk, dv, dg, z, xbc, dt, end = [int(v) for v in o]
    pieces = [w[..., gq:gcode], w[..., dq:dg], w[..., z:xbc], w[..., dg:z], w[..., xbc:dt],
              w[..., gcode:dq], w[..., dt:end],
              jnp.zeros(w.shape[:-1] + (N_PROJ - C_MISC - 64,), w.dtype)]
    return jnp.concatenate(pieces, axis=-1)


def _expansion():
    E = np.zeros((2, 128, SSD_WIDTH), np.float32)
    for d in range(2):
        for h in range(SSD_HEADS):
            E[d, DT_LANE0 + SSD_HEADS * d + h, h * SSD_HEADDIM:(h + 1) * SSD_HEADDIM] = 1.0
    return jnp.asarray(E, BF16)


def _misc_row(p):
    flat = p.reshape(p.shape[0], 1, 2 * SSD_HEADS).astype(F32)
    return jnp.pad(flat, ((0, 0), (0, 0), (DT_LANE0, 128 - DT_LANE0 - 2 * SSD_HEADS)))


def kernel(x, meta_tokens, rel_bias, final_norm_w, norm_w, w_in, w_out, gla_wa2, gla_ba, gla_norm_w,
           diff_lambda, diff_norm_w, conv_w, conv_b, ssd_A_log, ssd_dt_bias, ssd_D, ssd_norm_w):
    B, S, D = x.shape
    assert D == D_MODEL and S % BLK == 0
    Lp = TOK0 + S
    T = 384 if Lp % 384 == 0 else 128

    h = jnp.concatenate([
        jnp.zeros((B, PAD, D), x.dtype),
        jnp.broadcast_to(meta_tokens[None].astype(x.dtype), (B, N_META, D)),
        x], axis=1)

    band = _bias_band(rel_bias, T)
    bmax = jnp.broadcast_to((jnp.max(rel_bias.astype(F32), axis=0) * LOG2E)[:, None, None], (DIFF_HEADS, 1, 128))
    E = _expansion()
    lam_init = np.array([0.8 - 0.6 * math.exp(-0.3 * l) for l in range(DEPTH)], np.float32)
    lcoef = np.zeros((DEPTH, 1, 128), np.float32)
    lcoef[:, 0, 0] = lam_init
    lcoef[:, 0, 1] = 1.0 - lam_init

    kd = GLA_HEADS * GLA_DK
    wa_p = jnp.zeros((DEPTH, 2, 128, kd), F32)
    wa_p = wa_p.at[:, 0, 0:GLA_RANK].set(gla_wa2[:, 0].astype(F32))
    wa_p = wa_p.at[:, 1, GLA_RANK:2 * GLA_RANK].set(gla_wa2[:, 1].astype(F32))
    dtb_r = _misc_row(ssd_dt_bias)
    al_r = _misc_row(ssd_A_log)
    layers = dict(
        norm_w=norm_w,
        w_in=_permute_w_in(w_in).astype(BF16),
        w_out=w_out.astype(BF16),
        wa_p=wa_p,
        ba=gla_ba.reshape(DEPTH, 2, 1, kd).astype(F32),
        gla_nw=gla_norm_w.reshape(DEPTH, 1, GLA_DV).astype(F32),
        lam=diff_lambda.astype(F32),
        lcoef=jnp.asarray(lcoef),
        diff_nw=diff_norm_w.reshape(DEPTH, 1, DIFF_DV).astype(F32),
        conv_w=conv_w.astype(F32),
        conv_b=conv_b.astype(F32),
        dtb_r=dtb_r,
        al_r=al_r,
        dskip=jnp.repeat(ssd_D.astype(F32), SSD_HEADDIM, axis=-1).reshape(DEPTH, 1, SSD_WIDTH),
        ssd_nw=ssd_norm_w.reshape(DEPTH, 1, SSD_WIDTH).astype(F32),
    )

    def layer(h, p):
        proj, misc = _inproj(h, p["norm_w"], p["w_in"])
        o_gla = _gla(proj, misc, p["wa_p"], p["ba"], p["gla_nw"])
        o_diff = _attn(proj, band, bmax, p["lam"], p["lcoef"], p["diff_nw"], T)
        xc = _conv(proj, p["conv_w"], p["conv_b"])
        o_ssd = _ssd(proj, misc, xc, p["dtb_r"], p["al_r"], E, p["dskip"], p["ssd_nw"])
        return _outproj(h, o_gla, o_diff, o_ssd, p["w_out"]), None

    h, _ = lax.scan(layer, h, layers)
    return _final_norm(h, final_norm_w)
```
